```python
import math
import jax
import jax.numpy as jnp
from jax import lax
import numpy as np

D_MODEL = 1024
BATCH = 8
SEQ = 4096
DEPTH = 4

CTX_LEN = 256
GRID_W = 64
N_MIXERS = 4
MIX_W = D_MODEL
GROUP_W = MIX_W // N_MIXERS
HG_DK = 64
HG_DV = 64
HG_HEADS = GROUP_W // HG_DV
HG_QK = HG_HEADS * HG_DK
HG_V = HG_HEADS * HG_DV
HG_CHUNK = 16
S5_GROUP_CH = 16
S5_GROUPS = GROUP_W // S5_GROUP_CH
S5_STATE = 64
S5_DT_MIN = 1e-3
S5_DT_MAX = 1e-1
HY_CH = GROUP_W
HY_ORDER = 2
HY_EMB = 33
HY_HID = 64
HY_TARGET = 1e-2
HY_FAST = 0.3
HY_SLOW = 1.5
HY_MAX_DECAY = math.log(HY_TARGET) / HY_FAST
HY_MIN_DECAY = math.log(HY_TARGET) / HY_SLOW
ATT_HD = 64
ATT_HEADS = GROUP_W // ATT_HD
ATT_KV = 2
ATT_GROUP = ATT_HEADS // ATT_KV
WINDOW = 128
ATT_BLOCK = 128
ATT_SCALE = 1.0 / math.sqrt(ATT_HD)
ROPE_BASE = 10000.0
D_FF = 2816
EPS = 1e-6
IN_SIZES = (HG_QK, HG_QK, HG_QK, HG_V, HG_V, GROUP_W, 3 * HY_CH,
            ATT_HEADS * ATT_HD, ATT_KV * ATT_HD, ATT_KV * ATT_HD)
IN_COLS = sum(IN_SIZES)

kernel_name = 'hybrid_parallel_groups_diffusion_trunk'


def rms_norm(x, w):
    xf = x.astype(jnp.float32)
    y = xf * lax.rsqrt(jnp.mean(xf * xf, axis=-1, keepdims=True) + EPS)
    return (y * w.astype(jnp.float32)).astype(x.dtype)


def ada_params(cond, w, b):
    m = jnp.dot(jax.nn.silu(cond), w) + b
    return jnp.split(m[:, None, :], 6, axis=-1)


def dwconv3(x, w, b):
    xp = jnp.pad(x, ((0, 0), (1, 1), (0, 0)))
    return xp[:, :-2] * w[0] + xp[:, 1:-1] * w[1] + xp[:, 2:] * w[2] + b


def flip(t):
    return jnp.flip(t, axis=1)


def split_projection(p):
    offsets = []
    acc = 0
    for s in IN_SIZES[:-1]:
        acc += s
        offsets.append(acc)
    return jnp.split(p, offsets, axis=-1)


def gla_chunk_scan(q, k, v, log_f, s0):
    b_, L, h, _ = q.shape
    dv = v.shape[-1]
    n = L // HG_CHUNK

    def chunks(t):
        return t.reshape(b_, n, HG_CHUNK, h, t.shape[-1]).transpose(0, 1, 3, 2, 4)

    q, k, v, g = chunks(q), chunks(k), chunks(v), chunks(log_f)
    cum = jnp.cumsum(g, axis=3)
    ref = cum[:, :, :, HG_CHUNK // 2 - 1:HG_CHUNK // 2]
    last = cum[:, :, :, -1:]
    scores = jnp.einsum('bnhtk,bnhsk->bnhts', q * jnp.exp(cum - ref), k * jnp.exp(ref - cum))
    lower_tri = jnp.tril(jnp.ones((HG_CHUNK, HG_CHUNK), dtype=bool))
    scores = jnp.where(lower_tri, scores, 0.0)
    o_intra = jnp.einsum('bnhts,bnhsv->bnhtv', scores, v)
    ds = jnp.einsum('bnhsk,bnhsv->nbhkv', k * jnp.exp(last - cum), v)
    decay = jnp.exp(last[:, :, :, 0]).transpose(1, 0, 2, 3)

    def step(s, inp):
        ds_c, dec_c = inp
        return dec_c[..., None] * s + ds_c, s

    s_last, s_in = lax.scan(step, s0, (ds, decay))
    o_inter = jnp.einsum('bnhtk,nbhkv->bnhtv', q * jnp.exp(cum), s_in)
    o = (o_intra + o_inter).transpose(0, 1, 3, 2, 4).reshape(b_, L, h, dv)
    return o, s_last


def hgrn2_direction(q, f_logit, v, lb, s0):
    z = f_logit.astype(jnp.float32)
    f = lb + (1.0 - lb) * jax.nn.sigmoid(z)
    k = (1.0 - lb) * jax.nn.sigmoid(-z)
    return gla_chunk_scan(q, k, v, jnp.log(f), s0)


def hgrn2_inputs(q, ff, fb, v):
    def heads(t, d):
        return t.astype(jnp.float32).reshape(t.shape[0], t.shape[1], HG_HEADS, d)
    return jax.nn.silu(heads(q, HG_DK)), heads(ff, HG_DK), heads(fb, HG_DK), heads(v, HG_DV)


def hgrn2_bidir(q, ff, fb, v, lb_f, lb_b, s0_f, s0_b):
    o_f, s_f = hgrn2_direction(q, ff, v, lb_f, s0_f)
    o_b, s_b = hgrn2_direction(flip(q), flip(fb), flip(v), lb_b, s0_b)
    return o_f + flip(o_b), s_f, s_b


def s5_discretize(a_re, a_im, log_step, b_re, b_im):
    a_re = a_re.astype(jnp.float32)
    a_im = a_im.astype(jnp.float32)
    b_re = b_re.astype(jnp.float32)
    b_im = b_im.astype(jnp.float32)
    dt = jnp.exp(log_step.astype(jnp.float32))[:, None]
    mag = jnp.exp(a_re * dt)
    ang = a_im * dt
    ab_re, ab_im = mag * jnp.cos(ang), mag * jnp.sin(ang)
    den = a_re * a_re + a_im * a_im
    nr, ni = ab_re - 1.0, ab_im
    fr = (nr * a_re + ni * a_im) / den
    fi = (ni * a_re - nr * a_im) / den
    bb_re = fr[..., None] * b_re - fi[..., None] * b_im
    bb_im = fr[..., None] * b_im + fi[..., None] * b_re
    return ab_re, ab_im, bb_re, bb_im


def complex_affine_combine(e1, e2):
    a1r, a1i, b1r, b1i = e1
    a2r, a2i, b2r, b2i = e2
    return (a2r * a1r - a2i * a1i, a2r * a1i + a2i * a1r,
            a2r * b1r - a2i * b1i + b2r, a2r * b1i + a2i * b1r + b2i)


def s5_direction(u, disc, x0_re, x0_im):
    ab_re, ab_im, bb_re, bb_im = disc
    L = u.shape[1]
    bu_re = jnp.einsum('blgh,gph->lbgp', u, bb_re)
    bu_im = jnp.einsum('blgh,gph->lbgp', u, bb_im)
    bu_re = bu_re.at[0].add(ab_re * x0_re - ab_im * x0_im)
    bu_im = bu_im.at[0].add(ab_re * x0_im + ab_im * x0_re)
    a_re = jnp.broadcast_to(ab_re, (L, 1) + ab_re.shape)
    a_im = jnp.broadcast_to(ab_im, (L, 1) + ab_im.shape)
    _, _, xr, xi = lax.associative_scan(complex_affine_combine, (a_re, a_im, bu_re, bu_im), axis=0)
    return xr, xi


def s5_readout(xr, xi, c_re, c_im):
    return jnp.einsum('lbgp,ghp->blgh', xr, c_re) - jnp.einsum('lbgp,ghp->blgh', xi, c_im)


def s5_groups(t):
    return t.astype(jnp.float32).reshape(t.shape[0], t.shape[1], S5_GROUPS, S5_GROUP_CH)


def s5_output(u, xf, xb, c_re, c_im, d, glu_w):
    b_, L = u.shape[:2]
    c_re = c_re.astype(jnp.float32)
    c_im = c_im.astype(jnp.float32)
    y = s5_readout(xf[0], xf[1], c_re[0], c_im[0]) + flip(s5_readout(xb[0], xb[1], c_re[1], c_im[1]))
    y = y.reshape(b_, L, GROUP_W) + d.astype(jnp.float32) * u.reshape(b_, L, GROUP_W)
    z = jax.nn.gelu(y)
    return z * jax.nn.sigmoid(jnp.dot(z, glu_w.astype(jnp.float32)))


def hyena_filters(L, w1, b1, freq, w2, b2, w3):
    f32 = jnp.float32
    t01 = jnp.linspace(0.0, 1.0, L, dtype=f32)[:, None]
    w = 2.0 * math.pi * jnp.arange(L, dtype=f32)[:, None] / L
    bands = (HY_EMB - 1) // 2
    fr = jnp.linspace(1e-4, bands - 1, bands, dtype=f32)[None, :]
    z = jnp.concatenate([t01, jnp.cos(fr * w), -jnp.sin(fr * w)], axis=-1)
    h = jnp.sin(freq[0].astype(f32) * (jnp.dot(z, w1.astype(f32)) + b1.astype(f32)))
    h = jnp.sin(freq[1].astype(f32) * (jnp.dot(h, w2.astype(f32)) + b2.astype(f32)))
    h = jnp.dot(h, w3.astype(f32)).reshape(L, HY_ORDER, 2, HY_CH)
    deltas = jnp.linspace(HY_MIN_DECAY, HY_MAX_DECAY, HY_CH, dtype=f32)
    h = h * jnp.exp(-t01[:, :, None, None] * jnp.abs(deltas))
    h_f, h_b = h[:, :, 0], h[:, :, 1]
    k2 = jnp.concatenate([h_f[:1] + h_b[:1], h_f[1:], jnp.zeros_like(h_f[:1]),
                          jnp.flip(h_b[1:], axis=0)], axis=0)
    return jnp.fft.rfft(k2, axis=0)


def hyena_mixer(p, conv_w, conv_b, k_fft, bias):
    L = p.shape[1]
    p = dwconv3(p, conv_w, conv_b).astype(jnp.float32)
    x1, x2, z = jnp.split(p, 3, axis=-1)
    bias = bias.astype(jnp.float32)
    for o, gate in enumerate((x1, x2)):
        zf = jnp.fft.rfft(z, n=2 * L, axis=1)
        conv = jnp.fft.irfft(zf * k_fft[None, :, o], n=2 * L, axis=1)[:, :L]
        z = gate * (conv + bias[o] * z)
    return z


def axial_rope(L):
    rows = L // GRID_W
    row = jnp.repeat(jnp.arange(rows), GRID_W).astype(jnp.float32)
    col = jnp.tile(jnp.arange(GRID_W), rows).astype(jnp.float32)
    axis_dim = ATT_HD // 2
    inv = 1.0 / (ROPE_BASE ** (jnp.arange(0, axis_dim, 2, dtype=jnp.float32) / axis_dim))
    ang = jnp.concatenate([row[:, None] * inv, col[:, None] * inv], axis=-1)
    return jnp.cos(ang), jnp.sin(ang)


def apply_rope(x, cos, sin):
    xr = x.reshape(x.shape[:-1] + (ATT_HD // 2, 2))
    x0, x1 = xr[..., 0], xr[..., 1]
    c, s = cos[None, :, None, :], sin[None, :, None, :]
    return jnp.stack([x0 * c - x1 * s, x0 * s + x1 * c], axis=-1).reshape(x.shape)


def att_heads(t, n):
    return t.astype(jnp.float32).reshape(t.shape[0], t.shape[1], n, ATT_HD)


def context_attention(q, k, v, sink):
    b_, lc = q.shape[:2]
    qg = q.reshape(b_, lc, ATT_KV, ATT_GROUP, ATT_HD)
    s = jnp.einsum('bqkgd,bskd->bkgqs', qg, k) * ATT_SCALE
    sink_l = jnp.broadcast_to(sink.astype(jnp.float32).reshape(1, ATT_KV, ATT_GROUP, 1, 1), s.shape[:-1] + (1,))
    p = jax.nn.softmax(jnp.concatenate([sink_l, s], axis=-1), axis=-1)[..., 1:]
    o = jnp.einsum('bkgqs,bskd->bqkgd', p, v)
    return o.reshape(b_, lc, ATT_HEADS * ATT_HD)


def latent_window_attention(q, k, v, k_ctx, v_ctx, sink):
    b_, L = q.shape[:2]
    nb = L // ATT_BLOCK
    qb = q.reshape(b_, nb, ATT_BLOCK, ATT_KV, ATT_GROUP, ATT_HD)

    def band_blocks(t):
        tp = jnp.pad(t, ((0, 0), (ATT_BLOCK, ATT_BLOCK), (0, 0), (0, 0)))
        tp = tp.reshape(b_, nb + 2, ATT_BLOCK, ATT_KV, ATT_HD)
        return jnp.concatenate([tp[:, :-2], tp[:, 1:-1], tp[:, 2:]], axis=2)

    kw, vw = band_blocks(k), band_blocks(v)
    s_win = jnp.einsum('bnqkgd,bnskd->bnkgqs', qb, kw) * ATT_SCALE
    s_ctx = jnp.einsum('bnqkgd,bckd->bnkgqc', qb, k_ctx) * ATT_SCALE
    blk = jnp.arange(nb)[:, None, None] * ATT_BLOCK
    qpos = blk + jnp.arange(ATT_BLOCK)[None, :, None]
    kpos = blk - ATT_BLOCK + jnp.arange(3 * ATT_BLOCK)[None, None, :]
    valid = (jnp.abs(kpos - qpos) <= WINDOW) & (kpos >= 0) & (kpos < L)
    s_win = jnp.where(valid[None, :, None, None], s_win, -jnp.inf)
    sink_l = jnp.broadcast_to(sink.astype(jnp.float32).reshape(1, 1, ATT_KV, ATT_GROUP, 1, 1),
                              s_ctx.shape[:-1] + (1,))
    p = jax.nn.softmax(jnp.concatenate([sink_l, s_ctx, s_win], axis=-1), axis=-1)
    lc = k_ctx.shape[1]
    o = (jnp.einsum('bnkgqc,bckd->bnqkgd', p[..., 1:1 + lc], v_ctx)
         + jnp.einsum('bnkgqs,bnskd->bnqkgd', p[..., 1 + lc:], vw))
    return o.reshape(b_, L, ATT_HEADS * ATT_HD)


def merge_groups(hg, s5, hy, at, gate, norm_w):
    b_, L = s5.shape[:2]
    m = jnp.stack([hg.reshape(b_, L, HG_V), s5, hy, at], axis=2)
    m = rms_norm(m, norm_w.reshape(N_MIXERS, GROUP_W))
    hg_part = m[:, :, 0] * jax.nn.silu(gate.astype(jnp.float32))
    return jnp.concatenate([hg_part, m[:, :, 1:].reshape(b_, L, (N_MIXERS - 1) * GROUP_W)], axis=-1)


def conv_ffn(h, w_up, conv_w, conv_b, w_down):
    a, v = jnp.split(jnp.dot(h, w_up), 2, axis=-1)
    a = dwconv3(a, conv_w, conv_b)
    return jnp.dot(jax.nn.silu(a) * v, w_down)


def setup_inputs(seed: int = 0) -> dict:
    key = jax.random.key(seed)
    keys = iter(jax.random.split(key, 48))

    def normal(shape, scale):
        return jax.random.normal(next(keys), shape, jnp.float32) * scale

    def gain(shape):
        return 1.0 + normal(shape, 0.05)

    D = D_MODEL
    G, P, HS = S5_GROUPS, S5_STATE, S5_GROUP_CH
    n_idx = jnp.arange(P, dtype=jnp.float32)
    return {
        'x': normal((BATCH, SEQ, D), 1.0),
        'c': normal((BATCH, D), 1.0),
        'ctx': normal((BATCH, CTX_LEN, D), 1.0),
        'c_ctx': normal((D,), 1.0),
        'ada_w': normal((DEPTH, D, 6 * D), 0.5 * D ** -0.5),
        'ada_b': normal((DEPTH, 6 * D), 0.02),
        'norm_mix_w': gain((DEPTH, D)),
        'norm_ffn_w': gain((DEPTH, D)),
        'w_in': normal((DEPTH, D, IN_COLS), D ** -0.5),
        'hg_lb_logits': normal((DEPTH, 2, HG_QK), 0.5),
        's5_a_re': -0.5 + normal((DEPTH, 2, G, P), 0.01),
        's5_a_im': math.pi * n_idx + normal((DEPTH, 2, G, P), 0.01),
        's5_log_step': jax.random.uniform(next(keys), (DEPTH, 2, G), jnp.float32,
                                          math.log(S5_DT_MIN), math.log(S5_DT_MAX)),
        's5_b_re': normal((DEPTH, G, P, HS), (2 * HS) ** -0.5),
        's5_b_im': normal((DEPTH, G, P, HS), (2 * HS) ** -0.5),
        's5_c_re': normal((DEPTH, 2, G, HS, P), (2 * P) ** -0.5),
        's5_c_im': normal((DEPTH, 2, G, HS, P), (2 * P) ** -0.5),
        's5_d': normal((DEPTH, GROUP_W), 1.0),
        's5_glu_w': normal((DEPTH, GROUP_W, GROUP_W), GROUP_W ** -0.5),
        'hy_conv_w': normal((DEPTH, 3, 3 * HY_CH), 3 ** -0.5),
        'hy_conv_b': normal((DEPTH, 3 * HY_CH), 0.02),
        'hy_w1': normal((DEPTH, HY_EMB, HY_HID), HY_EMB ** -0.5),
        'hy_b1': normal((DEPTH, HY_HID), 0.1),
        'hy_freq': 1.0 + normal((DEPTH, 2, HY_HID), 0.05),
        'hy_w2': normal((DEPTH, HY_HID, HY_HID), HY_HID ** -0.5),
        'hy_b2': normal((DEPTH, HY_HID), 0.1),
        'hy_w3': normal((DEPTH, HY_HID, HY_ORDER * 2 * HY_CH), 0.1 * HY_HID ** -0.5),
        'hy_bias': normal((DEPTH, HY_ORDER, HY_CH), 0.5),
        'att_sink': normal((DEPTH, ATT_HEADS), 1.0),
        'merge_norm_w': gain((DEPTH, MIX_W)),
        'w_out': normal((DEPTH, MIX_W, D), MIX_W ** -0.5),
        'ffn_w_up': normal((DEPTH, D, 2 * D_FF), D ** -0.5),
        'ffn_conv_w': normal((DEPTH, 3, D_FF), 3 ** -0.5),
        'ffn_conv_b': normal((DEPTH, D_FF), 0.02),
        'ffn_w_down': normal((DEPTH, D_FF, D), D_FF ** -0.5),
        'final_norm_w': gain((D,)),
    }


def reference(x, c, ctx, c_ctx, ada_w, ada_b, norm_mix_w, norm_ffn_w, w_in, hg_lb_logits,
              s5_a_re, s5_a_im, s5_log_step, s5_b_re, s5_b_im, s5_c_re, s5_c_im, s5_d, s5_glu_w,
              hy_conv_w, hy_conv_b, hy_w1, hy_b1, hy_freq, hy_w2, hy_b2, hy_w3, hy_bias,
              att_sink, merge_norm_w, w_out, ffn_w_up, ffn_conv_w, ffn_conv_b, ffn_w_down,
              final_norm_w):
    f32 = jnp.float32
    bsz, seq_len, _ = x.shape
    ctx_len = ctx.shape[1]
    sm = jax.nn.softmax(hg_lb_logits.astype(f32), axis=0)
    lower_bounds = jnp.cumsum(sm, axis=0) - sm[0:1]
    rope_cos, rope_sin = axial_rope(seq_len)
    hg_zero = jnp.zeros((bsz, HG_HEADS, HG_DK, HG_DV), f32)
    s5_zero = jnp.zeros((bsz, S5_GROUPS, S5_STATE), f32)
    xc = ctx
    for l in range(DEPTH):
        last = l == DEPTH - 1
        sh1, sc1, g1, sh2, sc2, g2 = ada_params(c, ada_w[l], ada_b[l])
        csh1, csc1, cg1, csh2, csc2, cg2 = ada_params(c_ctx[None, :], ada_w[l], ada_b[l])
        p_lat = jnp.dot(rms_norm(x, norm_mix_w[l]) * (1.0 + sc1) + sh1, w_in[l])
        p_ctx = jnp.dot(rms_norm(xc, norm_mix_w[l]) * (1.0 + csc1) + csh1, w_in[l])
        (lq, lff, lfb, li, lg, lu, lhy, ltq, ltk, ltv) = split_projection(p_lat)
        (cq, cff, cfb, ci, cgate, cu, chy, ctq, ctk, ctv) = split_projection(p_ctx)

        lb_f = lower_bounds[l, 0].reshape(HG_HEADS, HG_DK)
        lb_b = lower_bounds[l, 1].reshape(HG_HEADS, HG_DK)
        hg_ctx, st_f, st_b = hgrn2_bidir(*hgrn2_inputs(cq, cff, cfb, ci), lb_f, lb_b, hg_zero, hg_zero)
        hg_lat, _, _ = hgrn2_bidir(*hgrn2_inputs(lq, lff, lfb, li), lb_f, lb_b, st_f, st_b)

        disc_f = s5_discretize(s5_a_re[l, 0], s5_a_im[l, 0], s5_log_step[l, 0], s5_b_re[l], s5_b_im[l])
        disc_b = s5_discretize(s5_a_re[l, 1], s5_a_im[l, 1], s5_log_step[l, 1], s5_b_re[l], s5_b_im[l])
        uc, ul = s5_groups(cu), s5_groups(lu)
        xcf = s5_direction(uc, disc_f, s5_zero, s5_zero)
        xcb = s5_direction(flip(uc), disc_b, s5_zero, s5_zero)
        xlf = s5_direction(ul, disc_f, xcf[0][-1], xcf[1][-1])
        xlb = s5_direction(flip(ul), disc_b, xcb[0][-1], xcb[1][-1])
        s5_lat = s5_output(ul, xlf, xlb, s5_c_re[l], s5_c_im[l], s5_d[l], s5_glu_w[l])

        hy_params = (hy_w1[l], hy_b1[l], hy_freq[l], hy_w2[l], hy_b2[l], hy_w3[l])
        hy_lat = hyena_mixer(lhy, hy_conv_w[l], hy_conv_b[l], hyena_filters(seq_len, *hy_params), hy_bias[l])

        tq = apply_rope(att_heads(ltq, ATT_HEADS), rope_cos, rope_sin)
        tk = apply_rope(att_heads(ltk, ATT_KV), rope_cos, rope_sin)
        tv = att_heads(ltv, ATT_KV)
        ck, cv = att_heads(ctk, ATT_KV), att_heads(ctv, ATT_KV)
        at_lat = latent_window_attention(tq, tk, tv, ck, cv, att_sink[l])

        mix_lat = merge_groups(hg_lat, s5_lat, hy_lat, at_lat, lg, merge_norm_w[l]).astype(x.dtype)
        x_new = x + g1 * jnp.dot(mix_lat, w_out[l])
        x_new = x_new + g2 * conv_ffn(rms_norm(x_new, norm_ffn_w[l]) * (1.0 + sc2) + sh2,
                                      ffn_w_up[l], ffn_conv_w[l], ffn_conv_b[l], ffn_w_down[l])
        if not last:
            s5_ctx = s5_output(uc, xcf, xcb, s5_c_re[l], s5_c_im[l], s5_d[l], s5_glu_w[l])
            hy_ctx = hyena_mixer(chy, hy_conv_w[l], hy_conv_b[l], hyena_filters(ctx_len, *hy_params), hy_bias[l])
            at_ctx = context_attention(att_heads(ctq, ATT_HEADS), ck, cv, att_sink[l])
            mix_ctx = merge_groups(hg_ctx, s5_ctx, hy_ctx, at_ctx, cgate, merge_norm_w[l]).astype(xc.dtype)
            xc = xc + cg1 * jnp.dot(mix_ctx, w_out[l])
            xc = xc + cg2 * conv_ffn(rms_norm(xc, norm_ffn_w[l]) * (1.0 + csc2) + csh2,
                                     ffn_w_up[l], ffn_conv_w[l], ffn_conv_b[l], ffn_w_down[l])
        x = x_new
    return rms_norm(x, final_norm_w)
```

```python
import functools
import math

import jax
import jax.numpy as jnp
import numpy as np
from jax import lax
from jax.experimental import pallas as pl
from jax.experimental.pallas import tpu as pltpu

F32 = jnp.float32
BF16 = jnp.bfloat16

D_MODEL = 1024
DEPTH = 4
GRID_W = 64
GROUP_W = 256
HG_DK = 64
HG_DV = 64
HG_HEADS = 4
HG_CHUNK = 16
S5_GROUP_CH = 16
S5_GROUPS = 16
S5_STATE = 64
HY_CH = 256
HY_ORDER = 2
HY_EMB = 33
HY_MAX_DECAY = math.log(1e-2) / 0.3
HY_MIN_DECAY = math.log(1e-2) / 1.5
ATT_HD = 64
ATT_HEADS = 4
ATT_KV = 2
WINDOW = 128
ATT_BLOCK = 128
ATT_SCALE = 1.0 / math.sqrt(ATT_HD)
ROPE_BASE = 10000.0
D_FF = 2816
EPS = 1e-6
IN_COLS = 2816
COL_HG = 0
COL_S5 = 1280
COL_HY = 1536
COL_TQ = 2304
COL_TK = 2560
COL_TV = 2688

FF_CHUNK = 256
N_FF_CHUNKS = D_FF // FF_CHUNK
HALO = 16
VMEM_LIMIT = 56 * 1024 * 1024


def _cparams(sem):
    return pltpu.CompilerParams(dimension_semantics=sem, vmem_limit_bytes=VMEM_LIMIT)


def _ada_kernel(c_ref, w_ref, b_ref, o_ref):
    cond = c_ref[...]
    act = cond * jax.nn.sigmoid(cond)
    o_ref[...] = jnp.dot(act.astype(BF16), w_ref[...].astype(BF16),
                         preferred_element_type=F32) + b_ref[...]


def _ada_all(cond, ada_w, ada_b):
    tn = 1024
    n6 = 6 * D_MODEL
    return pl.pallas_call(
        _ada_kernel,
        out_shape=jax.ShapeDtypeStruct((DEPTH, 16, n6), F32),
        grid=(DEPTH, n6 // tn),
        in_specs=[
            pl.BlockSpec((16, D_MODEL), lambda l, j: (0, 0)),
            pl.BlockSpec((None, D_MODEL, tn), lambda l, j: (l, 0, j)),
            pl.BlockSpec((None, 1, tn), lambda l, j: (l, 0, j)),
        ],
        out_specs=pl.BlockSpec((None, 16, tn), lambda l, j: (l, 0, j)),
        compiler_params=_cparams(("arbitrary", "arbitrary")),
        name="ada_mod",
    )(cond, ada_w, ada_b.reshape(DEPTH, 1, n6))


def _inproj_kernel(x_ref, mod_ref, nw_ref, w_ref, o_ref):
    x = x_ref[...]
    ms = jnp.mean(x * x, axis=-1, keepdims=True)
    y = x * lax.rsqrt(ms + EPS) * nw_ref[...]
    y = y * mod_ref[0:1, :] + mod_ref[1:2, :]
    o_ref[...] = jnp.dot(y.astype(BF16), w_ref[...], preferred_element_type=F32)


def _inproj(x, mods, nw, w_bf16, tm):
    b_, L, _ = x.shape
    return pl.pallas_call(
        _inproj_kernel,
        out_shape=jax.ShapeDtypeStruct((b_, L, IN_COLS), F32),
        grid=(b_, L // tm),
        in_specs=[
            pl.BlockSpec((None, tm, D_MODEL), lambda b, i: (b, i, 0)),
            pl.BlockSpec((None, 8, D_MODEL), lambda b, i: (b, 0, 0)),
            pl.BlockSpec((1, D_MODEL), lambda b, i: (0, 0)),
            pl.BlockSpec((D_MODEL, IN_COLS), lambda b, i: (0, 0)),
        ],
        out_specs=pl.BlockSpec((None, tm, IN_COLS), lambda b, i: (b, i, 0)),
        compiler_params=_cparams(("arbitrary", "arbitrary")),
        name="in_proj",
    )(x, mods, nw.reshape(1, D_MODEL), w_bf16)


def _half_swap(t):
    w = t.shape[-1]
    lane = lax.broadcasted_iota(jnp.int32, t.shape, t.ndim - 1)
    return jnp.where((lane % ATT_HD) < ATT_HD // 2,
                     pltpu.roll(t, w - ATT_HD // 2, t.ndim - 1),
                     pltpu.roll(t, ATT_HD // 2, t.ndim - 1))


def _rope(t, cos, sin):
    return t * cos + _half_swap(t) * sin


def _attend(q, keys, vals, valid, sink_ref):
    lane = lax.broadcasted_iota(jnp.int32, (1, 128), 1)
    lo = lane < ATT_HD
    kb = keys.astype(BF16)
    v_lo = jnp.where(lo, vals, 0.0).astype(BF16)
    v_hi = jnp.where(lo, 0.0, vals).astype(BF16)
    v_cat = jnp.concatenate([v_lo, v_hi], axis=0)
    outs = []
    for m in range(2):
        qm = q[:, 128 * m:128 * (m + 1)]
        probs = []
        for half in range(2):
            head = (0, 2, 1, 3)[2 * m + half]
            qh = jnp.where(lo if half == 0 else jnp.logical_not(lo), qm, 0.0).astype(BF16)
            s = lax.dot_general(qh, kb, (((1,), (1,)), ((), ())), preferred_element_type=F32)
            if valid is not None:
                s = jnp.where(valid, s, -jnp.inf)
            sink = sink_ref[head]
            mx = jnp.maximum(jnp.max(s, axis=-1, keepdims=True), sink)
            e = jnp.exp(s - mx)
            den = jnp.sum(e, axis=-1, keepdims=True) + jnp.exp(sink - mx)
            probs.append((e / den).astype(BF16))
        p_cat = jnp.concatenate(probs, axis=1)
        outs.append(jnp.dot(p_cat, v_cat, preferred_element_type=F32))
    return jnp.concatenate(outs, axis=1)


def _att_lat_kernel(sink_ref, q_ref, kp_ref, kc_ref, kn_ref, vp_ref, vc_ref, vn_ref,
                    kx_ref, vx_ref, cp_ref, cc_ref, cn_ref, sp_ref, sc_ref, sn_ref, o_ref,
                    *, seq_len):
    n = pl.program_id(1)
    cos_c, sin_c = cc_ref[...], sc_ref[...]
    q = _rope(q_ref[...], jnp.concatenate([cos_c, cos_c], axis=1),
              jnp.concatenate([sin_c, sin_c], axis=1)) * ATT_SCALE
    keys = jnp.concatenate([
        kx_ref[...],
        _rope(kp_ref[...], cp_ref[...], sp_ref[...]),
        _rope(kc_ref[...], cos_c, sin_c),
        _rope(kn_ref[...], cn_ref[...], sn_ref[...])], axis=0)
    vals = jnp.concatenate([vx_ref[...], vp_ref[...], vc_ref[...], vn_ref[...]], axis=0)
    lc = kx_ref.shape[0]
    s_tot = lc + 3 * ATT_BLOCK
    col = lax.broadcasted_iota(jnp.int32, (ATT_BLOCK, s_tot), 1)
    row = lax.broadcasted_iota(jnp.int32, (ATT_BLOCK, s_tot), 0)
    qpos = n * ATT_BLOCK + row
    kpos = (n - 1) * ATT_BLOCK + (col - lc)
    valid = (col < lc) | ((jnp.abs(kpos - qpos) <= WINDOW) & (kpos >= 0) & (kpos < seq_len))
    o_ref[...] = _attend(q, keys, vals, valid, sink_ref)


def _att_lat(p_lat, p_ctx, cos_t, sin_t, sink):
    b_, L, _ = p_lat.shape
    lc = p_ctx.shape[1]
    nb = L // ATT_BLOCK
    kcol, vcol = COL_TK // 128, COL_TV // 128
    prev = lambda b, n: jnp.maximum(n - 1, 0)
    nxt = lambda b, n: jnp.minimum(n + 1, nb - 1)
    blk = lambda c, f: pl.BlockSpec((None, ATT_BLOCK, 128), lambda b, n: (b, f(b, n), c))
    cur = lambda b, n: n
    tab = lambda f: pl.BlockSpec((ATT_BLOCK, 128), lambda b, n: (f(b, n), 0))
    return pl.pallas_call(
        functools.partial(_att_lat_kernel, seq_len=L),
        out_shape=jax.ShapeDtypeStruct((b_, L, GROUP_W), F32),
        grid=(b_, nb),
        in_specs=[
            pl.BlockSpec(memory_space=pltpu.SMEM),
            pl.BlockSpec((None, ATT_BLOCK, 256), lambda b, n: (b, n, COL_TQ // 256)),
            blk(kcol, prev), blk(kcol, cur), blk(kcol, nxt),
            blk(vcol, prev), blk(vcol, cur), blk(vcol, nxt),
            pl.BlockSpec((None, lc, 128), lambda b, n: (b, 0, kcol)),
            pl.BlockSpec((None, lc, 128), lambda b, n: (b, 0, vcol)),
            tab(prev), tab(cur), tab(nxt), tab(prev), tab(cur), tab(nxt),
        ],
        out_specs=pl.BlockSpec((None, ATT_BLOCK, GROUP_W), lambda b, n: (b, n, 0)),
        compiler_params=_cparams(("arbitrary", "arbitrary")),
        name="att_latent",
    )(sink, p_lat, p_lat, p_lat, p_lat, p_lat, p_lat, p_lat, p_ctx, p_ctx,
      cos_t, cos_t, cos_t, sin_t, sin_t, sin_t)


def _att_ctx_kernel(sink_ref, q_ref, k_ref, v_ref, o_ref):
    o_ref[...] = _attend(q_ref[...] * ATT_SCALE, k_ref[...], v_ref[...], None, sink_ref)


def _att_ctx(p_ctx, sink):
    b_, lc, _ = p_ctx.shape
    return pl.pallas_call(
        _att_ctx_kernel,
        out_shape=jax.ShapeDtypeStruct((b_, lc, GROUP_W), F32),
        grid=(b_,),
        in_specs=[
            pl.BlockSpec(memory_space=pltpu.SMEM),
            pl.BlockSpec((None, lc, 256), lambda b: (b, 0, COL_TQ // 256)),
            pl.BlockSpec((None, lc, 128), lambda b: (b, 0, COL_TK // 128)),
            pl.BlockSpec((None, lc, 128), lambda b: (b, 0, COL_TV // 128)),
        ],
        out_specs=pl.BlockSpec((None, lc, GROUP_W), lambda b: (b, 0, 0)),
        compiler_params=_cparams(("arbitrary",)),
        name="att_context",
    )(sink, p_ctx, p_ctx, p_ctx)


def _group_norm(m, w):
    return m * lax.rsqrt(jnp.mean(m * m, axis=-1, keepdims=True) + EPS) * w


def _merge_kernel(hg_ref, s5_ref, hy_ref, at_ref, gate_ref, x_ref, mod_ref, mw_ref, fw_ref,
                  wo_ref, xo_ref, ho_ref):
    gate = gate_ref[...]
    parts = [
        _group_norm(hg_ref[...], mw_ref[:, 0:256]) * (gate * jax.nn.sigmoid(gate)),
        _group_norm(s5_ref[...], mw_ref[:, 256:512]),
        _group_norm(hy_ref[...], mw_ref[:, 512:768]),
        _group_norm(at_ref[...], mw_ref[:, 768:1024]),
    ]
    mix = jnp.concatenate(parts, axis=1).astype(BF16)
    xn = x_ref[...] + mod_ref[2:3, :] * jnp.dot(mix, wo_ref[...], preferred_element_type=F32)
    xo_ref[...] = xn
    h = xn * lax.rsqrt(jnp.mean(xn * xn, axis=-1, keepdims=True) + EPS) * fw_ref[...]
    ho_ref[...] = (h * mod_ref[3:4, :] + mod_ref[4:5, :]).astype(BF16)


def _merge(hg, s5, hy, at, p, x, mods, mw, fw, wo_bf16, tm):
    b_, L, _ = x.shape
    grp = lambda: pl.BlockSpec((None, tm, GROUP_W), lambda b, i: (b, i, 0))
    return pl.pallas_call(
        _merge_kernel,
        out_shape=(jax.ShapeDtypeStruct((b_, L, D_MODEL), F32),
                   jax.ShapeDtypeStruct((b_, L, D_MODEL), BF16)),
        grid=(b_, L // tm),
        in_specs=[
            grp(), grp(), grp(), grp(),
            pl.BlockSpec((None, tm, GROUP_W), lambda b, i: (b, i, 4)),
            pl.BlockSpec((None, tm, D_MODEL), lambda b, i: (b, i, 0)),
            pl.BlockSpec((None, 8, D_MODEL), lambda b, i: (b, 0, 0)),
            pl.BlockSpec((1, D_MODEL), lambda b, i: (0, 0)),
            pl.BlockSpec((1, D_MODEL), lambda b, i: (0, 0)),
            pl.BlockSpec((D_MODEL, D_MODEL), lambda b, i: (0, 0)),
        ],
        out_specs=(pl.BlockSpec((None, tm, D_MODEL), lambda b, i: (b, i, 0)),
                   pl.BlockSpec((None, tm, D_MODEL), lambda b, i: (b, i, 0))),
        compiler_params=_cparams(("arbitrary", "arbitrary")),
        name="merge_out_proj",
    )(hg, s5, hy, at, p, x, mods, mw.reshape(1, D_MODEL), fw.reshape(1, D_MODEL), wo_bf16)


def _ffn_kernel(hp_ref, h_ref, hn_ref, x_ref, mod_ref, wa_ref, wv_ref, cw_ref, wd_ref, o_ref,
                a_scr, acc_scr, *, tm):
    i = pl.program_id(1)
    last = pl.num_programs(1) - 1
    h = h_ref[...]
    hp = jnp.where(i > 0, hp_ref[...], jnp.zeros_like(hp_ref))
    hn = jnp.where(i < last, hn_ref[...], jnp.zeros_like(hn_ref))
    h_ext = jnp.concatenate([hp, h, hn], axis=0)
    acc_scr[...] = jnp.zeros_like(acc_scr)

    def body(j, carry):
        a_scr[...] = jnp.dot(h_ext, wa_ref[j], preferred_element_type=F32)
        v = jnp.dot(h, wv_ref[j], preferred_element_type=F32)
        cw = cw_ref[j]
        a = (a_scr[HALO - 1:HALO - 1 + tm, :] * cw[0:1, :]
             + a_scr[HALO:HALO + tm, :] * cw[1:2, :]
             + a_scr[HALO + 1:HALO + 1 + tm, :] * cw[2:3, :] + cw[3:4, :])
        g = (a * jax.nn.sigmoid(a) * v).astype(BF16)
        acc_scr[...] += jnp.dot(g, wd_ref[j], preferred_element_type=F32)
        return carry

    lax.fori_loop(0, N_FF_CHUNKS, body, 0)
    o_ref[...] = x_ref[...] + mod_ref[5:6, :] * acc_scr[...]


def _ffn(h, x, mods, wa, wv, cw, wd, tm):
    b_, L, _ = x.shape
    nh = L // HALO
    r = tm // HALO
    const3 = lambda b, i: (0, 0, 0)
    return pl.pallas_call(
        functools.partial(_ffn_kernel, tm=tm),
        out_shape=jax.ShapeDtypeStruct((b_, L, D_MODEL), F32),
        grid=(b_, L // tm),
        in_specs=[
            pl.BlockSpec((None, HALO, D_MODEL), lambda b, i: (b, jnp.maximum(i * r - 1, 0), 0)),
            pl.BlockSpec((None, tm, D_MODEL), lambda b, i: (b, i, 0)),
            pl.BlockSpec((None, HALO, D_MODEL), lambda b, i: (b, jnp.minimum((i + 1) * r, nh - 1), 0)),
            pl.BlockSpec((None, tm, D_MODEL), lambda b, i: (b, i, 0)),
            pl.BlockSpec((None, 8, D_MODEL), lambda b, i: (b, 0, 0)),
            pl.BlockSpec((N_FF_CHUNKS, D_MODEL, FF_CHUNK), const3, pipeline_mode=pl.Buffered(1)),
            pl.BlockSpec((N_FF_CHUNKS, D_MODEL, FF_CHUNK), const3, pipeline_mode=pl.Buffered(1)),
            pl.BlockSpec((N_FF_CHUNKS, 8, FF_CHUNK), const3),
            pl.BlockSpec((N_FF_CHUNKS, FF_CHUNK, D_MODEL), const3, pipeline_mode=pl.Buffered(1)),
        ],
        out_specs=pl.BlockSpec((None, tm, D_MODEL), lambda b, i: (b, i, 0)),
        scratch_shapes=[pltpu.VMEM((tm + 2 * HALO, FF_CHUNK), F32),
                        pltpu.VMEM((tm, D_MODEL), F32)],
        compiler_params=_cparams(("arbitrary", "arbitrary")),
        name="conv_ffn",
    )(h, h, h, x, mods, wa, wv, cw, wd)


def _final_norm_kernel(x_ref, w_ref, o_ref):
    x = x_ref[...]
    o_ref[...] = x * lax.rsqrt(jnp.mean(x * x, axis=-1, keepdims=True) + EPS) * w_ref[...]


def _final_norm(x, w, tm):
    b_, L, _ = x.shape
    return pl.pallas_call(
        _final_norm_kernel,
        out_shape=jax.ShapeDtypeStruct(x.shape, F32),
        grid=(b_, L // tm),
        in_specs=[pl.BlockSpec((None, tm, D_MODEL), lambda b, i: (b, i, 0)),
                  pl.BlockSpec((1, D_MODEL), lambda b, i: (0, 0))],
        out_specs=pl.BlockSpec((None, tm, D_MODEL), lambda b, i: (b, i, 0)),
        compiler_params=_cparams(("arbitrary", "arbitrary")),
        name="final_norm",
    )(x, w.reshape(1, D_MODEL))


def _flip(t):
    return jnp.flip(t, axis=1)


def _gla_chunk_scan(q, k, v, log_f, s0):
    b_, L, h, _ = q.shape
    dv = v.shape[-1]
    n = L // HG_CHUNK

    def chunks(t):
        return t.reshape(b_, n, HG_CHUNK, h, t.shape[-1]).transpose(0, 1, 3, 2, 4)

    q, k, v, g = chunks(q), chunks(k), chunks(v), chunks(log_f)
    cum = jnp.cumsum(g, axis=3)
    ref = cum[:, :, :, HG_CHUNK // 2 - 1:HG_CHUNK // 2]
    last = cum[:, :, :, -1:]
    scores = jnp.einsum('bnhtk,bnhsk->bnhts', q * jnp.exp(cum - ref), k * jnp.exp(ref - cum))
    lower_tri = jnp.tril(jnp.ones((HG_CHUNK, HG_CHUNK), dtype=bool))
    scores = jnp.where(lower_tri, scores, 0.0)
    o_intra = jnp.einsum('bnhts,bnhsv->bnhtv', scores, v)
    ds = jnp.einsum('bnhsk,bnhsv->nbhkv', k * jnp.exp(last - cum), v)
    decay = jnp.exp(last[:, :, :, 0]).transpose(1, 0, 2, 3)

    def step(s, inp):
        ds_c, dec_c = inp
        return dec_c[..., None] * s + ds_c, s

    s_last, s_in = lax.scan(step, s0, (ds, decay))
    o_inter = jnp.einsum('bnhtk,nbhkv->bnhtv', q * jnp.exp(cum), s_in)
    o = (o_intra + o_inter).transpose(0, 1, 3, 2, 4).reshape(b_, L, h, dv)
    return o, s_last


def _hgrn2_direction(q, f_logit, v, lb, s0):
    z = f_logit.astype(F32)
    f = lb + (1.0 - lb) * jax.nn.sigmoid(z)
    k = (1.0 - lb) * jax.nn.sigmoid(-z)
    return _gla_chunk_scan(q, k, v, jnp.log(f), s0)


def _hgrn2_bidir(p, lb_f, lb_b, s0_f, s0_b):
    def heads(t):
        return t.reshape(t.shape[0], t.shape[1], HG_HEADS, HG_DK)
    q = jax.nn.silu(heads(p[..., 0:256]))
    ff, fb, v = heads(p[..., 256:512]), heads(p[..., 512:768]), heads(p[..., 768:1024])
    o_f, s_f = _hgrn2_direction(q, ff, v, lb_f, s0_f)
    o_b, s_b = _hgrn2_direction(_flip(q), _flip(fb), _flip(v), lb_b, s0_b)
    o = o_f + _flip(o_b)
    return o.reshape(o.shape[0], o.shape[1], GROUP_W), s_f, s_b


def _s5_discretize(a_re, a_im, log_step, b_re, b_im):
    dt = jnp.exp(log_step)[:, None]
    mag = jnp.exp(a_re * dt)
    ang = a_im * dt
    ab_re, ab_im = mag * jnp.cos(ang), mag * jnp.sin(ang)
    den = a_re * a_re + a_im * a_im
    nr, ni = ab_re - 1.0, ab_im
    fr = (nr * a_re + ni * a_im) / den
    fi = (ni * a_re - nr * a_im) / den
    bb_re = fr[..., None] * b_re - fi[..., None] * b_im
    bb_im = fr[..., None] * b_im + fi[..., None] * b_re
    return ab_re, ab_im, bb_re, bb_im


def _affine_combine(e1, e2):
    a1r, a1i, b1r, b1i = e1
    a2r, a2i, b2r, b2i = e2
    return (a2r * a1r - a2i * a1i, a2r * a1i + a2i * a1r,
            a2r * b1r - a2i * b1i + b2r, a2r * b1i + a2i * b1r + b2i)


def _s5_direction(u, disc, x0_re, x0_im):
    ab_re, ab_im, bb_re, bb_im = disc
    L = u.shape[1]
    bu_re = jnp.einsum('blgh,gph->lbgp', u, bb_re)
    bu_im = jnp.einsum('blgh,gph->lbgp', u, bb_im)
    bu_re = bu_re.at[0].add(ab_re * x0_re - ab_im * x0_im)
    bu_im = bu_im.at[0].add(ab_re * x0_im + ab_im * x0_re)
    a_re = jnp.broadcast_to(ab_re, (L, 1) + ab_re.shape)
    a_im = jnp.broadcast_to(ab_im, (L, 1) + ab_im.shape)
    _, _, xr, xi = lax.associative_scan(_affine_combine, (a_re, a_im, bu_re, bu_im), axis=0)
    return xr, xi


def _s5_readout(xr, xi, c_re, c_im):
    return jnp.einsum('lbgp,ghp->blgh', xr, c_re) - jnp.einsum('lbgp,ghp->blgh', xi, c_im)


def _s5_output(u, xf, xb, c_re, c_im, d, glu_w):
    b_, L = u.shape[:2]
    y = _s5_readout(xf[0], xf[1], c_re[0], c_im[0]) + _flip(_s5_readout(xb[0], xb[1], c_re[1], c_im[1]))
    y = y.reshape(b_, L, GROUP_W) + d * u.reshape(b_, L, GROUP_W)
    z = jax.nn.gelu(y)
    return z * jax.nn.sigmoid(jnp.dot(z, glu_w))


def _hyena_filters(L, w1, b1, freq, w2, b2, w3):
    t01 = jnp.linspace(0.0, 1.0, L, dtype=F32)[:, None]
    w = 2.0 * math.pi * jnp.arange(L, dtype=F32)[:, None] / L
    bands = (HY_EMB - 1) // 2
    fr = jnp.linspace(1e-4, bands - 1, bands, dtype=F32)[None, :]
    z = jnp.concatenate([t01, jnp.cos(fr * w), -jnp.sin(fr * w)], axis=-1)
    h = jnp.sin(freq[0] * (jnp.dot(z, w1) + b1))
    h = jnp.sin(freq[1] * (jnp.dot(h, w2) + b2))
    h = jnp.dot(h, w3).reshape(L, HY_ORDER, 2, HY_CH)
    deltas = jnp.linspace(HY_MIN_DECAY, HY_MAX_DECAY, HY_CH, dtype=F32)
    h = h * jnp.exp(-t01[:, :, None, None] * jnp.abs(deltas))
    h_f, h_b = h[:, :, 0], h[:, :, 1]
    k2 = jnp.concatenate([h_f[:1] + h_b[:1], h_f[1:], jnp.zeros_like(h_f[:1]),
                          jnp.flip(h_b[1:], axis=0)], axis=0)
    return jnp.fft.rfft(k2, axis=0)


def _dwconv3(x, w, b):
    xp = jnp.pad(x, ((0, 0), (1, 1), (0, 0)))
    return xp[:, :-2] * w[0] + xp[:, 1:-1] * w[1] + xp[:, 2:] * w[2] + b


def _hyena_mixer(p, conv_w, conv_b, k_fft, bias):
    L = p.shape[1]
    p = _dwconv3(p, conv_w, conv_b)
    x1, x2, z = jnp.split(p, 3, axis=-1)
    for o, gate in enumerate((x1, x2)):
        zf = jnp.fft.rfft(z, n=2 * L, axis=1)
        conv = jnp.fft.irfft(zf * k_fft[None, :, o], n=2 * L, axis=1)[:, :L]
        z = gate * (conv + bias[o] * z)
    return z


def _att_col_perm():
    inner = np.concatenate([np.arange(0, ATT_HD, 2), np.arange(1, ATT_HD, 2)])
    q = np.concatenate([h * ATT_HD + inner for h in (0, 2, 1, 3)])
    k = np.concatenate([h * ATT_HD + inner for h in (0, 1)])
    return np.concatenate([np.arange(COL_TQ), COL_TQ + q, COL_TK + k,
                           np.arange(COL_TV, IN_COLS)])


def _att_out_perm():
    at = np.concatenate([h * ATT_HD + np.arange(ATT_HD) for h in (0, 2, 1, 3)])
    return np.concatenate([np.arange(3 * GROUP_W), 3 * GROUP_W + at])


def _rope_tables(L):
    rows = L // GRID_W
    row = jnp.repeat(jnp.arange(rows), GRID_W).astype(F32)
    col = jnp.tile(jnp.arange(GRID_W), rows).astype(F32)
    axis_dim = ATT_HD // 2
    inv = 1.0 / (ROPE_BASE ** (jnp.arange(0, axis_dim, 2, dtype=F32) / axis_dim))
    ang = jnp.concatenate([row[:, None] * inv, col[:, None] * inv], axis=-1)
    c, s = jnp.cos(ang), jnp.sin(ang)
    return jnp.tile(jnp.concatenate([c, c], axis=1), (1, 2)), jnp.tile(jnp.concatenate([-s, s], axis=1), (1, 2))


def kernel(x, c, ctx, c_ctx, ada_w, ada_b, norm_mix_w, norm_ffn_w, w_in, hg_lb_logits, s5_a_re, s5_a_im, s5_log_step, s5_b_re, s5_b_im, s5_c_re, s5_c_im, s5_d, s5_glu_w, hy_conv_w, hy_conv_b, hy_w1, hy_b1, hy_freq, hy_w2, hy_b2, hy_w3, hy_bias, att_sink, merge_norm_w, w_out, ffn_w_up, ffn_conv_w, ffn_conv_b, ffn_w_down, final_norm_w):
    bsz, seq_len, _ = x.shape
    ctx_len = ctx.shape[1]

    cond = jnp.zeros((16, D_MODEL), F32).at[:bsz].set(c).at[bsz].set(c_ctx)
    ada = _ada_all(cond, ada_w, ada_b)

    def mods_of(m):
        sh1, sc1, g1, sh2, sc2, g2 = jnp.split(m, 6, axis=-1)
        z = jnp.zeros_like(sh1)
        return jnp.stack([1.0 + sc1, sh1, g1, 1.0 + sc2, sh2, g2, z, z], axis=1)

    sm = jax.nn.softmax(hg_lb_logits, axis=0)
    lower_bounds = jnp.cumsum(sm, axis=0) - sm[0:1]
    cos_t, sin_t = _rope_tables(seq_len)
    col_perm = _att_col_perm()
    out_perm = _att_out_perm()
    hg_zero = jnp.zeros((bsz, HG_HEADS, HG_DK, HG_DV), F32)
    s5_zero = jnp.zeros((bsz, S5_GROUPS, S5_STATE), F32)

    xc = ctx
    for l in range(DEPTH):
        last = l == DEPTH - 1
        mods_lat = mods_of(ada[l, :bsz])
        mods_ctx = jnp.broadcast_to(mods_of(ada[l, bsz:bsz + 1]), (bsz, 8, D_MODEL))
        w_in_l = w_in[l][:, col_perm].astype(BF16)
        mw = merge_norm_w[l][out_perm]
        wo = w_out[l][out_perm, :].astype(BF16)
        wa = ffn_w_up[l][:, :D_FF].reshape(D_MODEL, N_FF_CHUNKS, FF_CHUNK).transpose(1, 0, 2).astype(BF16)
        wv = ffn_w_up[l][:, D_FF:].reshape(D_MODEL, N_FF_CHUNKS, FF_CHUNK).transpose(1, 0, 2).astype(BF16)
        wd = ffn_w_down[l].reshape(N_FF_CHUNKS, FF_CHUNK, D_MODEL).astype(BF16)
        cw = jnp.concatenate([ffn_conv_w[l], ffn_conv_b[l][None], jnp.zeros((4, D_FF), F32)], axis=0)
        cw = cw.reshape(8, N_FF_CHUNKS, FF_CHUNK).transpose(1, 0, 2)

        p_lat = _inproj(x, mods_lat, norm_mix_w[l], w_in_l, 256)
        p_ctx = _inproj(xc, mods_ctx, norm_mix_w[l], w_in_l, ctx_len)

        lb_f = lower_bounds[l, 0].reshape(HG_HEADS, HG_DK)
        lb_b = lower_bounds[l, 1].reshape(HG_HEADS, HG_DK)
        hg_ctx, st_f, st_b = _hgrn2_bidir(p_ctx, lb_f, lb_b, hg_zero, hg_zero)
        hg_lat, _, _ = _hgrn2_bidir(p_lat, lb_f, lb_b, st_f, st_b)

        disc_f = _s5_discretize(s5_a_re[l, 0], s5_a_im[l, 0], s5_log_step[l, 0], s5_b_re[l], s5_b_im[l])
        disc_b = _s5_discretize(s5_a_re[l, 1], s5_a_im[l, 1], s5_log_step[l, 1], s5_b_re[l], s5_b_im[l])
        grp = lambda t: t.reshape(t.shape[0], t.shape[1], S5_GROUPS, S5_GROUP_CH)
        uc, ul = grp(p_ctx[..., COL_S5:COL_S5 + 256]), grp(p_lat[..., COL_S5:COL_S5 + 256])
        xcf = _s5_direction(uc, disc_f, s5_zero, s5_zero)
        xcb = _s5_direction(_flip(uc), disc_b, s5_zero, s5_zero)
        xlf = _s5_direction(ul, disc_f, xcf[0][-1], xcf[1][-1])
        xlb = _s5_direction(_flip(ul), disc_b, xcb[0][-1], xcb[1][-1])
        s5_lat = _s5_output(ul, xlf, xlb, s5_c_re[l], s5_c_im[l], s5_d[l], s5_glu_w[l])

        hy_params = (hy_w1[l], hy_b1[l], hy_freq[l], hy_w2[l], hy_b2[l], hy_w3[l])
        hy_lat = _hyena_mixer(p_lat[..., COL_HY:COL_HY + 768], hy_conv_w[l], hy_conv_b[l],
                              _hyena_filters(seq_len, *hy_params), hy_bias[l])

        at_lat = _att_lat(p_lat, p_ctx, cos_t, sin_t, att_sink[l])

        x_mid, h_lat = _merge(hg_lat, s5_lat, hy_lat, at_lat, p_lat, x, mods_lat, mw,
                              norm_ffn_w[l], wo, 256)
        x_new = _ffn(h_lat, x_mid, mods_lat, wa, wv, cw, wd, 512)
        if not last:
            s5_ctx = _s5_output(uc, xcf, xcb, s5_c_re[l], s5_c_im[l], s5_d[l], s5_glu_w[l])
            hy_ctx = _hyena_mixer(p_ctx[..., COL_HY:COL_HY + 768], hy_conv_w[l], hy_conv_b[l],
                                  _hyena_filters(ctx_len, *hy_params), hy_bias[l])
            at_ctx = _att_ctx(p_ctx, att_sink[l])
            xc_mid, h_ctx = _merge(hg_ctx, s5_ctx, hy_ctx, at_ctx, p_ctx, xc, mods_ctx, mw,
                                   norm_ffn_w[l], wo, ctx_len)
            xc = _ffn(h_ctx, xc_mid, mods_ctx, wa, wv, cw, wd, ctx_len)
        x = x_new
    return _final_norm(x, final_norm_w, 512)
```

```python
import functools
import math

import jax
import jax.numpy as jnp
import numpy as np
from jax import lax
from jax.experimental import pallas as pl
from jax.experimental.pallas import tpu as pltpu

F32 = jnp.float32
BF16 = jnp.bfloat16

D_MODEL = 1024
DEPTH = 4
GRID_W = 64
GROUP_W = 256
HG_DK = 64
HG_DV = 64
HG_HEADS = 4
HG_CHUNK = 16
S5_GROUP_CH = 16
S5_GROUPS = 16
S5_STATE = 64
HY_CH = 256
HY_ORDER = 2
HY_EMB = 33
HY_MAX_DECAY = math.log(1e-2) / 0.3
HY_MIN_DECAY = math.log(1e-2) / 1.5
ATT_HD = 64
ATT_HEADS = 4
ATT_KV = 2
WINDOW = 128
ATT_BLOCK = 128
ATT_SCALE = 1.0 / math.sqrt(ATT_HD)
ROPE_BASE = 10000.0
D_FF = 2816
EPS = 1e-6
IN_COLS = 2816
COL_HG = 0
COL_S5 = 1280
COL_HY = 1536
COL_TQ = 2304
COL_TK = 2560
COL_TV = 2688

FF_CHUNK = 256
N_FF_CHUNKS = D_FF // FF_CHUNK
HALO = 16
VMEM_LIMIT = 56 * 1024 * 1024


def _cparams(sem):
    return pltpu.CompilerParams(dimension_semantics=sem, vmem_limit_bytes=VMEM_LIMIT)


def _ada_kernel(c_ref, w_ref, b_ref, o_ref):
    cond = c_ref[...]
    act = cond * jax.nn.sigmoid(cond)
    o_ref[...] = jnp.dot(act.astype(BF16), w_ref[...].astype(BF16),
                         preferred_element_type=F32) + b_ref[...]


def _ada_all(cond, ada_w, ada_b):
    tn = 1024
    n6 = 6 * D_MODEL
    return pl.pallas_call(
        _ada_kernel,
        out_shape=jax.ShapeDtypeStruct((DEPTH, 16, n6), F32),
        grid=(DEPTH, n6 // tn),
        in_specs=[
            pl.BlockSpec((16, D_MODEL), lambda l, j: (0, 0)),
            pl.BlockSpec((None, D_MODEL, tn), lambda l, j: (l, 0, j)),
            pl.BlockSpec((None, 1, tn), lambda l, j: (l, 0, j)),
        ],
        out_specs=pl.BlockSpec((None, 16, tn), lambda l, j: (l, 0, j)),
        compiler_params=_cparams(("arbitrary", "arbitrary")),
        name="ada_mod",
    )(cond, ada_w, ada_b.reshape(DEPTH, 1, n6))


def _inproj_kernel(x_ref, mod_ref, nw_ref, w_ref, o_ref, u_ref):
    x = x_ref[...]
    ms = jnp.mean(x * x, axis=-1, keepdims=True)
    y = x * lax.rsqrt(ms + EPS) * nw_ref[...]
    y = y * mod_ref[0:1, :] + mod_ref[1:2, :]
    p = jnp.dot(y.astype(BF16), w_ref[...], preferred_element_type=F32)
    o_ref[...] = p
    u_ref[...] = p[:, COL_S5:COL_S5 + GROUP_W]


def _inproj(x, mods, nw, w_bf16, tm):
    b_, L, _ = x.shape
    return pl.pallas_call(
        _inproj_kernel,
        out_shape=(jax.ShapeDtypeStruct((b_, L, IN_COLS), F32),
                   jax.ShapeDtypeStruct((L, b_ * GROUP_W), F32)),
        grid=(b_, L // tm),
        in_specs=[
            pl.BlockSpec((None, tm, D_MODEL), lambda b, i: (b, i, 0)),
            pl.BlockSpec((None, 8, D_MODEL), lambda b, i: (b, 0, 0)),
            pl.BlockSpec((1, D_MODEL), lambda b, i: (0, 0)),
            pl.BlockSpec((D_MODEL, IN_COLS), lambda b, i: (0, 0)),
        ],
        out_specs=(pl.BlockSpec((None, tm, IN_COLS), lambda b, i: (b, i, 0)),
                   pl.BlockSpec((tm, GROUP_W), lambda b, i: (i, b))),
        compiler_params=_cparams(("arbitrary", "arbitrary")),
        name="in_proj",
    )(x, mods, nw.reshape(1, D_MODEL), w_bf16)


def _half_swap(t):
    w = t.shape[-1]
    lane = lax.broadcasted_iota(jnp.int32, t.shape, t.ndim - 1)
    return jnp.where((lane % ATT_HD) < ATT_HD // 2,
                     pltpu.roll(t, w - ATT_HD // 2, t.ndim - 1),
                     pltpu.roll(t, ATT_HD // 2, t.ndim - 1))


def _rope(t, cos, sin):
    return t * cos + _half_swap(t) * sin


def _attend(q, keys, vals, valid, sink_ref):
    lane = lax.broadcasted_iota(jnp.int32, (1, 128), 1)
    lo = lane < ATT_HD
    kb = keys.astype(BF16)
    v_lo = jnp.where(lo, vals, 0.0).astype(BF16)
    v_hi = jnp.where(lo, 0.0, vals).astype(BF16)
    v_cat = jnp.concatenate([v_lo, v_hi], axis=0)
    outs = []
    for m in range(2):
        qm = q[:, 128 * m:128 * (m + 1)]
        probs = []
        for half in range(2):
            head = (0, 2, 1, 3)[2 * m + half]
            qh = jnp.where(lo if half == 0 else jnp.logical_not(lo), qm, 0.0).astype(BF16)
            s = lax.dot_general(qh, kb, (((1,), (1,)), ((), ())), preferred_element_type=F32)
            if valid is not None:
                s = jnp.where(valid, s, -jnp.inf)
            sink = sink_ref[head]
            mx = jnp.maximum(jnp.max(s, axis=-1, keepdims=True), sink)
            e = jnp.exp(s - mx)
            den = jnp.sum(e, axis=-1, keepdims=True) + jnp.exp(sink - mx)
            probs.append((e / den).astype(BF16))
        p_cat = jnp.concatenate(probs, axis=1)
        outs.append(jnp.dot(p_cat, v_cat, preferred_element_type=F32))
    return jnp.concatenate(outs, axis=1)


def _att_lat_kernel(sink_ref, q_ref, kp_ref, kc_ref, kn_ref, vp_ref, vc_ref, vn_ref,
                    kx_ref, vx_ref, cp_ref, cc_ref, cn_ref, sp_ref, sc_ref, sn_ref, o_ref,
                    *, seq_len):
    n = pl.program_id(1)
    cos_c, sin_c = cc_ref[...], sc_ref[...]
    q = _rope(q_ref[...], jnp.concatenate([cos_c, cos_c], axis=1),
              jnp.concatenate([sin_c, sin_c], axis=1)) * ATT_SCALE
    keys = jnp.concatenate([
        kx_ref[...],
        _rope(kp_ref[...], cp_ref[...], sp_ref[...]),
        _rope(kc_ref[...], cos_c, sin_c),
        _rope(kn_ref[...], cn_ref[...], sn_ref[...])], axis=0)
    vals = jnp.concatenate([vx_ref[...], vp_ref[...], vc_ref[...], vn_ref[...]], axis=0)
    lc = kx_ref.shape[0]
    s_tot = lc + 3 * ATT_BLOCK
    col = lax.broadcasted_iota(jnp.int32, (ATT_BLOCK, s_tot), 1)
    row = lax.broadcasted_iota(jnp.int32, (ATT_BLOCK, s_tot), 0)
    qpos = n * ATT_BLOCK + row
    kpos = (n - 1) * ATT_BLOCK + (col - lc)
    valid = (col < lc) | ((jnp.abs(kpos - qpos) <= WINDOW) & (kpos >= 0) & (kpos < seq_len))
    o_ref[...] = _attend(q, keys, vals, valid, sink_ref)


def _att_lat(p_lat, p_ctx, cos_t, sin_t, sink):
    b_, L, _ = p_lat.shape
    lc = p_ctx.shape[1]
    nb = L // ATT_BLOCK
    kcol, vcol = COL_TK // 128, COL_TV // 128
    prev = lambda b, n: jnp.maximum(n - 1, 0)
    nxt = lambda b, n: jnp.minimum(n + 1, nb - 1)
    blk = lambda c, f: pl.BlockSpec((None, ATT_BLOCK, 128), lambda b, n: (b, f(b, n), c))
    cur = lambda b, n: n
    tab = lambda f: pl.BlockSpec((ATT_BLOCK, 128), lambda b, n: (f(b, n), 0))
    return pl.pallas_call(
        functools.partial(_att_lat_kernel, seq_len=L),
        out_shape=jax.ShapeDtypeStruct((b_, L, GROUP_W), F32),
        grid=(b_, nb),
        in_specs=[
            pl.BlockSpec(memory_space=pltpu.SMEM),
            pl.BlockSpec((None, ATT_BLOCK, 256), lambda b, n: (b, n, COL_TQ // 256)),
            blk(kcol, prev), blk(kcol, cur), blk(kcol, nxt),
            blk(vcol, prev), blk(vcol, cur), blk(vcol, nxt),
            pl.BlockSpec((None, lc, 128), lambda b, n: (b, 0, kcol)),
            pl.BlockSpec((None, lc, 128), lambda b, n: (b, 0, vcol)),
            tab(prev), tab(cur), tab(nxt), tab(prev), tab(cur), tab(nxt),
        ],
        out_specs=pl.BlockSpec((None, ATT_BLOCK, GROUP_W), lambda b, n: (b, n, 0)),
        compiler_params=_cparams(("arbitrary", "arbitrary")),
        name="att_latent",
    )(sink, p_lat, p_lat, p_lat, p_lat, p_lat, p_lat, p_lat, p_ctx, p_ctx,
      cos_t, cos_t, cos_t, sin_t, sin_t, sin_t)


def _att_ctx_kernel(sink_ref, q_ref, k_ref, v_ref, o_ref):
    o_ref[...] = _attend(q_ref[...] * ATT_SCALE, k_ref[...], v_ref[...], None, sink_ref)


def _att_ctx(p_ctx, sink):
    b_, lc, _ = p_ctx.shape
    return pl.pallas_call(
        _att_ctx_kernel,
        out_shape=jax.ShapeDtypeStruct((b_, lc, GROUP_W), F32),
        grid=(b_,),
        in_specs=[
            pl.BlockSpec(memory_space=pltpu.SMEM),
            pl.BlockSpec((None, lc, 256), lambda b: (b, 0, COL_TQ // 256)),
            pl.BlockSpec((None, lc, 128), lambda b: (b, 0, COL_TK // 128)),
            pl.BlockSpec((None, lc, 128), lambda b: (b, 0, COL_TV // 128)),
        ],
        out_specs=pl.BlockSpec((None, lc, GROUP_W), lambda b: (b, 0, 0)),
        compiler_params=_cparams(("arbitrary",)),
        name="att_context",
    )(sink, p_ctx, p_ctx, p_ctx)


S5_TT = 64
S5_NS = S5_GROUPS * S5_STATE
S5_LANES = 256


def _s5_kernel(uf_ref, ub_ref, wb_ref, wc_ref, a_ref, x0_ref, yf_ref, yb_ref, xl_ref,
               buf_f, buf_b, st):
    j = pl.program_id(0)

    @pl.when(j == 0)
    def _():
        st[...] = x0_ref[...]

    buf_f[...] = jnp.dot(uf_ref[...].astype(BF16), wb_ref[0], preferred_element_type=F32)
    buf_b[...] = jnp.dot(ub_ref[...].astype(BF16), wb_ref[1], preferred_element_type=F32)

    for cc in range(S5_NS // S5_LANES):
        re = slice(cc * S5_LANES, (cc + 1) * S5_LANES)
        im = slice(S5_NS + cc * S5_LANES, S5_NS + (cc + 1) * S5_LANES)
        arf, aif, arb, aib = a_ref[0, :, re], a_ref[1, :, re], a_ref[2, :, re], a_ref[3, :, re]

        def step(t, carry, re=re, im=im, arf=arf, aif=aif, arb=arb, aib=aib):
            xrf, xif, xrb, xib = carry
            rf = pl.multiple_of(t * 8, 8)
            rb = pl.multiple_of((S5_TT - 1 - t) * 8, 8)
            nrf = arf * xrf - aif * xif + buf_f[pl.ds(rf, 8), re]
            nif = arf * xif + aif * xrf + buf_f[pl.ds(rf, 8), im]
            nrb = arb * xrb - aib * xib + buf_b[pl.ds(rb, 8), re]
            nib = arb * xib + aib * xrb + buf_b[pl.ds(rb, 8), im]
            buf_f[pl.ds(rf, 8), re] = nrf
            buf_f[pl.ds(rf, 8), im] = nif
            buf_b[pl.ds(rb, 8), re] = nrb
            buf_b[pl.ds(rb, 8), im] = nib
            return nrf, nif, nrb, nib

        fin = lax.fori_loop(0, S5_TT, step,
                            (st[0, :, re], st[1, :, re], st[2, :, re], st[3, :, re]), unroll=4)
        for k in range(4):
            st[k, :, re] = fin[k]

    yf_ref[...] = jnp.dot(buf_f[...].astype(BF16), wc_ref[0], preferred_element_type=F32)
    yb_ref[...] = jnp.dot(buf_b[...].astype(BF16), wc_ref[1], preferred_element_type=F32)
    xl_ref[...] = st[...]


def _s5_scan(u_tm, wb, wc, a_bc, x0):
    L = u_tm.shape[0]
    rows = S5_TT * 8
    n = L // S5_TT
    u2 = u_tm.reshape(L * 8, GROUP_W)
    y_shape = jax.ShapeDtypeStruct((L * 8, GROUP_W), F32)
    yf, yb, xl = pl.pallas_call(
        _s5_kernel,
        out_shape=(y_shape, y_shape, jax.ShapeDtypeStruct((4, 8, S5_NS), F32)),
        grid=(n,),
        in_specs=[
            pl.BlockSpec((rows, GROUP_W), lambda j: (j, 0)),
            pl.BlockSpec((rows, GROUP_W), lambda j: (n - 1 - j, 0)),
            pl.BlockSpec((2, GROUP_W, 2 * S5_NS), lambda j: (0, 0, 0)),
            pl.BlockSpec((2, 2 * S5_NS, GROUP_W), lambda j: (0, 0, 0)),
            pl.BlockSpec((4, 8, S5_NS), lambda j: (0, 0, 0)),
            pl.BlockSpec((4, 8, S5_NS), lambda j: (0, 0, 0)),
        ],
        out_specs=(pl.BlockSpec((rows, GROUP_W), lambda j: (j, 0)),
                   pl.BlockSpec((rows, GROUP_W), lambda j: (n - 1 - j, 0)),
                   pl.BlockSpec((4, 8, S5_NS), lambda j: (0, 0, 0))),
        scratch_shapes=[pltpu.VMEM((rows, 2 * S5_NS), F32),
                        pltpu.VMEM((rows, 2 * S5_NS), F32),
                        pltpu.VMEM((4, 8, S5_NS), F32)],
        compiler_params=_cparams(("arbitrary",)),
        name="s5_scan",
    )(u2, u2, wb, wc, a_bc, x0)
    return yf.reshape(L, 8 * GROUP_W), yb.reshape(L, 8 * GROUP_W), xl


def _s5_weights(a_re, a_im, log_step, b_re, b_im, c_re, c_im):
    eye = jnp.eye(S5_GROUPS, dtype=F32)
    wbs, wcs, abc = [], [], []
    for d in range(2):
        ab_re, ab_im, bb_re, bb_im = _s5_discretize(a_re[d], a_im[d], log_step[d], b_re, b_im)
        wb_re = jnp.einsum('gph,gk->ghkp', bb_re, eye).reshape(GROUP_W, S5_NS)
        wb_im = jnp.einsum('gph,gk->ghkp', bb_im, eye).reshape(GROUP_W, S5_NS)
        wbs.append(jnp.concatenate([wb_re, wb_im], axis=1))
        wc_re = jnp.einsum('ghp,gk->gpkh', c_re[d], eye).reshape(S5_NS, GROUP_W)
        wc_im = jnp.einsum('ghp,gk->gpkh', c_im[d], eye).reshape(S5_NS, GROUP_W)
        wcs.append(jnp.concatenate([wc_re, -wc_im], axis=0))
        abc += [jnp.broadcast_to(ab_re.reshape(1, S5_NS), (8, S5_NS)),
                jnp.broadcast_to(ab_im.reshape(1, S5_NS), (8, S5_NS))]
    return jnp.stack(wbs).astype(BF16), jnp.stack(wcs).astype(BF16), jnp.stack(abc)


def _group_norm(m, w):
    return m * lax.rsqrt(jnp.mean(m * m, axis=-1, keepdims=True) + EPS) * w


def _merge_kernel(hg_ref, yf_ref, yb_ref, u_ref, sd_ref, glu_ref, hy_ref, at_ref, gate_ref, x_ref,
                  mod_ref, mw_ref, fw_ref, wo_ref, xo_ref, ho_ref):
    gate = gate_ref[...]
    z = jax.nn.gelu(yf_ref[...] + yb_ref[...] + sd_ref[...] * u_ref[...])
    s5 = z * jax.nn.sigmoid(jnp.dot(z.astype(BF16), glu_ref[...], preferred_element_type=F32))
    parts = [
        _group_norm(hg_ref[...], mw_ref[:, 0:256]) * (gate * jax.nn.sigmoid(gate)),
        _group_norm(s5, mw_ref[:, 256:512]),
        _group_norm(hy_ref[...], mw_ref[:, 512:768]),
        _group_norm(at_ref[...], mw_ref[:, 768:1024]),
    ]
    mix = jnp.concatenate(parts, axis=1).astype(BF16)
    xn = x_ref[...] + mod_ref[2:3, :] * jnp.dot(mix, wo_ref[...], preferred_element_type=F32)
    xo_ref[...] = xn
    h = xn * lax.rsqrt(jnp.mean(xn * xn, axis=-1, keepdims=True) + EPS) * fw_ref[...]
    ho_ref[...] = (h * mod_ref[3:4, :] + mod_ref[4:5, :]).astype(BF16)


def _merge(hg, s5_yf, s5_yb, s5_u, s5_d, glu_bf16, hy, at, p, x, mods, mw, fw, wo_bf16, tm):
    b_, L, _ = x.shape
    grp = lambda: pl.BlockSpec((None, tm, GROUP_W), lambda b, i: (b, i, 0))
    tmaj = lambda: pl.BlockSpec((tm, GROUP_W), lambda b, i: (i, b))
    return pl.pallas_call(
        _merge_kernel,
        out_shape=(jax.ShapeDtypeStruct((b_, L, D_MODEL), F32),
                   jax.ShapeDtypeStruct((b_, L, D_MODEL), BF16)),
        grid=(b_, L // tm),
        in_specs=[
            grp(), tmaj(), tmaj(), tmaj(),
            pl.BlockSpec((1, GROUP_W), lambda b, i: (0, 0)),
            pl.BlockSpec((GROUP_W, GROUP_W), lambda b, i: (0, 0)),
            grp(), grp(),
            pl.BlockSpec((None, tm, GROUP_W), lambda b, i: (b, i, 4)),
            pl.BlockSpec((None, tm, D_MODEL), lambda b, i: (b, i, 0)),
            pl.BlockSpec((None, 8, D_MODEL), lambda b, i: (b, 0, 0)),
            pl.BlockSpec((1, D_MODEL), lambda b, i: (0, 0)),
            pl.BlockSpec((1, D_MODEL), lambda b, i: (0, 0)),
            pl.BlockSpec((D_MODEL, D_MODEL), lambda b, i: (0, 0)),
        ],
        out_specs=(pl.BlockSpec((None, tm, D_MODEL), lambda b, i: (b, i, 0)),
                   pl.BlockSpec((None, tm, D_MODEL), lambda b, i: (b, i, 0))),
        compiler_params=_cparams(("arbitrary", "arbitrary")),
        name="merge_out_proj",
    )(hg, s5_yf, s5_yb, s5_u, s5_d.reshape(1, GROUP_W), glu_bf16, hy, at, p, x, mods,
      mw.reshape(1, D_MODEL), fw.reshape(1, D_MODEL), wo_bf16)


def _ffn_kernel(hp_ref, h_ref, hn_ref, x_ref, mod_ref, wa_ref, wv_ref, cw_ref, wd_ref, o_ref,
                a_scr, acc_scr, *, tm):
    i = pl.program_id(1)
    last = pl.num_programs(1) - 1
    h = h_ref[...]
    hp = jnp.where(i > 0, hp_ref[...], jnp.zeros_like(hp_ref))
    hn = jnp.where(i < last, hn_ref[...], jnp.zeros_like(hn_ref))
    h_ext = jnp.concatenate([hp, h, hn], axis=0)
    acc_scr[...] = jnp.zeros_like(acc_scr)

    def body(j, carry):
        a_scr[...] = jnp.dot(h_ext, wa_ref[j], preferred_element_type=F32)
        v = jnp.dot(h, wv_ref[j], preferred_element_type=F32)
        cw = cw_ref[j]
        a = (a_scr[HALO - 1:HALO - 1 + tm, :] * cw[0:1, :]
             + a_scr[HALO:HALO + tm, :] * cw[1:2, :]
             + a_scr[HALO + 1:HALO + 1 + tm, :] * cw[2:3, :] + cw[3:4, :])
        g = (a * jax.nn.sigmoid(a) * v).astype(BF16)
        acc_scr[...] += jnp.dot(g, wd_ref[j], preferred_element_type=F32)
        return carry

    lax.fori_loop(0, N_FF_CHUNKS, body, 0)
    o_ref[...] = x_ref[...] + mod_ref[5:6, :] * acc_scr[...]


def _ffn(h, x, mods, wa, wv, cw, wd, tm):
    b_, L, _ = x.shape
    nh = L // HALO
    r = tm // HALO
    const3 = lambda b, i: (0, 0, 0)
    return pl.pallas_call(
        functools.partial(_ffn_kernel, tm=tm),
        out_shape=jax.ShapeDtypeStruct((b_, L, D_MODEL), F32),
        grid=(b_, L // tm),
        in_specs=[
            pl.BlockSpec((None, HALO, D_MODEL), lambda b, i: (b, jnp.maximum(i * r - 1, 0), 0)),
            pl.BlockSpec((None, tm, D_MODEL), lambda b, i: (b, i, 0)),
            pl.BlockSpec((None, HALO, D_MODEL), lambda b, i: (b, jnp.minimum((i + 1) * r, nh - 1), 0)),
            pl.BlockSpec((None, tm, D_MODEL), lambda b, i: (b, i, 0)),
            pl.BlockSpec((None, 8, D_MODEL), lambda b, i: (b, 0, 0)),
            pl.BlockSpec((N_FF_CHUNKS, D_MODEL, FF_CHUNK), const3, pipeline_mode=pl.Buffered(1)),
            pl.BlockSpec((N_FF_CHUNKS, D_MODEL, FF_CHUNK), const3, pipeline_mode=pl.Buffered(1)),
            pl.BlockSpec((N_FF_CHUNKS, 8, FF_CHUNK), const3),
            pl.BlockSpec((N_FF_CHUNKS, FF_CHUNK, D_MODEL), const3, pipeline_mode=pl.Buffered(1)),
        ],
        out_specs=pl.BlockSpec((None, tm, D_MODEL), lambda b, i: (b, i, 0)),
        scratch_shapes=[pltpu.VMEM((tm + 2 * HALO, FF_CHUNK), F32),
                        pltpu.VMEM((tm, D_MODEL), F32)],
        compiler_params=_cparams(("arbitrary", "arbitrary")),
        name="conv_ffn",
    )(h, h, h, x, mods, wa, wv, cw, wd)


def _final_norm_kernel(x_ref, w_ref, o_ref):
    x = x_ref[...]
    o_ref[...] = x * lax.rsqrt(jnp.mean(x * x, axis=-1, keepdims=True) + EPS) * w_ref[...]


def _final_norm(x, w, tm):
    b_, L, _ = x.shape
    return pl.pallas_call(
        _final_norm_kernel,
        out_shape=jax.ShapeDtypeStruct(x.shape, F32),
        grid=(b_, L // tm),
        in_specs=[pl.BlockSpec((None, tm, D_MODEL), lambda b, i: (b, i, 0)),
                  pl.BlockSpec((1, D_MODEL), lambda b, i: (0, 0))],
        out_specs=pl.BlockSpec((None, tm, D_MODEL), lambda b, i: (b, i, 0)),
        compiler_params=_cparams(("arbitrary", "arbitrary")),
        name="final_norm",
    )(x, w.reshape(1, D_MODEL))


def _flip(t):
    return jnp.flip(t, axis=1)


def _gla_chunk_scan(q, k, v, log_f, s0):
    b_, L, h, _ = q.shape
    dv = v.shape[-1]
    n = L // HG_CHUNK

    def chunks(t):
        return t.reshape(b_, n, HG_CHUNK, h, t.shape[-1]).transpose(0, 1, 3, 2, 4)

    q, k, v, g = chunks(q), chunks(k), chunks(v), chunks(log_f)
    cum = jnp.cumsum(g, axis=3)
    ref = cum[:, :, :, HG_CHUNK // 2 - 1:HG_CHUNK // 2]
    last = cum[:, :, :, -1:]
    scores = jnp.einsum('bnhtk,bnhsk->bnhts', q * jnp.exp(cum - ref), k * jnp.exp(ref - cum))
    lower_tri = jnp.tril(jnp.ones((HG_CHUNK, HG_CHUNK), dtype=bool))
    scores = jnp.where(lower_tri, scores, 0.0)
    o_intra = jnp.einsum('bnhts,bnhsv->bnhtv', scores, v)
    ds = jnp.einsum('bnhsk,bnhsv->nbhkv', k * jnp.exp(last - cum), v)
    decay = jnp.exp(last[:, :, :, 0]).transpose(1, 0, 2, 3)

    def step(s, inp):
        ds_c, dec_c = inp
        return dec_c[..., None] * s + ds_c, s

    s_last, s_in = lax.scan(step, s0, (ds, decay))
    o_inter = jnp.einsum('bnhtk,nbhkv->bnhtv', q * jnp.exp(cum), s_in)
    o = (o_intra + o_inter).transpose(0, 1, 3, 2, 4).reshape(b_, L, h, dv)
    return o, s_last


def _hgrn2_direction(q, f_logit, v, lb, s0):
    z = f_logit.astype(F32)
    f = lb + (1.0 - lb) * jax.nn.sigmoid(z)
    k = (1.0 - lb) * jax.nn.sigmoid(-z)
    return _gla_chunk_scan(q, k, v, jnp.log(f), s0)


def _hgrn2_bidir(p, lb_f, lb_b, s0_f, s0_b):
    def heads(t):
        return t.reshape(t.shape[0], t.shape[1], HG_HEADS, HG_DK)
    q = jax.nn.silu(heads(p[..., 0:256]))
    ff, fb, v = heads(p[..., 256:512]), heads(p[..., 512:768]), heads(p[..., 768:1024])
    o_f, s_f = _hgrn2_direction(q, ff, v, lb_f, s0_f)
    o_b, s_b = _hgrn2_direction(_flip(q), _flip(fb), _flip(v), lb_b, s0_b)
    o = o_f + _flip(o_b)
    return o.reshape(o.shape[0], o.shape[1], GROUP_W), s_f, s_b


def _s5_discretize(a_re, a_im, log_step, b_re, b_im):
    dt = jnp.exp(log_step)[:, None]
    mag = jnp.exp(a_re * dt)
    ang = a_im * dt
    ab_re, ab_im = mag * jnp.cos(ang), mag * jnp.sin(ang)
    den = a_re * a_re + a_im * a_im
    nr, ni = ab_re - 1.0, ab_im
    fr = (nr * a_re + ni * a_im) / den
    fi = (ni * a_re - nr * a_im) / den
    bb_re = fr[..., None] * b_re - fi[..., None] * b_im
    bb_im = fr[..., None] * b_im + fi[..., None] * b_re
    return ab_re, ab_im, bb_re, bb_im


def _hyena_filters(L, w1, b1, freq, w2, b2, w3):
    t01 = jnp.linspace(0.0, 1.0, L, dtype=F32)[:, None]
    w = 2.0 * math.pi * jnp.arange(L, dtype=F32)[:, None] / L
    bands = (HY_EMB - 1) // 2
    fr = jnp.linspace(1e-4, bands - 1, bands, dtype=F32)[None, :]
    z = jnp.concatenate([t01, jnp.cos(fr * w), -jnp.sin(fr * w)], axis=-1)
    h = jnp.sin(freq[0] * (jnp.dot(z, w1) + b1))
    h = jnp.sin(freq[1] * (jnp.dot(h, w2) + b2))
    h = jnp.dot(h, w3).reshape(L, HY_ORDER, 2, HY_CH)
    deltas = jnp.linspace(HY_MIN_DECAY, HY_MAX_DECAY, HY_CH, dtype=F32)
    h = h * jnp.exp(-t01[:, :, None, None] * jnp.abs(deltas))
    h_f, h_b = h[:, :, 0], h[:, :, 1]
    k2 = jnp.concatenate([h_f[:1] + h_b[:1], h_f[1:], jnp.zeros_like(h_f[:1]),
                          jnp.flip(h_b[1:], axis=0)], axis=0)
    return jnp.fft.rfft(k2, axis=0)


def _dwconv3(x, w, b):
    xp = jnp.pad(x, ((0, 0), (1, 1), (0, 0)))
    return xp[:, :-2] * w[0] + xp[:, 1:-1] * w[1] + xp[:, 2:] * w[2] + b


def _hyena_mixer(p, conv_w, conv_b, k_fft, bias):
    L = p.shape[1]
    p = _dwconv3(p, conv_w, conv_b)
    x1, x2, z = jnp.split(p, 3, axis=-1)
    for o, gate in enumerate((x1, x2)):
        zf = jnp.fft.rfft(z, n=2 * L, axis=1)
        conv = jnp.fft.irfft(zf * k_fft[None, :, o], n=2 * L, axis=1)[:, :L]
        z = gate * (conv + bias[o] * z)
    return z


def _att_col_perm():
    inner = np.concatenate([np.arange(0, ATT_HD, 2), np.arange(1, ATT_HD, 2)])
    q = np.concatenate([h * ATT_HD + inner for h in (0, 2, 1, 3)])
    k = np.concatenate([h * ATT_HD + inner for h in (0, 1)])
    return np.concatenate([np.arange(COL_TQ), COL_TQ + q, COL_TK + k,
                           np.arange(COL_TV, IN_COLS)])


def _att_out_perm():
    at = np.concatenate([h * ATT_HD + np.arange(ATT_HD) for h in (0, 2, 1, 3)])
    return np.concatenate([np.arange(3 * GROUP_W), 3 * GROUP_W + at])


def _rope_tables(L):
    rows = L // GRID_W
    row = jnp.repeat(jnp.arange(rows), GRID_W).astype(F32)
    col = jnp.tile(jnp.arange(GRID_W), rows).astype(F32)
    axis_dim = ATT_HD // 2
    inv = 1.0 / (ROPE_BASE ** (jnp.arange(0, axis_dim, 2, dtype=F32) / axis_dim))
    ang = jnp.concatenate([row[:, None] * inv, col[:, None] * inv], axis=-1)
    c, s = jnp.cos(ang), jnp.sin(ang)
    return jnp.tile(jnp.concatenate([c, c], axis=1), (1, 2)), jnp.tile(jnp.concatenate([-s, s], axis=1), (1, 2))


def kernel(x, c, ctx, c_ctx, ada_w, ada_b, norm_mix_w, norm_ffn_w, w_in, hg_lb_logits, s5_a_re, s5_a_im, s5_log_step, s5_b_re, s5_b_im, s5_c_re, s5_c_im, s5_d, s5_glu_w, hy_conv_w, hy_conv_b, hy_w1, hy_b1, hy_freq, hy_w2, hy_b2, hy_w3, hy_bias, att_sink, merge_norm_w, w_out, ffn_w_up, ffn_conv_w, ffn_conv_b, ffn_w_down, final_norm_w):
    bsz, seq_len, _ = x.shape
    ctx_len = ctx.shape[1]

    cond = jnp.zeros((16, D_MODEL), F32).at[:bsz].set(c).at[bsz].set(c_ctx)
    ada = _ada_all(cond, ada_w, ada_b)

    def mods_of(m):
        sh1, sc1, g1, sh2, sc2, g2 = jnp.split(m, 6, axis=-1)
        z = jnp.zeros_like(sh1)
        return jnp.stack([1.0 + sc1, sh1, g1, 1.0 + sc2, sh2, g2, z, z], axis=1)

    sm = jax.nn.softmax(hg_lb_logits, axis=0)
    lower_bounds = jnp.cumsum(sm, axis=0) - sm[0:1]
    cos_t, sin_t = _rope_tables(seq_len)
    col_perm = _att_col_perm()
    out_perm = _att_out_perm()
    hg_zero = jnp.zeros((bsz, HG_HEADS, HG_DK, HG_DV), F32)
    s5_zero = jnp.zeros((4, bsz, S5_NS), F32)

    xc = ctx
    for l in range(DEPTH):
        last = l == DEPTH - 1
        mods_lat = mods_of(ada[l, :bsz])
        mods_ctx = jnp.broadcast_to(mods_of(ada[l, bsz:bsz + 1]), (bsz, 8, D_MODEL))
        w_in_l = w_in[l][:, col_perm].astype(BF16)
        mw = merge_norm_w[l][out_perm]
        wo = w_out[l][out_perm, :].astype(BF16)
        wa = ffn_w_up[l][:, :D_FF].reshape(D_MODEL, N_FF_CHUNKS, FF_CHUNK).transpose(1, 0, 2).astype(BF16)
        wv = ffn_w_up[l][:, D_FF:].reshape(D_MODEL, N_FF_CHUNKS, FF_CHUNK).transpose(1, 0, 2).astype(BF16)
        wd = ffn_w_down[l].reshape(N_FF_CHUNKS, FF_CHUNK, D_MODEL).astype(BF16)
        cw = jnp.concatenate([ffn_conv_w[l], ffn_conv_b[l][None], jnp.zeros((4, D_FF), F32)], axis=0)
        cw = cw.reshape(8, N_FF_CHUNKS, FF_CHUNK).transpose(1, 0, 2)

        p_lat, u_lat = _inproj(x, mods_lat, norm_mix_w[l], w_in_l, 256)
        p_ctx, u_ctx = _inproj(xc, mods_ctx, norm_mix_w[l], w_in_l, ctx_len)

        lb_f = lower_bounds[l, 0].reshape(HG_HEADS, HG_DK)
        lb_b = lower_bounds[l, 1].reshape(HG_HEADS, HG_DK)
        hg_ctx, st_f, st_b = _hgrn2_bidir(p_ctx, lb_f, lb_b, hg_zero, hg_zero)
        hg_lat, _, _ = _hgrn2_bidir(p_lat, lb_f, lb_b, st_f, st_b)

        s5_wb, s5_wc, s5_a = _s5_weights(s5_a_re[l], s5_a_im[l], s5_log_step[l], s5_b_re[l],
                                         s5_b_im[l], s5_c_re[l], s5_c_im[l])
        glu = s5_glu_w[l].astype(BF16)
        yf_ctx, yb_ctx, s5_state = _s5_scan(u_ctx, s5_wb, s5_wc, s5_a, s5_zero)
        yf_lat, yb_lat, _ = _s5_scan(u_lat, s5_wb, s5_wc, s5_a, s5_state)

        hy_params = (hy_w1[l], hy_b1[l], hy_freq[l], hy_w2[l], hy_b2[l], hy_w3[l])
        hy_lat = _hyena_mixer(p_lat[..., COL_HY:COL_HY + 768], hy_conv_w[l], hy_conv_b[l],
                              _hyena_filters(seq_len, *hy_params), hy_bias[l])

        at_lat = _att_lat(p_lat, p_ctx, cos_t, sin_t, att_sink[l])

        x_mid, h_lat = _merge(hg_lat, yf_lat, yb_lat, u_lat, s5_d[l], glu, hy_lat, at_lat, p_lat, x,
                              mods_lat, mw, norm_ffn_w[l], wo, 256)
        x_new = _ffn(h_lat, x_mid, mods_lat, wa, wv, cw, wd, 512)
        if not last:
            hy_ctx = _hyena_mixer(p_ctx[..., COL_HY:COL_HY + 768], hy_conv_w[l], hy_conv_b[l],
                                  _hyena_filters(ctx_len, *hy_params), hy_bias[l])
            at_ctx = _att_ctx(p_ctx, att_sink[l])
            xc_mid, h_ctx = _merge(hg_ctx, yf_ctx, yb_ctx, u_ctx, s5_d[l], glu, hy_ctx, at_ctx, p_ctx,
                                   xc, mods_ctx, mw, norm_ffn_w[l], wo, ctx_len)
            xc = _ffn(h_ctx, xc_mid, mods_ctx, wa, wv, cw, wd, ctx_len)
        x = x_new
    return _final_norm(x, final_norm_w, 512)
```

```python
import functools
import math

import jax
import jax.numpy as jnp
import numpy as np
from jax import lax
from jax.experimental import pallas as pl
from jax.experimental.pallas import tpu as pltpu

F32 = jnp.float32
BF16 = jnp.bfloat16

D_MODEL = 1024
DEPTH = 4
GRID_W = 64
GROUP_W = 256
HG_DK = 64
HG_DV = 64
HG_HEADS = 4
HG_CHUNK = 16
S5_GROUP_CH = 16
S5_GROUPS = 16
S5_STATE = 64
HY_CH = 256
HY_ORDER = 2
HY_EMB = 33
HY_MAX_DECAY = math.log(1e-2) / 0.3
HY_MIN_DECAY = math.log(1e-2) / 1.5
ATT_HD = 64
ATT_HEADS = 4
ATT_KV = 2
WINDOW = 128
ATT_BLOCK = 128
ATT_SCALE = 1.0 / math.sqrt(ATT_HD)
ROPE_BASE = 10000.0
D_FF = 2816
EPS = 1e-6
IN_COLS = 2816
COL_HG = 0
COL_S5 = 1280
COL_HY = 1536
COL_TQ = 2304
COL_TK = 2560
COL_TV = 2688

FF_CHUNK = 256
N_FF_CHUNKS = D_FF // FF_CHUNK
HALO = 16
VMEM_LIMIT = 56 * 1024 * 1024


def _cparams(sem):
    return pltpu.CompilerParams(dimension_semantics=sem, vmem_limit_bytes=VMEM_LIMIT)


def _ada_kernel(c_ref, w_ref, b_ref, o_ref):
    cond = c_ref[...]
    act = cond * jax.nn.sigmoid(cond)
    o_ref[...] = jnp.dot(act.astype(BF16), w_ref[...].astype(BF16),
                         preferred_element_type=F32) + b_ref[...]


def _ada_all(cond, ada_w, ada_b):
    tn = 1024
    n6 = 6 * D_MODEL
    return pl.pallas_call(
        _ada_kernel,
        out_shape=jax.ShapeDtypeStruct((DEPTH, 16, n6), F32),
        grid=(DEPTH, n6 // tn),
        in_specs=[
            pl.BlockSpec((16, D_MODEL), lambda l, j: (0, 0)),
            pl.BlockSpec((None, D_MODEL, tn), lambda l, j: (l, 0, j)),
            pl.BlockSpec((None, 1, tn), lambda l, j: (l, 0, j)),
        ],
        out_specs=pl.BlockSpec((None, 16, tn), lambda l, j: (l, 0, j)),
        compiler_params=_cparams(("arbitrary", "arbitrary")),
        name="ada_mod",
    )(cond, ada_w, ada_b.reshape(DEPTH, 1, n6))


def _inproj_kernel(x_ref, mod_ref, nw_ref, w_ref, wh_ref, o_ref, u_ref, zh_ref):
    x = x_ref[...]
    ms = jnp.mean(x * x, axis=-1, keepdims=True)
    y = x * lax.rsqrt(ms + EPS) * nw_ref[...]
    y = (y * mod_ref[0:1, :] + mod_ref[1:2, :]).astype(BF16)
    p = jnp.dot(y, w_ref[...], preferred_element_type=F32)
    o_ref[...] = p
    u_ref[...] = p[:, COL_S5:COL_S5 + GROUP_W]
    zt = lax.dot_general(wh_ref[...], y, _NT, preferred_element_type=F32)
    for s in range(zh_ref.shape[0]):
        zh_ref[s] = zt[:, 128 * s:128 * (s + 1)]


def _inproj(x, mods, nw, w_bf16, wh_bf16, tm):
    b_, L, _ = x.shape
    return pl.pallas_call(
        _inproj_kernel,
        out_shape=(jax.ShapeDtypeStruct((b_, L, IN_COLS), F32),
                   jax.ShapeDtypeStruct((L, b_ * GROUP_W), F32),
                   jax.ShapeDtypeStruct((b_, L // 128, 3 * HY_CH, 128), F32)),
        grid=(b_, L // tm),
        in_specs=[
            pl.BlockSpec((None, tm, D_MODEL), lambda b, i: (b, i, 0)),
            pl.BlockSpec((None, 8, D_MODEL), lambda b, i: (b, 0, 0)),
            pl.BlockSpec((1, D_MODEL), lambda b, i: (0, 0)),
            pl.BlockSpec((D_MODEL, IN_COLS), lambda b, i: (0, 0)),
            pl.BlockSpec((3 * HY_CH, D_MODEL), lambda b, i: (0, 0)),
        ],
        out_specs=(pl.BlockSpec((None, tm, IN_COLS), lambda b, i: (b, i, 0)),
                   pl.BlockSpec((tm, GROUP_W), lambda b, i: (i, b)),
                   pl.BlockSpec((None, tm // 128, 3 * HY_CH, 128), lambda b, i: (b, i, 0, 0))),
        compiler_params=_cparams(("arbitrary", "arbitrary")),
        name="in_proj",
    )(x, mods, nw.reshape(1, D_MODEL), w_bf16, wh_bf16)


def _half_swap(t):
    w = t.shape[-1]
    lane = lax.broadcasted_iota(jnp.int32, t.shape, t.ndim - 1)
    return jnp.where((lane % ATT_HD) < ATT_HD // 2,
                     pltpu.roll(t, w - ATT_HD // 2, t.ndim - 1),
                     pltpu.roll(t, ATT_HD // 2, t.ndim - 1))


def _rope(t, cos, sin):
    return t * cos + _half_swap(t) * sin


def _attend(q, keys, vals, valid, sink_ref):
    lane = lax.broadcasted_iota(jnp.int32, (1, 128), 1)
    lo = lane < ATT_HD
    kb = keys.astype(BF16)
    v_lo = jnp.where(lo, vals, 0.0).astype(BF16)
    v_hi = jnp.where(lo, 0.0, vals).astype(BF16)
    v_cat = jnp.concatenate([v_lo, v_hi], axis=0)
    outs = []
    for m in range(2):
        qm = q[:, 128 * m:128 * (m + 1)]
        probs = []
        for half in range(2):
            head = (0, 2, 1, 3)[2 * m + half]
            qh = jnp.where(lo if half == 0 else jnp.logical_not(lo), qm, 0.0).astype(BF16)
            s = lax.dot_general(qh, kb, (((1,), (1,)), ((), ())), preferred_element_type=F32)
            if valid is not None:
                s = jnp.where(valid, s, -jnp.inf)
            sink = sink_ref[head]
            mx = jnp.maximum(jnp.max(s, axis=-1, keepdims=True), sink)
            e = jnp.exp(s - mx)
            den = jnp.sum(e, axis=-1, keepdims=True) + jnp.exp(sink - mx)
            probs.append((e / den).astype(BF16))
        p_cat = jnp.concatenate(probs, axis=1)
        outs.append(jnp.dot(p_cat, v_cat, preferred_element_type=F32))
    return jnp.concatenate(outs, axis=1)


def _att_lat_kernel(sink_ref, q_ref, kp_ref, kc_ref, kn_ref, vp_ref, vc_ref, vn_ref,
                    kx_ref, vx_ref, cp_ref, cc_ref, cn_ref, sp_ref, sc_ref, sn_ref, o_ref,
                    *, seq_len):
    n = pl.program_id(1)
    cos_c, sin_c = cc_ref[...], sc_ref[...]
    q = _rope(q_ref[...], jnp.concatenate([cos_c, cos_c], axis=1),
              jnp.concatenate([sin_c, sin_c], axis=1)) * ATT_SCALE
    keys = jnp.concatenate([
        kx_ref[...],
        _rope(kp_ref[...], cp_ref[...], sp_ref[...]),
        _rope(kc_ref[...], cos_c, sin_c),
        _rope(kn_ref[...], cn_ref[...], sn_ref[...])], axis=0)
    vals = jnp.concatenate([vx_ref[...], vp_ref[...], vc_ref[...], vn_ref[...]], axis=0)
    lc = kx_ref.shape[0]
    s_tot = lc + 3 * ATT_BLOCK
    col = lax.broadcasted_iota(jnp.int32, (ATT_BLOCK, s_tot), 1)
    row = lax.broadcasted_iota(jnp.int32, (ATT_BLOCK, s_tot), 0)
    qpos = n * ATT_BLOCK + row
    kpos = (n - 1) * ATT_BLOCK + (col - lc)
    valid = (col < lc) | ((jnp.abs(kpos - qpos) <= WINDOW) & (kpos >= 0) & (kpos < seq_len))
    o_ref[...] = _attend(q, keys, vals, valid, sink_ref)


def _att_lat(p_lat, p_ctx, cos_t, sin_t, sink):
    b_, L, _ = p_lat.shape
    lc = p_ctx.shape[1]
    nb = L // ATT_BLOCK
    kcol, vcol = COL_TK // 128, COL_TV // 128
    prev = lambda b, n: jnp.maximum(n - 1, 0)
    nxt = lambda b, n: jnp.minimum(n + 1, nb - 1)
    blk = lambda c, f: pl.BlockSpec((None, ATT_BLOCK, 128), lambda b, n: (b, f(b, n), c))
    cur = lambda b, n: n
    tab = lambda f: pl.BlockSpec((ATT_BLOCK, 128), lambda b, n: (f(b, n), 0))
    return pl.pallas_call(
        functools.partial(_att_lat_kernel, seq_len=L),
        out_shape=jax.ShapeDtypeStruct((b_, L, GROUP_W), F32),
        grid=(b_, nb),
        in_specs=[
            pl.BlockSpec(memory_space=pltpu.SMEM),
            pl.BlockSpec((None, ATT_BLOCK, 256), lambda b, n: (b, n, COL_TQ // 256)),
            blk(kcol, prev), blk(kcol, cur), blk(kcol, nxt),
            blk(vcol, prev), blk(vcol, cur), blk(vcol, nxt),
            pl.BlockSpec((None, lc, 128), lambda b, n: (b, 0, kcol)),
            pl.BlockSpec((None, lc, 128), lambda b, n: (b, 0, vcol)),
            tab(prev), tab(cur), tab(nxt), tab(prev), tab(cur), tab(nxt),
        ],
        out_specs=pl.BlockSpec((None, ATT_BLOCK, GROUP_W), lambda b, n: (b, n, 0)),
        compiler_params=_cparams(("arbitrary", "arbitrary")),
        name="att_latent",
    )(sink, p_lat, p_lat, p_lat, p_lat, p_lat, p_lat, p_lat, p_ctx, p_ctx,
      cos_t, cos_t, cos_t, sin_t, sin_t, sin_t)


def _att_ctx_kernel(sink_ref, q_ref, k_ref, v_ref, o_ref):
    o_ref[...] = _attend(q_ref[...] * ATT_SCALE, k_ref[...], v_ref[...], None, sink_ref)


def _att_ctx(p_ctx, sink):
    b_, lc, _ = p_ctx.shape
    return pl.pallas_call(
        _att_ctx_kernel,
        out_shape=jax.ShapeDtypeStruct((b_, lc, GROUP_W), F32),
        grid=(b_,),
        in_specs=[
            pl.BlockSpec(memory_space=pltpu.SMEM),
            pl.BlockSpec((None, lc, 256), lambda b: (b, 0, COL_TQ // 256)),
            pl.BlockSpec((None, lc, 128), lambda b: (b, 0, COL_TK // 128)),
            pl.BlockSpec((None, lc, 128), lambda b: (b, 0, COL_TV // 128)),
        ],
        out_specs=pl.BlockSpec((None, lc, GROUP_W), lambda b: (b, 0, 0)),
        compiler_params=_cparams(("arbitrary",)),
        name="att_context",
    )(sink, p_ctx, p_ctx, p_ctx)


HG_T = 128
HG_LEVELS = (64, 32, 16)
_NT = (((1,), (1,)), ((), ()))
_TN = (((0,), (0,)), ((), ()))


def _row_fill(ref, rows, blk):
    w = ref.shape[-1]
    return jnp.concatenate([jnp.broadcast_to(ref[r:r + 1, :], (blk, w)) for r in rows], axis=0)


def _gla_block(q_raw, z, v, lb, st_ref, cum_scr, rev):
    t = HG_T
    sg = jax.nn.sigmoid(z)
    g = jnp.log(lb + (1.0 - lb) * sg)
    k = (1.0 - lb) * jax.nn.sigmoid(-z)
    q = q_raw * jax.nn.sigmoid(q_raw)

    g_hi = g.astype(BF16)
    r1 = g - g_hi.astype(F32)
    g_mid = r1.astype(BF16)
    g_lo = (r1 - g_mid.astype(F32)).astype(BF16)
    ti = lax.broadcasted_iota(jnp.int32, (t, t), 0)
    si = lax.broadcasted_iota(jnp.int32, (t, t), 1)
    tri = jnp.where((si >= ti) if rev else (si <= ti), 1.0, 0.0).astype(BF16)
    c3 = jnp.dot(tri, jnp.concatenate([g_hi, g_mid, g_lo], axis=1), preferred_element_type=F32)
    cum = c3[:, 0:256] + c3[:, 256:512] + c3[:, 512:768]
    cum_scr[...] = cum
    edge = 0 if rev else t - 1
    g_tot = cum_scr[edge:edge + 1, :]

    lane = lax.broadcasted_iota(jnp.int32, (1, 128), 1)
    lo = lane < HG_DK
    rowi = lax.broadcasted_iota(jnp.int32, (t, 1), 0)
    vi = lax.broadcasted_iota(jnp.int32, (128, 128), 0)
    ki = lax.broadcasted_iota(jnp.int32, (128, 128), 1)
    same_head = (vi < HG_DV) == (ki < HG_DK)

    def scores(qt, kt):
        q2 = jnp.concatenate([jnp.where(lo, qt, 0.0), jnp.where(lo, 0.0, qt)], axis=0).astype(BF16)
        return lax.dot_general(q2, kt.astype(BF16), _NT, preferred_element_type=F32)

    outs = []
    for pr in range(2):
        sl = slice(128 * pr, 128 * (pr + 1))
        qp, kp, vp, cp, gp = q[:, sl], k[:, sl], v[:, sl], cum[:, sl], g_tot[:, sl]
        a0 = jnp.zeros((t, t), F32)
        a1 = jnp.zeros((t, t), F32)
        for sh, h in zip((6, 5, 4), HG_LEVELS):
            rows = [b * 2 * h + (h if rev else h - 1) for b in range(t // (2 * h))]
            e = jnp.exp(-jnp.abs(cp - _row_fill(cum_scr.at[:, sl], rows, 2 * h)))
            odd = (jnp.right_shift(rowi, sh) & 1) == 1
            q_on = jnp.logical_not(odd) if rev else odd
            sc = scores(jnp.where(q_on, qp * e, 0.0), jnp.where(q_on, 0.0, kp * e))
            if 2 * h < t:
                keep = jnp.right_shift(ti, sh + 1) == jnp.right_shift(si, sh + 1)
                a0 += jnp.where(keep, sc[:t], 0.0)
                a1 += jnp.where(keep, sc[t:], 0.0)
            else:
                a0 += sc[:t]
                a1 += sc[t:]
        rows = [b * HG_CHUNK + (HG_CHUNK // 2 if rev else HG_CHUNK // 2 - 1) for b in range(t // HG_CHUNK)]
        dd = cp - _row_fill(cum_scr.at[:, sl], rows, HG_CHUNK)
        sc = scores(qp * jnp.exp(dd), kp * jnp.exp(-dd))
        keep = (jnp.right_shift(ti, 4) == jnp.right_shift(si, 4)) & ((si >= ti) if rev else (si <= ti))
        a0 += jnp.where(keep, sc[:t], 0.0)
        a1 += jnp.where(keep, sc[t:], 0.0)

        v2 = jnp.concatenate([jnp.where(lo, vp, 0.0), jnp.where(lo, 0.0, vp)], axis=0).astype(BF16)
        o = jnp.dot(jnp.concatenate([a0, a1], axis=1).astype(BF16), v2, preferred_element_type=F32)
        st = st_ref[pr]
        o += lax.dot_general((qp * jnp.exp(cp)).astype(BF16), st.astype(BF16), _NT,
                             preferred_element_type=F32)
        upd = lax.dot_general(vp.astype(BF16), (kp * jnp.exp(gp - cp)).astype(BF16), _TN,
                              preferred_element_type=F32)
        st_ref[pr] = st * jnp.exp(gp) + jnp.where(same_head, upd, 0.0)
        outs.append(o)
    return jnp.concatenate(outs, axis=1)


def _hgrn2_kernel(qf_ref, zf_ref, vf_ref, qb_ref, zb_ref, vb_ref, lb_ref, s0_ref,
                  of_ref, ob_ref, sl_ref, st_f, st_b, cum_scr):
    j = pl.program_id(1)

    @pl.when(j == 0)
    def _():
        st_f[...] = s0_ref[0]
        st_b[...] = s0_ref[1]

    of_ref[...] = _gla_block(qf_ref[...], zf_ref[...], vf_ref[...], lb_ref[0:1, :], st_f, cum_scr, False)
    ob_ref[...] = _gla_block(qb_ref[...], zb_ref[...], vb_ref[...], lb_ref[1:2, :], st_b, cum_scr, True)
    sl_ref[0] = st_f[...]
    sl_ref[1] = st_b[...]


def _hgrn2(p, lb, s0):
    b_, L, _ = p.shape
    n = L // HG_T
    fwd = lambda c: pl.BlockSpec((None, HG_T, GROUP_W), lambda b, j: (b, j, c))
    bwd = lambda c: pl.BlockSpec((None, HG_T, GROUP_W), lambda b, j: (b, n - 1 - j, c))
    st_spec = pl.BlockSpec((None, 2, 2, 128, 128), lambda b, j: (b, 0, 0, 0, 0))
    o_shape = jax.ShapeDtypeStruct((b_, L, GROUP_W), F32)
    return pl.pallas_call(
        _hgrn2_kernel,
        out_shape=(o_shape, o_shape, jax.ShapeDtypeStruct((b_, 2, 2, 128, 128), F32)),
        grid=(b_, n),
        in_specs=[fwd(0), fwd(1), fwd(3), bwd(0), bwd(2), bwd(3),
                  pl.BlockSpec((2, GROUP_W), lambda b, j: (0, 0)), st_spec],
        out_specs=(fwd(0), bwd(0), st_spec),
        scratch_shapes=[pltpu.VMEM((2, 128, 128), F32), pltpu.VMEM((2, 128, 128), F32),
                        pltpu.VMEM((HG_T, GROUP_W), F32)],
        compiler_params=_cparams(("arbitrary", "arbitrary")),
        name="hgrn2",
    )(p, p, p, p, p, p, lb, s0)


S5_TT = 64
S5_NS = S5_GROUPS * S5_STATE
S5_LANES = 256


def _s5_kernel(uf_ref, ub_ref, wb_ref, wc_ref, a_ref, x0_ref, yf_ref, yb_ref, xl_ref,
               buf_f, buf_b, st):
    j = pl.program_id(0)

    @pl.when(j == 0)
    def _():
        st[...] = x0_ref[...]

    buf_f[...] = jnp.dot(uf_ref[...].astype(BF16), wb_ref[0], preferred_element_type=F32)
    buf_b[...] = jnp.dot(ub_ref[...].astype(BF16), wb_ref[1], preferred_element_type=F32)

    for cc in range(S5_NS // S5_LANES):
        re = slice(cc * S5_LANES, (cc + 1) * S5_LANES)
        im = slice(S5_NS + cc * S5_LANES, S5_NS + (cc + 1) * S5_LANES)
        arf, aif, arb, aib = a_ref[0, :, re], a_ref[1, :, re], a_ref[2, :, re], a_ref[3, :, re]

        def step(t, carry, re=re, im=im, arf=arf, aif=aif, arb=arb, aib=aib):
            xrf, xif, xrb, xib = carry
            rf = pl.multiple_of(t * 8, 8)
            rb = pl.multiple_of((S5_TT - 1 - t) * 8, 8)
            nrf = arf * xrf - aif * xif + buf_f[pl.ds(rf, 8), re]
            nif = arf * xif + aif * xrf + buf_f[pl.ds(rf, 8), im]
            nrb = arb * xrb - aib * xib + buf_b[pl.ds(rb, 8), re]
            nib = arb * xib + aib * xrb + buf_b[pl.ds(rb, 8), im]
            buf_f[pl.ds(rf, 8), re] = nrf
            buf_f[pl.ds(rf, 8), im] = nif
            buf_b[pl.ds(rb, 8), re] = nrb
            buf_b[pl.ds(rb, 8), im] = nib
            return nrf, nif, nrb, nib

        fin = lax.fori_loop(0, S5_TT, step,
                            (st[0, :, re], st[1, :, re], st[2, :, re], st[3, :, re]), unroll=4)
        for k in range(4):
            st[k, :, re] = fin[k]

    yf_ref[...] = jnp.dot(buf_f[...].astype(BF16), wc_ref[0], preferred_element_type=F32)
    yb_ref[...] = jnp.dot(buf_b[...].astype(BF16), wc_ref[1], preferred_element_type=F32)
    xl_ref[...] = st[...]


def _s5_scan(u_tm, wb, wc, a_bc, x0):
    L = u_tm.shape[0]
    rows = S5_TT * 8
    n = L // S5_TT
    u2 = u_tm.reshape(L * 8, GROUP_W)
    y_shape = jax.ShapeDtypeStruct((L * 8, GROUP_W), F32)
    yf, yb, xl = pl.pallas_call(
        _s5_kernel,
        out_shape=(y_shape, y_shape, jax.ShapeDtypeStruct((4, 8, S5_NS), F32)),
        grid=(n,),
        in_specs=[
            pl.BlockSpec((rows, GROUP_W), lambda j: (j, 0)),
            pl.BlockSpec((rows, GROUP_W), lambda j: (n - 1 - j, 0)),
            pl.BlockSpec((2, GROUP_W, 2 * S5_NS), lambda j: (0, 0, 0)),
            pl.BlockSpec((2, 2 * S5_NS, GROUP_W), lambda j: (0, 0, 0)),
            pl.BlockSpec((4, 8, S5_NS), lambda j: (0, 0, 0)),
            pl.BlockSpec((4, 8, S5_NS), lambda j: (0, 0, 0)),
        ],
        out_specs=(pl.BlockSpec((rows, GROUP_W), lambda j: (j, 0)),
                   pl.BlockSpec((rows, GROUP_W), lambda j: (n - 1 - j, 0)),
                   pl.BlockSpec((4, 8, S5_NS), lambda j: (0, 0, 0))),
        scratch_shapes=[pltpu.VMEM((rows, 2 * S5_NS), F32),
                        pltpu.VMEM((rows, 2 * S5_NS), F32),
                        pltpu.VMEM((4, 8, S5_NS), F32)],
        compiler_params=_cparams(("arbitrary",)),
        name="s5_scan",
    )(u2, u2, wb, wc, a_bc, x0)
    return yf.reshape(L, 8 * GROUP_W), yb.reshape(L, 8 * GROUP_W), xl


def _s5_weights(a_re, a_im, log_step, b_re, b_im, c_re, c_im):
    eye = jnp.eye(S5_GROUPS, dtype=F32)
    wbs, wcs, abc = [], [], []
    for d in range(2):
        ab_re, ab_im, bb_re, bb_im = _s5_discretize(a_re[d], a_im[d], log_step[d], b_re, b_im)
        wb_re = jnp.einsum('gph,gk->ghkp', bb_re, eye).reshape(GROUP_W, S5_NS)
        wb_im = jnp.einsum('gph,gk->ghkp', bb_im, eye).reshape(GROUP_W, S5_NS)
        wbs.append(jnp.concatenate([wb_re, wb_im], axis=1))
        wc_re = jnp.einsum('ghp,gk->gpkh', c_re[d], eye).reshape(S5_NS, GROUP_W)
        wc_im = jnp.einsum('ghp,gk->gpkh', c_im[d], eye).reshape(S5_NS, GROUP_W)
        wcs.append(jnp.concatenate([wc_re, -wc_im], axis=0))
        abc += [jnp.broadcast_to(ab_re.reshape(1, S5_NS), (8, S5_NS)),
                jnp.broadcast_to(ab_im.reshape(1, S5_NS), (8, S5_NS))]
    return jnp.stack(wbs).astype(BF16), jnp.stack(wcs).astype(BF16), jnp.stack(abc)


def _group_norm(m, w):
    return m * lax.rsqrt(jnp.mean(m * m, axis=-1, keepdims=True) + EPS) * w


def _merge_kernel(hgf_ref, hgb_ref, yf_ref, yb_ref, u_ref, sd_ref, glu_ref, hy_ref, at_ref, gate_ref,
                  x_ref, mod_ref, mw_ref, fw_ref, wo_ref, xo_ref, ho_ref):
    gate = gate_ref[...]
    z = jax.nn.gelu(yf_ref[...] + yb_ref[...] + sd_ref[...] * u_ref[...])
    s5 = z * jax.nn.sigmoid(jnp.dot(z.astype(BF16), glu_ref[...], preferred_element_type=F32))
    parts = [
        _group_norm(hgf_ref[...] + hgb_ref[...], mw_ref[:, 0:256]) * (gate * jax.nn.sigmoid(gate)),
        _group_norm(s5, mw_ref[:, 256:512]),
        _group_norm(jnp.concatenate([hy_ref[s].T for s in range(hy_ref.shape[0])], axis=0),
                    mw_ref[:, 512:768]),
        _group_norm(at_ref[...], mw_ref[:, 768:1024]),
    ]
    mix = jnp.concatenate(parts, axis=1).astype(BF16)
    xn = x_ref[...] + mod_ref[2:3, :] * jnp.dot(mix, wo_ref[...], preferred_element_type=F32)
    xo_ref[...] = xn
    h = xn * lax.rsqrt(jnp.mean(xn * xn, axis=-1, keepdims=True) + EPS) * fw_ref[...]
    ho_ref[...] = (h * mod_ref[3:4, :] + mod_ref[4:5, :]).astype(BF16)


def _merge(hg_f, hg_b, s5_yf, s5_yb, s5_u, s5_d, glu_bf16, hy, at, p, x, mods, mw, fw, wo_bf16, tm):
    b_, L, _ = x.shape
    grp = lambda: pl.BlockSpec((None, tm, GROUP_W), lambda b, i: (b, i, 0))
    tmaj = lambda: pl.BlockSpec((tm, GROUP_W), lambda b, i: (i, b))
    return pl.pallas_call(
        _merge_kernel,
        out_shape=(jax.ShapeDtypeStruct((b_, L, D_MODEL), F32),
                   jax.ShapeDtypeStruct((b_, L, D_MODEL), BF16)),
        grid=(b_, L // tm),
        in_specs=[
            grp(), grp(), tmaj(), tmaj(), tmaj(),
            pl.BlockSpec((1, GROUP_W), lambda b, i: (0, 0)),
            pl.BlockSpec((GROUP_W, GROUP_W), lambda b, i: (0, 0)),
            pl.BlockSpec((None, tm // 128, HY_CH, 128), lambda b, i: (b, i, 0, 0)),
            grp(),
            pl.BlockSpec((None, tm, GROUP_W), lambda b, i: (b, i, 4)),
            pl.BlockSpec((None, tm, D_MODEL), lambda b, i: (b, i, 0)),
            pl.BlockSpec((None, 8, D_MODEL), lambda b, i: (b, 0, 0)),
            pl.BlockSpec((1, D_MODEL), lambda b, i: (0, 0)),
            pl.BlockSpec((1, D_MODEL), lambda b, i: (0, 0)),
            pl.BlockSpec((D_MODEL, D_MODEL), lambda b, i: (0, 0)),
        ],
        out_specs=(pl.BlockSpec((None, tm, D_MODEL), lambda b, i: (b, i, 0)),
                   pl.BlockSpec((None, tm, D_MODEL), lambda b, i: (b, i, 0))),
        compiler_params=_cparams(("arbitrary", "arbitrary")),
        name="merge_out_proj",
    )(hg_f, hg_b, s5_yf, s5_yb, s5_u, s5_d.reshape(1, GROUP_W), glu_bf16, hy, at, p, x, mods,
      mw.reshape(1, D_MODEL), fw.reshape(1, D_MODEL), wo_bf16)


def _ffn_kernel(hp_ref, h_ref, hn_ref, x_ref, mod_ref, wa_ref, wv_ref, cw_ref, wd_ref, o_ref,
                a_scr, acc_scr, *, tm):
    i = pl.program_id(1)
    last = pl.num_programs(1) - 1
    h = h_ref[...]
    hp = jnp.where(i > 0, hp_ref[...], jnp.zeros_like(hp_ref))
    hn = jnp.where(i < last, hn_ref[...], jnp.zeros_like(hn_ref))
    h_ext = jnp.concatenate([hp, h, hn], axis=0)
    acc_scr[...] = jnp.zeros_like(acc_scr)

    def body(j, carry):
        a_scr[...] = jnp.dot(h_ext, wa_ref[j], preferred_element_type=F32)
        v = jnp.dot(h, wv_ref[j], preferred_element_type=F32)
        cw = cw_ref[j]
        a = (a_scr[HALO - 1:HALO - 1 + tm, :] * cw[0:1, :]
             + a_scr[HALO:HALO + tm, :] * cw[1:2, :]
             + a_scr[HALO + 1:HALO + 1 + tm, :] * cw[2:3, :] + cw[3:4, :])
        g = (a * jax.nn.sigmoid(a) * v).astype(BF16)
        acc_scr[...] += jnp.dot(g, wd_ref[j], preferred_element_type=F32)
        return carry

    lax.fori_loop(0, N_FF_CHUNKS, body, 0)
    o_ref[...] = x_ref[...] + mod_ref[5:6, :] * acc_scr[...]


def _ffn(h, x, mods, wa, wv, cw, wd, tm):
    b_, L, _ = x.shape
    nh = L // HALO
    r = tm // HALO
    const3 = lambda b, i: (0, 0, 0)
    return pl.pallas_call(
        functools.partial(_ffn_kernel, tm=tm),
        out_shape=jax.ShapeDtypeStruct((b_, L, D_MODEL), F32),
        grid=(b_, L // tm),
        in_specs=[
            pl.BlockSpec((None, HALO, D_MODEL), lambda b, i: (b, jnp.maximum(i * r - 1, 0), 0)),
            pl.BlockSpec((None, tm, D_MODEL), lambda b, i: (b, i, 0)),
            pl.BlockSpec((None, HALO, D_MODEL), lambda b, i: (b, jnp.minimum((i + 1) * r, nh - 1), 0)),
            pl.BlockSpec((None, tm, D_MODEL), lambda b, i: (b, i, 0)),
            pl.BlockSpec((None, 8, D_MODEL), lambda b, i: (b, 0, 0)),
            pl.BlockSpec((N_FF_CHUNKS, D_MODEL, FF_CHUNK), const3, pipeline_mode=pl.Buffered(1)),
            pl.BlockSpec((N_FF_CHUNKS, D_MODEL, FF_CHUNK), const3, pipeline_mode=pl.Buffered(1)),
            pl.BlockSpec((N_FF_CHUNKS, 8, FF_CHUNK), const3),
            pl.BlockSpec((N_FF_CHUNKS, FF_CHUNK, D_MODEL), const3, pipeline_mode=pl.Buffered(1)),
        ],
        out_specs=pl.BlockSpec((None, tm, D_MODEL), lambda b, i: (b, i, 0)),
        scratch_shapes=[pltpu.VMEM((tm + 2 * HALO, FF_CHUNK), F32),
                        pltpu.VMEM((tm, D_MODEL), F32)],
        compiler_params=_cparams(("arbitrary", "arbitrary")),
        name="conv_ffn",
    )(h, h, h, x, mods, wa, wv, cw, wd)


def _final_norm_kernel(x_ref, w_ref, o_ref):
    x = x_ref[...]
    o_ref[...] = x * lax.rsqrt(jnp.mean(x * x, axis=-1, keepdims=True) + EPS) * w_ref[...]


def _final_norm(x, w, tm):
    b_, L, _ = x.shape
    return pl.pallas_call(
        _final_norm_kernel,
        out_shape=jax.ShapeDtypeStruct(x.shape, F32),
        grid=(b_, L // tm),
        in_specs=[pl.BlockSpec((None, tm, D_MODEL), lambda b, i: (b, i, 0)),
                  pl.BlockSpec((1, D_MODEL), lambda b, i: (0, 0))],
        out_specs=pl.BlockSpec((None, tm, D_MODEL), lambda b, i: (b, i, 0)),
        compiler_params=_cparams(("arbitrary", "arbitrary")),
        name="final_norm",
    )(x, w.reshape(1, D_MODEL))


def _s5_discretize(a_re, a_im, log_step, b_re, b_im):
    dt = jnp.exp(log_step)[:, None]
    mag = jnp.exp(a_re * dt)
    ang = a_im * dt
    ab_re, ab_im = mag * jnp.cos(ang), mag * jnp.sin(ang)
    den = a_re * a_re + a_im * a_im
    nr, ni = ab_re - 1.0, ab_im
    fr = (nr * a_re + ni * a_im) / den
    fi = (ni * a_re - nr * a_im) / den
    bb_re = fr[..., None] * b_re - fi[..., None] * b_im
    bb_im = fr[..., None] * b_im + fi[..., None] * b_re
    return ab_re, ab_im, bb_re, bb_im


HY_LANES = 128
HY_CB = 8
HY_EMB_PAD = 128


def _vadd(a, b):
    return b if a is None else a if b is None else a + b


def _vsub(a, b):
    return (None if b is None else -b) if a is None else a if b is None else a - b


def _vscale(a, c):
    return None if (a is None or c == 0.0) else a if c == 1.0 else -a if c == -1.0 else a * c


def _cmul_const(x, wr, wi):
    re, im = x
    return (_vsub(_vscale(re, wr), _vscale(im, wi)), _vadd(_vscale(im, wr), _vscale(re, wi)))


def _unit_root(k, n, sign):
    k %= n
    if (4 * k) % n == 0:
        return ((1.0, 0.0), (0.0, float(sign)), (-1.0, 0.0), (0.0, float(-sign)))[4 * k // n]
    ang = sign * 2.0 * math.pi * k / n
    return math.cos(ang), math.sin(ang)


def _fft_slabs(x, sign, n_out=None):
    n = len(x)
    if n == 1:
        return list(x)
    n_out = n if n_out is None else n_out
    ev = _fft_slabs(x[0::2], sign)
    od = _fft_slabs(x[1::2], sign)
    out = [None] * n_out
    for k in range(n // 2):
        t = _cmul_const(od[k], *_unit_root(k, n, sign))
        if k < n_out:
            out[k] = (_vadd(ev[k][0], t[0]), _vadd(ev[k][1], t[1]))
        if k + n // 2 < n_out:
            out[k + n // 2] = (_vsub(ev[k][0], t[0]), _vsub(ev[k][1], t[1]))
    return out


def _twiddle(slabs, tw_ref, conj):
    out = [slabs[0]]
    for k in range(1, len(slabs)):
        re, im = slabs[k]
        tr = tw_ref[0, k]
        ti = -tw_ref[1, k] if conj else tw_ref[1, k]
        if im is None:
            out.append((re * tr, re * ti))
        else:
            out.append((re * tr - im * ti, re * ti + im * tr))
    return out


def _dot_split(x, w_ref):
    hi = x.astype(BF16)
    lo = (x - hi.astype(F32)).astype(BF16)
    return (jnp.dot(hi, w_ref[0], preferred_element_type=F32)
            + jnp.dot(lo, w_ref[0], preferred_element_type=F32)
            + jnp.dot(hi, w_ref[1], preferred_element_type=F32))


def _lane_dft(slabs, w_ref):
    zero = jnp.zeros((HY_CB, HY_LANES), F32)
    rows = jnp.concatenate(
        [jnp.concatenate([zero if re is None else re, zero if im is None else im], axis=1)
         for re, im in slabs], axis=0)
    out = _dot_split(rows, w_ref)
    return [(out[HY_CB * i:HY_CB * (i + 1), :HY_LANES], out[HY_CB * i:HY_CB * (i + 1), HY_LANES:])
            for i in range(len(slabs))]


def _short_conv(ref, part, cw_ref, ch, lane):
    n = ref.shape[1]
    w0, w1, w2, bb = cw_ref[ch, 0], cw_ref[ch, 1], cw_ref[ch, 2], cw_ref[ch, 3]
    right = [pltpu.roll(ref[part, s], 1, 1) for s in range(n)]
    left = [pltpu.roll(ref[part, s], HY_LANES - 1, 1) for s in range(n)]
    zero = jnp.zeros((HY_CB, HY_LANES), F32)
    out = []
    for s in range(n):
        xm = jnp.where(lane == 0, right[s - 1] if s > 0 else zero, right[s])
        xp = jnp.where(lane == HY_LANES - 1, left[s + 1] if s < n - 1 else zero, left[s])
        out.append(xm * w0 + ref[part, s] * w1 + xp * w2 + bb)
    return out


def _hy_conv_kernel(x1_ref, x2_ref, z_ref, cw_ref, bias_ref, kf_ref, tw_ref, wf_ref, wi_ref, o_ref,
                    *, n1):
    nh = n1 // 2
    lane = lax.broadcasted_iota(jnp.int32, (HY_CB, HY_LANES), 1)
    gates = [[_short_conv(r, part, cw_ref, ch, lane) for part in range(2)]
             for ch, r in ((0, x1_ref), (1, x2_ref))]
    z = [_short_conv(z_ref, part, cw_ref, 2, lane) for part in range(2)]
    for o in range(HY_ORDER):
        spec = _fft_slabs([(z[0][s], z[1][s]) for s in range(nh)] + [(None, None)] * nh, -1)
        spec = _lane_dft(_twiddle(spec, tw_ref, False), wf_ref)
        spec = [(re * kf_ref[o, 0, k] - im * kf_ref[o, 1, k], re * kf_ref[o, 1, k] + im * kf_ref[o, 0, k])
                for k, (re, im) in enumerate(spec)]
        spec = _twiddle(_lane_dft(spec, wi_ref), tw_ref, True)
        y = _fft_slabs(spec, 1, n_out=nh)
        bo = bias_ref[o]
        z = [[gates[o][part][s] * (y[s][part] + bo * z[part][s]) for s in range(nh)]
             for part in range(2)]
    for part in range(2):
        for s in range(nh):
            o_ref[part, s] = z[part][s]


def _hy_conv(zh, cw, bias_b, kf, tw, wf, wi):
    b_, nh = zh.shape[0], zh.shape[1]
    half, ncb = b_ // 2, HY_CH // HY_CB
    n1 = 2 * nh
    zspec = lambda ch: pl.BlockSpec((2, None, nh, HY_CB, HY_LANES),
                                    lambda c, b: (0, b, 0, ch * ncb + c, 0))
    const3 = lambda c, b: (0, 0, 0)
    z5 = zh.reshape(2, half, nh, 3 * HY_CH, HY_LANES)
    out = pl.pallas_call(
        functools.partial(_hy_conv_kernel, n1=n1),
        out_shape=jax.ShapeDtypeStruct((2, half, nh, HY_CH, HY_LANES), F32),
        grid=(ncb, half),
        in_specs=[
            zspec(0), zspec(1), zspec(2),
            pl.BlockSpec((3, 4, HY_CB, HY_LANES), lambda c, b: (0, 0, c, 0)),
            pl.BlockSpec((HY_ORDER, HY_CB, HY_LANES), lambda c, b: (0, c, 0)),
            pl.BlockSpec((HY_ORDER, 2, n1, HY_CB, HY_LANES), lambda c, b: (0, 0, 0, c, 0)),
            pl.BlockSpec((2, n1, 8, HY_LANES), lambda c, b: (0, 0, 0, 0)),
            pl.BlockSpec((2, 256, 256), const3),
            pl.BlockSpec((2, 256, 256), const3),
        ],
        out_specs=pl.BlockSpec((2, None, nh, HY_CB, HY_LANES), lambda c, b: (0, b, 0, c, 0)),
        compiler_params=_cparams(("arbitrary", "arbitrary")),
        name="hyena_conv",
    )(z5, z5, z5, cw, bias_b, kf, tw, wf, wi)
    return out.reshape(b_, nh, HY_CH, HY_LANES)


def _hy_spectrum_kernel(k_ref, tw_ref, wf_ref, o_ref, *, n1):
    spec = _fft_slabs([(k_ref[s], None) for s in range(n1)], -1)
    spec = _lane_dft(_twiddle(spec, tw_ref, False), wf_ref)
    scale = 1.0 / (n1 * HY_LANES)
    for k in range(n1):
        o_ref[0, k] = spec[k][0] * scale
        o_ref[1, k] = spec[k][1] * scale


def _hy_spectrum(taps, tw, wf):
    n1 = taps.shape[1]
    return pl.pallas_call(
        functools.partial(_hy_spectrum_kernel, n1=n1),
        out_shape=jax.ShapeDtypeStruct((HY_ORDER, 2, n1, HY_CH, HY_LANES), F32),
        grid=(HY_ORDER, HY_CH // HY_CB),
        in_specs=[
            pl.BlockSpec((None, n1, HY_CB, HY_LANES), lambda o, c: (o, 0, c, 0)),
            pl.BlockSpec((2, n1, 8, HY_LANES), lambda o, c: (0, 0, 0, 0)),
            pl.BlockSpec((2, 256, 256), lambda o, c: (0, 0, 0)),
        ],
        out_specs=pl.BlockSpec((None, 2, n1, HY_CB, HY_LANES), lambda o, c: (o, 0, 0, c, 0)),
        compiler_params=_cparams(("arbitrary", "arbitrary")),
        name="hyena_spectrum",
    )(taps, tw, wf)


def _dot_f32(a, b):
    a_hi = a.astype(BF16)
    a_lo = (a - a_hi.astype(F32)).astype(BF16)
    b_hi = b.astype(BF16)
    b_lo = (b - b_hi.astype(F32)).astype(BF16)
    return (jnp.dot(a_hi, b_hi, preferred_element_type=F32)
            + jnp.dot(a_lo, b_hi, preferred_element_type=F32)
            + jnp.dot(a_hi, b_lo, preferred_element_type=F32))


def _hy_mlp_kernel(emb_ref, w1_ref, w2_ref, w3_ref, vec_ref, dl_ref, o_ref):
    z = emb_ref[...]
    h = jnp.sin(vec_ref[2:3, :] * (_dot_f32(z, w1_ref[...]) + vec_ref[0:1, :]))
    h = jnp.sin(vec_ref[3:4, :] * (_dot_f32(h, w2_ref[...]) + vec_ref[1:2, :]))
    o_ref[...] = _dot_f32(h, w3_ref[...]) * jnp.exp(-z[:, 0:1] * dl_ref[...])


def _hy_filter_taps(L, w1, b1, freq, w2, b2, w3):
    pad = HY_EMB_PAD
    t01 = np.linspace(0.0, 1.0, L, dtype=np.float32)[:, None]
    w = (np.float32(2.0 * math.pi) * np.arange(L, dtype=np.float32)[:, None]) / np.float32(L)
    bands = (HY_EMB - 1) // 2
    fr = np.linspace(1e-4, bands - 1, bands, dtype=np.float32)[None, :]
    arg = (fr * w).astype(np.float64)
    emb = np.zeros((L, pad), np.float32)
    emb[:, 0:1] = t01
    emb[:, 1:1 + bands] = np.cos(arg)
    emb[:, 1 + bands:HY_EMB] = -np.sin(arg)
    hid = w1.shape[1]
    w1p = jnp.zeros((pad, pad), F32).at[:HY_EMB, :hid].set(w1)
    w2p = jnp.zeros((pad, pad), F32).at[:hid, :hid].set(w2)
    w3p = jnp.zeros((pad, w3.shape[1]), F32).at[:hid, :].set(w3)
    vec = jnp.zeros((8, pad), F32).at[0, :hid].set(b1).at[1, :hid].set(b2)
    vec = vec.at[2, :hid].set(freq[0]).at[3, :hid].set(freq[1])
    deltas = np.abs(np.linspace(HY_MIN_DECAY, HY_MAX_DECAY, HY_CH, dtype=np.float32))
    dl = np.tile(deltas, 2 * HY_ORDER)[None, :]
    tl = min(L, 512)
    n_out = w3.shape[1]
    h = pl.pallas_call(
        _hy_mlp_kernel,
        out_shape=jax.ShapeDtypeStruct((L, n_out), F32),
        grid=(L // tl,),
        in_specs=[
            pl.BlockSpec((tl, pad), lambda i: (i, 0)),
            pl.BlockSpec((pad, pad), lambda i: (0, 0)),
            pl.BlockSpec((pad, pad), lambda i: (0, 0)),
            pl.BlockSpec((pad, n_out), lambda i: (0, 0)),
            pl.BlockSpec((8, pad), lambda i: (0, 0)),
            pl.BlockSpec((1, n_out), lambda i: (0, 0)),
        ],
        out_specs=pl.BlockSpec((tl, n_out), lambda i: (i, 0)),
        compiler_params=_cparams(("arbitrary",)),
        name="hyena_filter_mlp",
    )(jnp.asarray(emb), w1p, w2p, w3p, vec, jnp.asarray(dl))
    h = h.reshape(L, HY_ORDER, 2, HY_CH)
    h_f, h_b = h[:, :, 0], h[:, :, 1]
    k2 = jnp.concatenate([h_f[:1] + h_b[:1], h_f[1:], jnp.zeros_like(h_f[:1]),
                          jnp.flip(h_b[1:], axis=0)], axis=0)
    n1 = 2 * L // HY_LANES
    return k2.reshape(n1, HY_LANES, HY_ORDER, HY_CH).transpose(2, 0, 3, 1)


def _hy_constants(n1):
    n = n1 * HY_LANES
    ang = -2.0 * np.pi * np.outer(np.arange(n1), np.arange(HY_LANES)) / n
    tw = np.stack([np.cos(ang), np.sin(ang)])[:, :, None, :].repeat(8, axis=2).astype(np.float32)
    a2 = -2.0 * np.pi * np.outer(np.arange(HY_LANES), np.arange(HY_LANES)) / HY_LANES
    fr, fi = np.cos(a2), np.sin(a2)
    fwd = np.block([[fr, fi], [-fi, fr]])
    inv = np.block([[fr, -fi], [fi, fr]])

    def split(m):
        hi = m.astype(BF16)
        lo = (m - hi.astype(np.float64)).astype(BF16)
        return jnp.asarray(np.stack([hi, lo]))

    return jnp.asarray(tw), split(fwd), split(inv)


def _att_col_perm():
    inner = np.concatenate([np.arange(0, ATT_HD, 2), np.arange(1, ATT_HD, 2)])
    q = np.concatenate([h * ATT_HD + inner for h in (0, 2, 1, 3)])
    k = np.concatenate([h * ATT_HD + inner for h in (0, 1)])
    return np.concatenate([np.arange(COL_TQ), COL_TQ + q, COL_TK + k,
                           np.arange(COL_TV, IN_COLS)])


def _att_out_perm():
    at = np.concatenate([h * ATT_HD + np.arange(ATT_HD) for h in (0, 2, 1, 3)])
    return np.concatenate([np.arange(3 * GROUP_W), 3 * GROUP_W + at])


def _rope_tables(L):
    rows = L // GRID_W
    row = jnp.repeat(jnp.arange(rows), GRID_W).astype(F32)
    col = jnp.tile(jnp.arange(GRID_W), rows).astype(F32)
    axis_dim = ATT_HD // 2
    inv = 1.0 / (ROPE_BASE ** (jnp.arange(0, axis_dim, 2, dtype=F32) / axis_dim))
    ang = jnp.concatenate([row[:, None] * inv, col[:, None] * inv], axis=-1)
    c, s = jnp.cos(ang), jnp.sin(ang)
    return jnp.tile(jnp.concatenate([c, c], axis=1), (1, 2)), jnp.tile(jnp.concatenate([-s, s], axis=1), (1, 2))


def kernel(x, c, ctx, c_ctx, ada_w, ada_b, norm_mix_w, norm_ffn_w, w_in, hg_lb_logits, s5_a_re, s5_a_im, s5_log_step, s5_b_re, s5_b_im, s5_c_re, s5_c_im, s5_d, s5_glu_w, hy_conv_w, hy_conv_b, hy_w1, hy_b1, hy_freq, hy_w2, hy_b2, hy_w3, hy_bias, att_sink, merge_norm_w, w_out, ffn_w_up, ffn_conv_w, ffn_conv_b, ffn_w_down, final_norm_w):
    bsz, seq_len, _ = x.shape
    ctx_len = ctx.shape[1]

    cond = jnp.zeros((16, D_MODEL), F32).at[:bsz].set(c).at[bsz].set(c_ctx)
    ada = _ada_all(cond, ada_w, ada_b)

    def mods_of(m):
        sh1, sc1, g1, sh2, sc2, g2 = jnp.split(m, 6, axis=-1)
        z = jnp.zeros_like(sh1)
        return jnp.stack([1.0 + sc1, sh1, g1, 1.0 + sc2, sh2, g2, z, z], axis=1)

    sm = jax.nn.softmax(hg_lb_logits, axis=0)
    lower_bounds = jnp.cumsum(sm, axis=0) - sm[0:1]
    cos_t, sin_t = _rope_tables(seq_len)
    col_perm = _att_col_perm()
    out_perm = _att_out_perm()
    hg_zero = jnp.zeros((bsz, 2, 2, 128, 128), F32)
    hy_tw_lat, hy_wf, hy_wi = _hy_constants(2 * seq_len // HY_LANES)
    hy_tw_ctx, _, _ = _hy_constants(2 * ctx_len // HY_LANES)
    s5_zero = jnp.zeros((4, bsz, S5_NS), F32)

    xc = ctx
    for l in range(DEPTH):
        last = l == DEPTH - 1
        mods_lat = mods_of(ada[l, :bsz])
        mods_ctx = jnp.broadcast_to(mods_of(ada[l, bsz:bsz + 1]), (bsz, 8, D_MODEL))
        w_in_l = w_in[l][:, col_perm].astype(BF16)
        mw = merge_norm_w[l][out_perm]
        wo = w_out[l][out_perm, :].astype(BF16)
        wa = ffn_w_up[l][:, :D_FF].reshape(D_MODEL, N_FF_CHUNKS, FF_CHUNK).transpose(1, 0, 2).astype(BF16)
        wv = ffn_w_up[l][:, D_FF:].reshape(D_MODEL, N_FF_CHUNKS, FF_CHUNK).transpose(1, 0, 2).astype(BF16)
        wd = ffn_w_down[l].reshape(N_FF_CHUNKS, FF_CHUNK, D_MODEL).astype(BF16)
        cw = jnp.concatenate([ffn_conv_w[l], ffn_conv_b[l][None], jnp.zeros((4, D_FF), F32)], axis=0)
        cw = cw.reshape(8, N_FF_CHUNKS, FF_CHUNK).transpose(1, 0, 2)

        wh = w_in[l][:, COL_HY:COL_HY + 3 * HY_CH].T.astype(BF16)
        p_lat, u_lat, zh_lat = _inproj(x, mods_lat, norm_mix_w[l], w_in_l, wh, 256)
        p_ctx, u_ctx, zh_ctx = _inproj(xc, mods_ctx, norm_mix_w[l], w_in_l, wh, ctx_len)

        hgf_ctx, hgb_ctx, hg_state = _hgrn2(p_ctx, lower_bounds[l], hg_zero)
        hgf_lat, hgb_lat, _ = _hgrn2(p_lat, lower_bounds[l], hg_state)

        s5_wb, s5_wc, s5_a = _s5_weights(s5_a_re[l], s5_a_im[l], s5_log_step[l], s5_b_re[l],
                                         s5_b_im[l], s5_c_re[l], s5_c_im[l])
        glu = s5_glu_w[l].astype(BF16)
        yf_ctx, yb_ctx, s5_state = _s5_scan(u_ctx, s5_wb, s5_wc, s5_a, s5_zero)
        yf_lat, yb_lat, _ = _s5_scan(u_lat, s5_wb, s5_wc, s5_a, s5_state)

        hy_params = (hy_w1[l], hy_b1[l], hy_freq[l], hy_w2[l], hy_b2[l], hy_w3[l])
        hy_cw = jnp.concatenate([hy_conv_w[l], hy_conv_b[l][None]], axis=0)
        hy_cw = jnp.broadcast_to(hy_cw.reshape(4, 3, HY_CH).transpose(1, 0, 2)[..., None],
                                 (3, 4, HY_CH, HY_LANES))
        hy_bb = jnp.broadcast_to(hy_bias[l][..., None], (HY_ORDER, HY_CH, HY_LANES))
        kf_lat = _hy_spectrum(_hy_filter_taps(seq_len, *hy_params), hy_tw_lat, hy_wf)
        hy_lat = _hy_conv(zh_lat, hy_cw, hy_bb, kf_lat, hy_tw_lat, hy_wf, hy_wi)

        at_lat = _att_lat(p_lat, p_ctx, cos_t, sin_t, att_sink[l])

        x_mid, h_lat = _merge(hgf_lat, hgb_lat, yf_lat, yb_lat, u_lat, s5_d[l], glu, hy_lat, at_lat, p_lat, x,
                              mods_lat, mw, norm_ffn_w[l], wo, 256)
        x_new = _ffn(h_lat, x_mid, mods_lat, wa, wv, cw, wd, 512)
        if not last:
            kf_ctx = _hy_spectrum(_hy_filter_taps(ctx_len, *hy_params), hy_tw_ctx, hy_wf)
            hy_ctx = _hy_conv(zh_ctx, hy_cw, hy_bb, kf_ctx, hy_tw_ctx, hy_wf, hy_wi)
            at_ctx = _att_ctx(p_ctx, att_sink[l])
            xc_mid, h_ctx = _merge(hgf_ctx, hgb_ctx, yf_ctx, yb_ctx, u_ctx, s5_d[l], glu, hy_ctx, at_ctx, p_ctx,
                                   xc, mods_ctx, mw, norm_ffn_w[l], wo, ctx_len)
            xc = _ffn(h_ctx, xc_mid, mods_ctx, wa, wv, cw, wd, ctx_len)
        x = x_new
    return _final_norm(x, final_norm_w, 512)
```

```python
import functools
import math

import jax
import jax.numpy as jnp
import numpy as np
from jax import lax
from jax.experimental import pallas as pl
from jax.experimental.pallas import tpu as pltpu

F32 = jnp.float32
BF16 = jnp.bfloat16

D_MODEL = 1024
DEPTH = 4
GRID_W = 64
GROUP_W = 256
HG_DK = 64
HG_DV = 64
HG_HEADS = 4
HG_CHUNK = 16
S5_GROUP_CH = 16
S5_GROUPS = 16
S5_STATE = 64
HY_CH = 256
HY_ORDER = 2
HY_EMB = 33
HY_MAX_DECAY = math.log(1e-2) / 0.3
HY_MIN_DECAY = math.log(1e-2) / 1.5
ATT_HD = 64
ATT_HEADS = 4
ATT_KV = 2
WINDOW = 128
ATT_BLOCK = 128
ATT_SCALE = 1.0 / math.sqrt(ATT_HD)
ROPE_BASE = 10000.0
D_FF = 2816
EPS = 1e-6
IN_COLS = 2816
W_COL_S5 = 1280
W_COL_HY = 1536
W_COL_TQ = 2304
W_COL_TK = 2560
W_COL_TV = 2688
P_COLS = 1792
COL_TQ = 1280
COL_TK = 1536
COL_TV = 1664

FF_CHUNK = 256
N_FF_CHUNKS = D_FF // FF_CHUNK
HALO = 16
VMEM_LIMIT = 56 * 1024 * 1024


def _cparams(sem):
    return pltpu.CompilerParams(dimension_semantics=sem, vmem_limit_bytes=VMEM_LIMIT)


def _ada_kernel(c_ref, w_ref, b_ref, o_ref):
    cond = c_ref[...]
    act = cond * jax.nn.sigmoid(cond)
    o_ref[...] = jnp.dot(act.astype(BF16), w_ref[...].astype(BF16),
                         preferred_element_type=F32) + b_ref[...]


def _ada_all(cond, ada_w, ada_b):
    tn = 1024
    n6 = 6 * D_MODEL
    return pl.pallas_call(
        _ada_kernel,
        out_shape=jax.ShapeDtypeStruct((DEPTH, 16, n6), F32),
        grid=(DEPTH, n6 // tn),
        in_specs=[
            pl.BlockSpec((16, D_MODEL), lambda l, j: (0, 0)),
            pl.BlockSpec((None, D_MODEL, tn), lambda l, j: (l, 0, j)),
            pl.BlockSpec((None, 1, tn), lambda l, j: (l, 0, j)),
        ],
        out_specs=pl.BlockSpec((None, 16, tn), lambda l, j: (l, 0, j)),
        compiler_params=_cparams(("arbitrary", "arbitrary")),
        name="ada_mod",
    )(cond, ada_w, ada_b.reshape(DEPTH, 1, n6))


def _inproj_kernel(x_ref, mod_ref, nw_ref, w_ref, wh_ref, o_ref, u_ref, zh_ref):
    x = x_ref[...]
    ms = jnp.mean(x * x, axis=-1, keepdims=True)
    y = x * lax.rsqrt(ms + EPS) * nw_ref[...]
    y = (y * mod_ref[0:1, :] + mod_ref[1:2, :]).astype(BF16)
    p = jnp.dot(y, w_ref[...], preferred_element_type=F32)
    o_ref[...] = p[:, :P_COLS]
    for c in range(2):
        u_ref.at[c][pl.ds(pl.program_id(1), x.shape[0], stride=8), :] = (
            p[:, P_COLS + 128 * c:P_COLS + 128 * (c + 1)])
    zt = lax.dot_general(wh_ref[...], y, _NT, preferred_element_type=F32)
    for s in range(zh_ref.shape[0]):
        zh_ref[s] = zt[:, 128 * s:128 * (s + 1)]


def _inproj(x, mods, nw, w_bf16, wh_bf16, tm):
    b_, L, _ = x.shape
    return pl.pallas_call(
        _inproj_kernel,
        out_shape=(jax.ShapeDtypeStruct((b_, L, P_COLS), F32),
                   jax.ShapeDtypeStruct((2, L * b_, 128), F32),
                   jax.ShapeDtypeStruct((b_, L // 128, 3 * HY_CH, 128), F32)),
        grid=(L // tm, b_),
        in_specs=[
            pl.BlockSpec((None, tm, D_MODEL), lambda i, b: (b, i, 0)),
            pl.BlockSpec((None, 8, D_MODEL), lambda i, b: (b, 0, 0)),
            pl.BlockSpec((1, D_MODEL), lambda i, b: (0, 0)),
            pl.BlockSpec((D_MODEL, P_COLS + GROUP_W), lambda i, b: (0, 0)),
            pl.BlockSpec((3 * HY_CH, D_MODEL), lambda i, b: (0, 0)),
        ],
        out_specs=(pl.BlockSpec((None, tm, P_COLS), lambda i, b: (b, i, 0)),
                   pl.BlockSpec((2, tm * b_, 128), lambda i, b: (0, i, 0)),
                   pl.BlockSpec((None, tm // 128, 3 * HY_CH, 128), lambda i, b: (b, i, 0, 0))),
        compiler_params=_cparams(("arbitrary", "arbitrary")),
        name="in_proj",
    )(x, mods, nw.reshape(1, D_MODEL), w_bf16, wh_bf16)


def _half_swap(t):
    w = t.shape[-1]
    lane = lax.broadcasted_iota(jnp.int32, t.shape, t.ndim - 1)
    return jnp.where((lane % ATT_HD) < ATT_HD // 2,
                     pltpu.roll(t, w - ATT_HD // 2, t.ndim - 1),
                     pltpu.roll(t, ATT_HD // 2, t.ndim - 1))


def _rope(t, cos, sin):
    return t * cos + _half_swap(t) * sin


def _attend(q, keys, vals, valid, sink_ref):
    lane = lax.broadcasted_iota(jnp.int32, (1, 128), 1)
    lo = lane < ATT_HD
    kb = keys.astype(BF16)
    v_lo = jnp.where(lo, vals, 0.0).astype(BF16)
    v_hi = jnp.where(lo, 0.0, vals).astype(BF16)
    v_cat = jnp.concatenate([v_lo, v_hi], axis=0)
    outs = []
    for m in range(2):
        qm = q[:, 128 * m:128 * (m + 1)]
        probs = []
        for half in range(2):
            head = (0, 2, 1, 3)[2 * m + half]
            qh = jnp.where(lo if half == 0 else jnp.logical_not(lo), qm, 0.0).astype(BF16)
            s = lax.dot_general(qh, kb, (((1,), (1,)), ((), ())), preferred_element_type=F32)
            if valid is not None:
                s = jnp.where(valid, s, -jnp.inf)
            sink = sink_ref[head]
            mx = jnp.maximum(jnp.max(s, axis=-1, keepdims=True), sink)
            e = jnp.exp(s - mx)
            den = jnp.sum(e, axis=-1, keepdims=True) + jnp.exp(sink - mx)
            probs.append((e / den).astype(BF16))
        p_cat = jnp.concatenate(probs, axis=1)
        outs.append(jnp.dot(p_cat, v_cat, preferred_element_type=F32))
    return jnp.concatenate(outs, axis=1)


def _att_lat_kernel(sink_ref, q_ref, kp_ref, kc_ref, kn_ref, vp_ref, vc_ref, vn_ref,
                    kx_ref, vx_ref, cp_ref, cc_ref, cn_ref, sp_ref, sc_ref, sn_ref, o_ref,
                    *, seq_len):
    n = pl.program_id(1)
    cos_c, sin_c = cc_ref[...], sc_ref[...]
    q = _rope(q_ref[...], jnp.concatenate([cos_c, cos_c], axis=1),
              jnp.concatenate([sin_c, sin_c], axis=1)) * ATT_SCALE
    keys = jnp.concatenate([
        kx_ref[...],
        _rope(kp_ref[...], cp_ref[...], sp_ref[...]),
        _rope(kc_ref[...], cos_c, sin_c),
        _rope(kn_ref[...], cn_ref[...], sn_ref[...])], axis=0)
    vals = jnp.concatenate([vx_ref[...], vp_ref[...], vc_ref[...], vn_ref[...]], axis=0)
    lc = kx_ref.shape[0]
    s_tot = lc + 3 * ATT_BLOCK
    col = lax.broadcasted_iota(jnp.int32, (ATT_BLOCK, s_tot), 1)
    row = lax.broadcasted_iota(jnp.int32, (ATT_BLOCK, s_tot), 0)
    qpos = n * ATT_BLOCK + row
    kpos = (n - 1) * ATT_BLOCK + (col - lc)
    valid = (col < lc) | ((jnp.abs(kpos - qpos) <= WINDOW) & (kpos >= 0) & (kpos < seq_len))
    o_ref[...] = _attend(q, keys, vals, valid, sink_ref)


def _att_lat(p_lat, p_ctx, cos_t, sin_t, sink):
    b_, L, _ = p_lat.shape
    lc = p_ctx.shape[1]
    nb = L // ATT_BLOCK
    kcol, vcol = COL_TK // 128, COL_TV // 128
    prev = lambda b, n: jnp.maximum(n - 1, 0)
    nxt = lambda b, n: jnp.minimum(n + 1, nb - 1)
    blk = lambda c, f: pl.BlockSpec((None, ATT_BLOCK, 128), lambda b, n: (b, f(b, n), c))
    cur = lambda b, n: n
    tab = lambda f: pl.BlockSpec((ATT_BLOCK, 128), lambda b, n: (f(b, n), 0))
    return pl.pallas_call(
        functools.partial(_att_lat_kernel, seq_len=L),
        out_shape=jax.ShapeDtypeStruct((b_, L, GROUP_W), F32),
        grid=(b_, nb),
        in_specs=[
            pl.BlockSpec(memory_space=pltpu.SMEM),
            pl.BlockSpec((None, ATT_BLOCK, 256), lambda b, n: (b, n, COL_TQ // 256)),
            blk(kcol, prev), blk(kcol, cur), blk(kcol, nxt),
            blk(vcol, prev), blk(vcol, cur), blk(vcol, nxt),
            pl.BlockSpec((None, lc, 128), lambda b, n: (b, 0, kcol)),
            pl.BlockSpec((None, lc, 128), lambda b, n: (b, 0, vcol)),
            tab(prev), tab(cur), tab(nxt), tab(prev), tab(cur), tab(nxt),
        ],
        out_specs=pl.BlockSpec((None, ATT_BLOCK, GROUP_W), lambda b, n: (b, n, 0)),
        compiler_params=_cparams(("arbitrary", "arbitrary")),
        name="att_latent",
    )(sink, p_lat, p_lat, p_lat, p_lat, p_lat, p_lat, p_lat, p_ctx, p_ctx,
      cos_t, cos_t, cos_t, sin_t, sin_t, sin_t)


def _att_ctx_kernel(sink_ref, q_ref, k_ref, v_ref, o_ref):
    o_ref[...] = _attend(q_ref[...] * ATT_SCALE, k_ref[...], v_ref[...], None, sink_ref)


def _att_ctx(p_ctx, sink):
    b_, lc, _ = p_ctx.shape
    return pl.pallas_call(
        _att_ctx_kernel,
        out_shape=jax.ShapeDtypeStruct((b_, lc, GROUP_W), F32),
        grid=(b_,),
        in_specs=[
            pl.BlockSpec(memory_space=pltpu.SMEM),
            pl.BlockSpec((None, lc, 256), lambda b: (b, 0, COL_TQ // 256)),
            pl.BlockSpec((None, lc, 128), lambda b: (b, 0, COL_TK // 128)),
            pl.BlockSpec((None, lc, 128), lambda b: (b, 0, COL_TV // 128)),
        ],
        out_specs=pl.BlockSpec((None, lc, GROUP_W), lambda b: (b, 0, 0)),
        compiler_params=_cparams(("arbitrary",)),
        name="att_context",
    )(sink, p_ctx, p_ctx, p_ctx)


HG_T = 128
HG_LEVELS = (64, 32, 16)
_NT = (((1,), (1,)), ((), ()))
_TN = (((0,), (0,)), ((), ()))


def _row_fill(ref, rows, blk):
    w = ref.shape[-1]
    return jnp.concatenate([jnp.broadcast_to(ref[r:r + 1, :], (blk, w)) for r in rows], axis=0)


def _gla_block(q_raw, z, v, lb, st_ref, cum_scr, rev):
    t = HG_T
    sg = jax.nn.sigmoid(z)
    g = jnp.log(lb + (1.0 - lb) * sg)
    k = (1.0 - lb) * jax.nn.sigmoid(-z)
    q = q_raw * jax.nn.sigmoid(q_raw)

    g_hi = g.astype(BF16)
    r1 = g - g_hi.astype(F32)
    g_mid = r1.astype(BF16)
    g_lo = (r1 - g_mid.astype(F32)).astype(BF16)
    ti = lax.broadcasted_iota(jnp.int32, (t, t), 0)
    si = lax.broadcasted_iota(jnp.int32, (t, t), 1)
    tri = jnp.where((si >= ti) if rev else (si <= ti), 1.0, 0.0).astype(BF16)
    c3 = jnp.dot(tri, jnp.concatenate([g_hi, g_mid, g_lo], axis=1), preferred_element_type=F32)
    cum = c3[:, 0:256] + c3[:, 256:512] + c3[:, 512:768]
    cum_scr[...] = cum
    edge = 0 if rev else t - 1
    g_tot = cum_scr[edge:edge + 1, :]

    lane = lax.broadcasted_iota(jnp.int32, (1, 128), 1)
    lo = lane < HG_DK
    rowi = lax.broadcasted_iota(jnp.int32, (t, 1), 0)
    vi = lax.broadcasted_iota(jnp.int32, (128, 128), 0)
    ki = lax.broadcasted_iota(jnp.int32, (128, 128), 1)
    same_head = (vi < HG_DV) == (ki < HG_DK)

    def scores(qt, kt):
        q2 = jnp.concatenate([jnp.where(lo, qt, 0.0), jnp.where(lo, 0.0, qt)], axis=0).astype(BF16)
        return lax.dot_general(q2, kt.astype(BF16), _NT, preferred_element_type=F32)

    outs = []
    for pr in range(2):
        sl = slice(128 * pr, 128 * (pr + 1))
        qp, kp, vp, cp, gp = q[:, sl], k[:, sl], v[:, sl], cum[:, sl], g_tot[:, sl]
        a0 = jnp.zeros((t, t), F32)
        a1 = jnp.zeros((t, t), F32)
        for sh, h in zip((6, 5, 4), HG_LEVELS):
            rows = [b * 2 * h + (h if rev else h - 1) for b in range(t // (2 * h))]
            e = jnp.exp(-jnp.abs(cp - _row_fill(cum_scr.at[:, sl], rows, 2 * h)))
            odd = (jnp.right_shift(rowi, sh) & 1) == 1
            q_on = jnp.logical_not(odd) if rev else odd
            sc = scores(jnp.where(q_on, qp * e, 0.0), jnp.where(q_on, 0.0, kp * e))
            if 2 * h < t:
                keep = jnp.right_shift(ti, sh + 1) == jnp.right_shift(si, sh + 1)
                a0 += jnp.where(keep, sc[:t], 0.0)
                a1 += jnp.where(keep, sc[t:], 0.0)
            else:
                a0 += sc[:t]
                a1 += sc[t:]
        rows = [b * HG_CHUNK + (HG_CHUNK // 2 if rev else HG_CHUNK // 2 - 1) for b in range(t // HG_CHUNK)]
        dd = cp - _row_fill(cum_scr.at[:, sl], rows, HG_CHUNK)
        sc = scores(qp * jnp.exp(dd), kp * jnp.exp(-dd))
        keep = (jnp.right_shift(ti, 4) == jnp.right_shift(si, 4)) & ((si >= ti) if rev else (si <= ti))
        a0 += jnp.where(keep, sc[:t], 0.0)
        a1 += jnp.where(keep, sc[t:], 0.0)

        v2 = jnp.concatenate([jnp.where(lo, vp, 0.0), jnp.where(lo, 0.0, vp)], axis=0).astype(BF16)
        o = jnp.dot(jnp.concatenate([a0, a1], axis=1).astype(BF16), v2, preferred_element_type=F32)
        st = st_ref[pr]
        o += lax.dot_general((qp * jnp.exp(cp)).astype(BF16), st.astype(BF16), _NT,
                             preferred_element_type=F32)
        upd = lax.dot_general(vp.astype(BF16), (kp * jnp.exp(gp - cp)).astype(BF16), _TN,
                              preferred_element_type=F32)
        st_ref[pr] = st * jnp.exp(gp) + jnp.where(same_head, upd, 0.0)
        outs.append(o)
    return jnp.concatenate(outs, axis=1)


def _hgrn2_kernel(qf_ref, zf_ref, vf_ref, qb_ref, zb_ref, vb_ref, lb_ref, s0_ref,
                  of_ref, ob_ref, sl_ref, st_f, st_b, cum_scr):
    j = pl.program_id(1)

    @pl.when(j == 0)
    def _():
        st_f[...] = s0_ref[0]
        st_b[...] = s0_ref[1]

    of_ref[...] = _gla_block(qf_ref[...], zf_ref[...], vf_ref[...], lb_ref[0:1, :], st_f, cum_scr, False)
    ob_ref[...] = _gla_block(qb_ref[...], zb_ref[...], vb_ref[...], lb_ref[1:2, :], st_b, cum_scr, True)
    sl_ref[0] = st_f[...]
    sl_ref[1] = st_b[...]


def _hgrn2(p, lb, s0):
    b_, L, _ = p.shape
    n = L // HG_T
    fwd = lambda c: pl.BlockSpec((None, HG_T, GROUP_W), lambda b, j: (b, j, c))
    bwd = lambda c: pl.BlockSpec((None, HG_T, GROUP_W), lambda b, j: (b, n - 1 - j, c))
    st_spec = pl.BlockSpec((None, 2, 2, 128, 128), lambda b, j: (b, 0, 0, 0, 0))
    o_shape = jax.ShapeDtypeStruct((b_, L, GROUP_W), F32)
    return pl.pallas_call(
        _hgrn2_kernel,
        out_shape=(o_shape, o_shape, jax.ShapeDtypeStruct((b_, 2, 2, 128, 128), F32)),
        grid=(b_, n),
        in_specs=[fwd(0), fwd(1), fwd(3), bwd(0), bwd(2), bwd(3),
                  pl.BlockSpec((2, GROUP_W), lambda b, j: (0, 0)), st_spec],
        out_specs=(fwd(0), bwd(0), st_spec),
        scratch_shapes=[pltpu.VMEM((2, 128, 128), F32), pltpu.VMEM((2, 128, 128), F32),
                        pltpu.VMEM((HG_T, GROUP_W), F32)],
        compiler_params=_cparams(("arbitrary", "arbitrary")),
        name="hgrn2",
    )(p, p, p, p, p, p, lb, s0)


S5_TT = 64
S5_NS = S5_GROUPS * S5_STATE
S5_LANES = 256


def _s5_kernel(uf_ref, ub_ref, wb_ref, wc_ref, a_ref, x0_ref, yf_ref, yb_ref, xl_ref,
               buf_f, buf_b, st):
    j = pl.program_id(0)

    @pl.when(j == 0)
    def _():
        st[...] = x0_ref[...]

    halves = lambda r: jnp.concatenate([r[0], r[1]], axis=1).astype(BF16)
    buf_f[...] = jnp.dot(halves(uf_ref), wb_ref[0], preferred_element_type=F32)
    buf_b[...] = jnp.dot(halves(ub_ref), wb_ref[1], preferred_element_type=F32)

    for cc in range(S5_NS // S5_LANES):
        re = slice(cc * S5_LANES, (cc + 1) * S5_LANES)
        im = slice(S5_NS + cc * S5_LANES, S5_NS + (cc + 1) * S5_LANES)
        arf, aif, arb, aib = a_ref[0, :, re], a_ref[1, :, re], a_ref[2, :, re], a_ref[3, :, re]

        def step(t, carry, re=re, im=im, arf=arf, aif=aif, arb=arb, aib=aib):
            xrf, xif, xrb, xib = carry
            rf = pl.multiple_of(t * 8, 8)
            rb = pl.multiple_of((S5_TT - 1 - t) * 8, 8)
            nrf = arf * xrf - aif * xif + buf_f[pl.ds(rf, 8), re]
            nif = arf * xif + aif * xrf + buf_f[pl.ds(rf, 8), im]
            nrb = arb * xrb - aib * xib + buf_b[pl.ds(rb, 8), re]
            nib = arb * xib + aib * xrb + buf_b[pl.ds(rb, 8), im]
            buf_f[pl.ds(rf, 8), re] = nrf
            buf_f[pl.ds(rf, 8), im] = nif
            buf_b[pl.ds(rb, 8), re] = nrb
            buf_b[pl.ds(rb, 8), im] = nib
            return nrf, nif, nrb, nib

        fin = lax.fori_loop(0, S5_TT, step,
                            (st[0, :, re], st[1, :, re], st[2, :, re], st[3, :, re]), unroll=4)
        for k in range(4):
            st[k, :, re] = fin[k]

    for y_ref, buf, d in ((yf_ref, buf_f, 0), (yb_ref, buf_b, 1)):
        y = jnp.dot(buf[...].astype(BF16), wc_ref[d], preferred_element_type=F32)
        y_ref[0] = y[:, :128]
        y_ref[1] = y[:, 128:]
    xl_ref[...] = st[...]


def _s5_scan(u2, wb, wc, a_bc, x0):
    L = u2.shape[1] // 8
    rows = S5_TT * 8
    n = L // S5_TT
    y_shape = jax.ShapeDtypeStruct((2, L * 8, 128), F32)
    return pl.pallas_call(
        _s5_kernel,
        out_shape=(y_shape, y_shape, jax.ShapeDtypeStruct((4, 8, S5_NS), F32)),
        grid=(n,),
        in_specs=[
            pl.BlockSpec((2, rows, 128), lambda j: (0, j, 0)),
            pl.BlockSpec((2, rows, 128), lambda j: (0, n - 1 - j, 0)),
            pl.BlockSpec((2, GROUP_W, 2 * S5_NS), lambda j: (0, 0, 0)),
            pl.BlockSpec((2, 2 * S5_NS, GROUP_W), lambda j: (0, 0, 0)),
            pl.BlockSpec((4, 8, S5_NS), lambda j: (0, 0, 0)),
            pl.BlockSpec((4, 8, S5_NS), lambda j: (0, 0, 0)),
        ],
        out_specs=(pl.BlockSpec((2, rows, 128), lambda j: (0, j, 0)),
                   pl.BlockSpec((2, rows, 128), lambda j: (0, n - 1 - j, 0)),
                   pl.BlockSpec((4, 8, S5_NS), lambda j: (0, 0, 0))),
        scratch_shapes=[pltpu.VMEM((rows, 2 * S5_NS), F32),
                        pltpu.VMEM((rows, 2 * S5_NS), F32),
                        pltpu.VMEM((4, 8, S5_NS), F32)],
        compiler_params=_cparams(("arbitrary",)),
        name="s5_scan",
    )(u2, u2, wb, wc, a_bc, x0)


def _s5_weights(a_re, a_im, log_step, b_re, b_im, c_re, c_im):
    eye = jnp.eye(S5_GROUPS, dtype=F32)
    wbs, wcs, abc = [], [], []
    for d in range(2):
        ab_re, ab_im, bb_re, bb_im = _s5_discretize(a_re[d], a_im[d], log_step[d], b_re, b_im)
        wb_re = jnp.einsum('gph,gk->ghkp', bb_re, eye).reshape(GROUP_W, S5_NS)
        wb_im = jnp.einsum('gph,gk->ghkp', bb_im, eye).reshape(GROUP_W, S5_NS)
        wbs.append(jnp.concatenate([wb_re, wb_im], axis=1))
        wc_re = jnp.einsum('ghp,gk->gpkh', c_re[d], eye).reshape(S5_NS, GROUP_W)
        wc_im = jnp.einsum('ghp,gk->gpkh', c_im[d], eye).reshape(S5_NS, GROUP_W)
        wcs.append(jnp.concatenate([wc_re, -wc_im], axis=0))
        abc += [jnp.broadcast_to(ab_re.reshape(1, S5_NS), (8, S5_NS)),
                jnp.broadcast_to(ab_im.reshape(1, S5_NS), (8, S5_NS))]
    return jnp.stack(wbs).astype(BF16), jnp.stack(wcs).astype(BF16), jnp.stack(abc)


def _group_norm(m, w):
    return m * lax.rsqrt(jnp.mean(m * m, axis=-1, keepdims=True) + EPS) * w


def _merge_kernel(hgf_ref, hgb_ref, yf_ref, yb_ref, u_ref, sd_ref, glu_ref, hy_ref, at_ref, gate_ref,
                  x_ref, mod_ref, mw_ref, fw_ref, wo_ref, xo_ref, ho_ref):
    gate = gate_ref[...]
    rows = pl.ds(pl.program_id(1), gate.shape[0], stride=8)
    pick = lambda r: jnp.concatenate([r.at[0][rows, :], r.at[1][rows, :]], axis=1)
    z = jax.nn.gelu(pick(yf_ref) + pick(yb_ref) + sd_ref[...] * pick(u_ref))
    s5 = z * jax.nn.sigmoid(jnp.dot(z.astype(BF16), glu_ref[...], preferred_element_type=F32))
    parts = [
        _group_norm(hgf_ref[...] + hgb_ref[...], mw_ref[:, 0:256]) * (gate * jax.nn.sigmoid(gate)),
        _group_norm(s5, mw_ref[:, 256:512]),
        _group_norm(jnp.concatenate([hy_ref[s].T for s in range(hy_ref.shape[0])], axis=0),
                    mw_ref[:, 512:768]),
        _group_norm(at_ref[...], mw_ref[:, 768:1024]),
    ]
    mix = jnp.concatenate(parts, axis=1).astype(BF16)
    xn = x_ref[...] + mod_ref[2:3, :] * jnp.dot(mix, wo_ref[...], preferred_element_type=F32)
    xo_ref[...] = xn
    h = xn * lax.rsqrt(jnp.mean(xn * xn, axis=-1, keepdims=True) + EPS) * fw_ref[...]
    ho_ref[...] = (h * mod_ref[3:4, :] + mod_ref[4:5, :]).astype(BF16)


def _merge(hg_f, hg_b, s5_yf, s5_yb, s5_u, s5_d, glu_bf16, hy, at, p, x, mods, mw, fw, wo_bf16, tm):
    b_, L, _ = x.shape
    grp = lambda: pl.BlockSpec((None, tm, GROUP_W), lambda i, b: (b, i, 0))
    allb = lambda: pl.BlockSpec((2, tm * b_, 128), lambda i, b: (0, i, 0))
    return pl.pallas_call(
        _merge_kernel,
        out_shape=(jax.ShapeDtypeStruct((b_, L, D_MODEL), F32),
                   jax.ShapeDtypeStruct((b_, L, D_MODEL), BF16)),
        grid=(L // tm, b_),
        in_specs=[
            grp(), grp(), allb(), allb(), allb(),
            pl.BlockSpec((1, GROUP_W), lambda i, b: (0, 0)),
            pl.BlockSpec((GROUP_W, GROUP_W), lambda i, b: (0, 0)),
            pl.BlockSpec((None, tm // 128, HY_CH, 128), lambda i, b: (b, i, 0, 0)),
            grp(),
            pl.BlockSpec((None, tm, GROUP_W), lambda i, b: (b, i, 4)),
            pl.BlockSpec((None, tm, D_MODEL), lambda i, b: (b, i, 0)),
            pl.BlockSpec((None, 8, D_MODEL), lambda i, b: (b, 0, 0)),
            pl.BlockSpec((1, D_MODEL), lambda i, b: (0, 0)),
            pl.BlockSpec((1, D_MODEL), lambda i, b: (0, 0)),
            pl.BlockSpec((D_MODEL, D_MODEL), lambda i, b: (0, 0)),
        ],
        out_specs=(pl.BlockSpec((None, tm, D_MODEL), lambda i, b: (b, i, 0)),
                   pl.BlockSpec((None, tm, D_MODEL), lambda i, b: (b, i, 0))),
        compiler_params=_cparams(("arbitrary", "arbitrary")),
        name="merge_out_proj",
    )(hg_f, hg_b, s5_yf, s5_yb, s5_u, s5_d.reshape(1, GROUP_W), glu_bf16, hy, at, p, x, mods,
      mw.reshape(1, D_MODEL), fw.reshape(1, D_MODEL), wo_bf16)


def _ffn_kernel(hp_ref, h_ref, hn_ref, x_ref, mod_ref, wa_ref, wv_ref, cw_ref, wd_ref, fw_ref, o_ref,
                acc_scr, *, tm, final_norm):
    i = pl.program_id(1)
    last = pl.num_programs(1) - 1
    h = h_ref[...]
    hp = jnp.where(i > 0, hp_ref[...], jnp.zeros_like(hp_ref))
    hn = jnp.where(i < last, hn_ref[...], jnp.zeros_like(hn_ref))
    h_ext = jnp.concatenate([hp, h, hn], axis=0)
    rows = tm + 2 * HALO

    def gated(j):
        a = jnp.dot(h_ext, wa_ref[j], preferred_element_type=F32)
        v = jnp.dot(h, wv_ref[j], preferred_element_type=F32)
        cw = cw_ref[j]
        conv = (pltpu.roll(a, 1, 0)[HALO:HALO + tm] * cw[0:1, :]
                + a[HALO:HALO + tm] * cw[1:2, :]
                + pltpu.roll(a, rows - 1, 0)[HALO:HALO + tm] * cw[2:3, :] + cw[3:4, :])
        return (conv * jax.nn.sigmoid(conv) * v).astype(BF16)

    for j in range(0, N_FF_CHUNKS, 2):
        n = min(2, N_FF_CHUNKS - j)
        g = jnp.concatenate([gated(j + d) for d in range(n)], axis=1)
        wd = wd_ref[j:j + n].reshape(n * FF_CHUNK, D_MODEL)
        part = jnp.dot(g, wd, preferred_element_type=F32)
        if j == 0:
            acc_scr[...] = part
        elif j + n < N_FF_CHUNKS:
            acc_scr[...] += part
        else:
            xo = x_ref[...] + mod_ref[5:6, :] * (acc_scr[...] + part)
            if final_norm:
                xo = xo * lax.rsqrt(jnp.mean(xo * xo, axis=-1, keepdims=True) + EPS) * fw_ref[...]
            o_ref[...] = xo


def _ffn(h, x, mods, wa, wv, cw, wd, tm, final_w=None):
    b_, L, _ = x.shape
    nh = L // HALO
    r = tm // HALO
    const3 = lambda b, i: (0, 0, 0)
    fw = jnp.ones((1, D_MODEL), F32) if final_w is None else final_w.reshape(1, D_MODEL)
    return pl.pallas_call(
        functools.partial(_ffn_kernel, tm=tm, final_norm=final_w is not None),
        out_shape=jax.ShapeDtypeStruct((b_, L, D_MODEL), F32),
        grid=(b_, L // tm),
        in_specs=[
            pl.BlockSpec((None, HALO, D_MODEL), lambda b, i: (b, jnp.maximum(i * r - 1, 0), 0)),
            pl.BlockSpec((None, tm, D_MODEL), lambda b, i: (b, i, 0)),
            pl.BlockSpec((None, HALO, D_MODEL), lambda b, i: (b, jnp.minimum((i + 1) * r, nh - 1), 0)),
            pl.BlockSpec((None, tm, D_MODEL), lambda b, i: (b, i, 0)),
            pl.BlockSpec((None, 8, D_MODEL), lambda b, i: (b, 0, 0)),
            pl.BlockSpec((N_FF_CHUNKS, D_MODEL, FF_CHUNK), const3, pipeline_mode=pl.Buffered(1)),
            pl.BlockSpec((N_FF_CHUNKS, D_MODEL, FF_CHUNK), const3, pipeline_mode=pl.Buffered(1)),
            pl.BlockSpec((N_FF_CHUNKS, 8, FF_CHUNK), const3),
            pl.BlockSpec((N_FF_CHUNKS, FF_CHUNK, D_MODEL), const3, pipeline_mode=pl.Buffered(1)),
            pl.BlockSpec((1, D_MODEL), lambda b, i: (0, 0)),
        ],
        out_specs=pl.BlockSpec((None, tm, D_MODEL), lambda b, i: (b, i, 0)),
        scratch_shapes=[pltpu.VMEM((tm, D_MODEL), F32)],
        compiler_params=_cparams(("arbitrary", "arbitrary")),
        name="conv_ffn",
    )(h, h, h, x, mods, wa, wv, cw, wd, fw)


def _s5_discretize(a_re, a_im, log_step, b_re, b_im):
    dt = jnp.exp(log_step)[:, None]
    mag = jnp.exp(a_re * dt)
    ang = a_im * dt
    ab_re, ab_im = mag * jnp.cos(ang), mag * jnp.sin(ang)
    den = a_re * a_re + a_im * a_im
    nr, ni = ab_re - 1.0, ab_im
    fr = (nr * a_re + ni * a_im) / den
    fi = (ni * a_re - nr * a_im) / den
    bb_re = fr[..., None] * b_re - fi[..., None] * b_im
    bb_im = fr[..., None] * b_im + fi[..., None] * b_re
    return ab_re, ab_im, bb_re, bb_im


HY_LANES = 128
HY_CB = 8
HY_EMB_PAD = 128


def _vadd(a, b):
    return b if a is None else a if b is None else a + b


def _vsub(a, b):
    return (None if b is None else -b) if a is None else a if b is None else a - b


def _vscale(a, c):
    return None if (a is None or c == 0.0) else a if c == 1.0 else -a if c == -1.0 else a * c


def _cmul_const(x, wr, wi):
    re, im = x
    return (_vsub(_vscale(re, wr), _vscale(im, wi)), _vadd(_vscale(im, wr), _vscale(re, wi)))


def _unit_root(k, n, sign):
    k %= n
    if (4 * k) % n == 0:
        return ((1.0, 0.0), (0.0, float(sign)), (-1.0, 0.0), (0.0, float(-sign)))[4 * k // n]
    ang = sign * 2.0 * math.pi * k / n
    return math.cos(ang), math.sin(ang)


def _fft_slabs(x, sign, n_out=None):
    n = len(x)
    if n == 1:
        return list(x)
    n_out = n if n_out is None else n_out
    ev = _fft_slabs(x[0::2], sign)
    od = _fft_slabs(x[1::2], sign)
    out = [None] * n_out
    for k in range(n // 2):
        t = _cmul_const(od[k], *_unit_root(k, n, sign))
        if k < n_out:
            out[k] = (_vadd(ev[k][0], t[0]), _vadd(ev[k][1], t[1]))
        if k + n // 2 < n_out:
            out[k + n // 2] = (_vsub(ev[k][0], t[0]), _vsub(ev[k][1], t[1]))
    return out


def _twiddle(slabs, tw_ref, conj):
    out = [slabs[0]]
    for k in range(1, len(slabs)):
        re, im = slabs[k]
        tr = tw_ref[0, k]
        ti = -tw_ref[1, k] if conj else tw_ref[1, k]
        if im is None:
            out.append((re * tr, re * ti))
        else:
            out.append((re * tr - im * ti, re * ti + im * tr))
    return out


def _dot_split(x, w_ref):
    hi = x.astype(BF16)
    lo = (x - hi.astype(F32)).astype(BF16)
    return (jnp.dot(hi, w_ref[0], preferred_element_type=F32)
            + jnp.dot(lo, w_ref[0], preferred_element_type=F32)
            + jnp.dot(hi, w_ref[1], preferred_element_type=F32))


def _lane_dft(slabs, w_ref):
    zero = jnp.zeros((HY_CB, HY_LANES), F32)
    rows = jnp.concatenate(
        [jnp.concatenate([zero if re is None else re, zero if im is None else im], axis=1)
         for re, im in slabs], axis=0)
    out = _dot_split(rows, w_ref)
    return [(out[HY_CB * i:HY_CB * (i + 1), :HY_LANES], out[HY_CB * i:HY_CB * (i + 1), HY_LANES:])
            for i in range(len(slabs))]


def _short_conv(ref, part, cw_ref, ch, lane):
    n = ref.shape[1]
    w0, w1, w2, bb = cw_ref[ch, 0], cw_ref[ch, 1], cw_ref[ch, 2], cw_ref[ch, 3]
    right = [pltpu.roll(ref[part, s], 1, 1) for s in range(n)]
    left = [pltpu.roll(ref[part, s], HY_LANES - 1, 1) for s in range(n)]
    zero = jnp.zeros((HY_CB, HY_LANES), F32)
    out = []
    for s in range(n):
        xm = jnp.where(lane == 0, right[s - 1] if s > 0 else zero, right[s])
        xp = jnp.where(lane == HY_LANES - 1, left[s + 1] if s < n - 1 else zero, left[s])
        out.append(xm * w0 + ref[part, s] * w1 + xp * w2 + bb)
    return out


def _hy_conv_kernel(x1_ref, x2_ref, z_ref, cw_ref, bias_ref, kf_ref, tw_ref, wf_ref, wi_ref, o_ref,
                    *, n1):
    nh = n1 // 2
    lane = lax.broadcasted_iota(jnp.int32, (HY_CB, HY_LANES), 1)
    gates = [[_short_conv(r, part, cw_ref, ch, lane) for part in range(2)]
             for ch, r in ((0, x1_ref), (1, x2_ref))]
    z = [_short_conv(z_ref, part, cw_ref, 2, lane) for part in range(2)]
    for o in range(HY_ORDER):
        spec = _fft_slabs([(z[0][s], z[1][s]) for s in range(nh)] + [(None, None)] * nh, -1)
        spec = _lane_dft(_twiddle(spec, tw_ref, False), wf_ref)
        spec = [(re * kf_ref[o, 0, k] - im * kf_ref[o, 1, k], re * kf_ref[o, 1, k] + im * kf_ref[o, 0, k])
                for k, (re, im) in enumerate(spec)]
        spec = _twiddle(_lane_dft(spec, wi_ref), tw_ref, True)
        y = _fft_slabs(spec, 1, n_out=nh)
        bo = bias_ref[o]
        z = [[gates[o][part][s] * (y[s][part] + bo * z[part][s]) for s in range(nh)]
             for part in range(2)]
    for part in range(2):
        for s in range(nh):
            o_ref[part, s] = z[part][s]


def _hy_conv(zh, cw, bias_b, kf, tw, wf, wi):
    b_, nh = zh.shape[0], zh.shape[1]
    half, ncb = b_ // 2, HY_CH // HY_CB
    n1 = 2 * nh
    zspec = lambda ch: pl.BlockSpec((2, None, nh, HY_CB, HY_LANES),
                                    lambda c, b: (0, b, 0, ch * ncb + c, 0))
    const3 = lambda c, b: (0, 0, 0)
    z5 = zh.reshape(2, half, nh, 3 * HY_CH, HY_LANES)
    out = pl.pallas_call(
        functools.partial(_hy_conv_kernel, n1=n1),
        out_shape=jax.ShapeDtypeStruct((2, half, nh, HY_CH, HY_LANES), F32),
        grid=(ncb, half),
        in_specs=[
            zspec(0), zspec(1), zspec(2),
            pl.BlockSpec((3, 4, HY_CB, HY_LANES), lambda c, b: (0, 0, c, 0)),
            pl.BlockSpec((HY_ORDER, HY_CB, HY_LANES), lambda c, b: (0, c, 0)),
            pl.BlockSpec((HY_ORDER, 2, n1, HY_CB, HY_LANES), lambda c, b: (0, 0, 0, c, 0)),
            pl.BlockSpec((2, n1, 8, HY_LANES), lambda c, b: (0, 0, 0, 0)),
            pl.BlockSpec((2, 256, 256), const3),
            pl.BlockSpec((2, 256, 256), const3),
        ],
        out_specs=pl.BlockSpec((2, None, nh, HY_CB, HY_LANES), lambda c, b: (0, b, 0, c, 0)),
        compiler_params=_cparams(("arbitrary", "arbitrary")),
        name="hyena_conv",
    )(z5, z5, z5, cw, bias_b, kf, tw, wf, wi)
    return out.reshape(b_, nh, HY_CH, HY_LANES)


def _hy_spectrum_kernel(k_ref, tw_ref, wf_ref, o_ref, *, n1):
    spec = _fft_slabs([(k_ref[s], None) for s in range(n1)], -1)
    spec = _lane_dft(_twiddle(spec, tw_ref, False), wf_ref)
    scale = 1.0 / (n1 * HY_LANES)
    for k in range(n1):
        o_ref[0, k] = spec[k][0] * scale
        o_ref[1, k] = spec[k][1] * scale


def _hy_spectrum(taps, tw, wf):
    n1 = taps.shape[1]
    return pl.pallas_call(
        functools.partial(_hy_spectrum_kernel, n1=n1),
        out_shape=jax.ShapeDtypeStruct((HY_ORDER, 2, n1, HY_CH, HY_LANES), F32),
        grid=(HY_ORDER, HY_CH // HY_CB),
        in_specs=[
            pl.BlockSpec((None, n1, HY_CB, HY_LANES), lambda o, c: (o, 0, c, 0)),
            pl.BlockSpec((2, n1, 8, HY_LANES), lambda o, c: (0, 0, 0, 0)),
            pl.BlockSpec((2, 256, 256), lambda o, c: (0, 0, 0)),
        ],
        out_specs=pl.BlockSpec((None, 2, n1, HY_CB, HY_LANES), lambda o, c: (o, 0, 0, c, 0)),
        compiler_params=_cparams(("arbitrary", "arbitrary")),
        name="hyena_spectrum",
    )(taps, tw, wf)


def _dot_f32(a, b, dims=(((1,), (0,)), ((), ()))):
    a_hi = a.astype(BF16)
    a_lo = (a - a_hi.astype(F32)).astype(BF16)
    b_hi = b.astype(BF16)
    b_lo = (b - b_hi.astype(F32)).astype(BF16)
    dot = lambda p, q: lax.dot_general(p, q, dims, preferred_element_type=F32)
    return dot(a_hi, b_hi) + dot(a_lo, b_hi) + dot(a_hi, b_lo)


def _hy_mlp_kernel(emb_ref, w1_ref, w2_ref, w3t_ref, vec_ref, pos_ref, dl_ref, o_ref):
    z = emb_ref[...]
    h = jnp.sin(vec_ref[2:3, :] * (_dot_f32(z, w1_ref[...]) + vec_ref[0:1, :]))
    h = jnp.sin(vec_ref[3:4, :] * (_dot_f32(h, w2_ref[...]) + vec_ref[1:2, :]))
    ht = _dot_f32(w3t_ref[...], h, _NT)
    dl = dl_ref[...]
    for s in range(o_ref.shape[1]):
        lanes = slice(HY_LANES * s, HY_LANES * (s + 1))
        win = jnp.exp(-pos_ref[0:1, lanes] * dl)
        wf, wb = pos_ref[1:2, lanes] * win, pos_ref[2:3, lanes] * win
        for o in range(HY_ORDER):
            base = 2 * HY_CH * o
            o_ref[o, s] = (ht[base:base + HY_CH, lanes] * wf
                           + ht[base + HY_CH:base + 2 * HY_CH, lanes] * wb)


def _hy_filter_taps(L, w1, b1, freq, w2, b2, w3):
    pad = HY_EMB_PAD
    n = 2 * L
    t01 = np.linspace(0.0, 1.0, L, dtype=np.float32)[:, None]
    w = (np.float32(2.0 * math.pi) * np.arange(L, dtype=np.float32)[:, None]) / np.float32(L)
    bands = (HY_EMB - 1) // 2
    fr = np.linspace(1e-4, bands - 1, bands, dtype=np.float32)[None, :]
    arg = (fr * w).astype(np.float64)
    emb = np.zeros((L, pad), np.float32)
    emb[:, 0:1] = t01
    emb[:, 1:1 + bands] = np.cos(arg)
    emb[:, 1 + bands:HY_EMB] = -np.sin(arg)
    pos_idx = np.arange(n)
    lag = np.where(pos_idx < L, pos_idx, (n - pos_idx) % L)
    pos = np.zeros((8, n), np.float32)
    pos[0] = t01[lag, 0]
    pos[1] = pos_idx < L
    pos[2] = (pos_idx > L) | (pos_idx == 0)
    hid = w1.shape[1]
    n_out = w3.shape[1]
    w1p = jnp.zeros((pad, pad), F32).at[:HY_EMB, :hid].set(w1)
    w2p = jnp.zeros((pad, pad), F32).at[:hid, :hid].set(w2)
    w3t = jnp.zeros((n_out, pad), F32).at[:, :hid].set(w3.T)
    vec = jnp.zeros((8, pad), F32).at[0, :hid].set(b1).at[1, :hid].set(b2)
    vec = vec.at[2, :hid].set(freq[0]).at[3, :hid].set(freq[1])
    deltas = np.abs(np.linspace(HY_MIN_DECAY, HY_MAX_DECAY, HY_CH, dtype=np.float32))
    dl = np.broadcast_to(deltas[:, None], (HY_CH, HY_LANES))
    tl = min(n, 512)
    return pl.pallas_call(
        _hy_mlp_kernel,
        out_shape=jax.ShapeDtypeStruct((HY_ORDER, n // HY_LANES, HY_CH, HY_LANES), F32),
        grid=(n // tl,),
        in_specs=[
            pl.BlockSpec((tl, pad), lambda i: (i, 0)),
            pl.BlockSpec((pad, pad), lambda i: (0, 0)),
            pl.BlockSpec((pad, pad), lambda i: (0, 0)),
            pl.BlockSpec((n_out, pad), lambda i: (0, 0)),
            pl.BlockSpec((8, pad), lambda i: (0, 0)),
            pl.BlockSpec((8, tl), lambda i: (0, i)),
            pl.BlockSpec((HY_CH, HY_LANES), lambda i: (0, 0)),
        ],
        out_specs=pl.BlockSpec((HY_ORDER, tl // HY_LANES, HY_CH, HY_LANES), lambda i: (0, i, 0, 0)),
        compiler_params=_cparams(("arbitrary",)),
        name="hyena_filter_mlp",
    )(jnp.asarray(emb[lag]), w1p, w2p, w3t, vec, jnp.asarray(pos), jnp.asarray(dl))


def _hy_constants(n1):
    n = n1 * HY_LANES
    ang = -2.0 * np.pi * np.outer(np.arange(n1), np.arange(HY_LANES)) / n
    tw = np.stack([np.cos(ang), np.sin(ang)])[:, :, None, :].repeat(8, axis=2).astype(np.float32)
    a2 = -2.0 * np.pi * np.outer(np.arange(HY_LANES), np.arange(HY_LANES)) / HY_LANES
    fr, fi = np.cos(a2), np.sin(a2)
    fwd = np.block([[fr, fi], [-fi, fr]])
    inv = np.block([[fr, -fi], [fi, fr]])

    def split(m):
        hi = m.astype(BF16)
        lo = (m - hi.astype(np.float64)).astype(BF16)
        return jnp.asarray(np.stack([hi, lo]))

    return jnp.asarray(tw), split(fwd), split(inv)


def _att_col_perm():
    inner = np.concatenate([np.arange(0, ATT_HD, 2), np.arange(1, ATT_HD, 2)])
    q = np.concatenate([h * ATT_HD + inner for h in (0, 2, 1, 3)])
    k = np.concatenate([h * ATT_HD + inner for h in (0, 1)])
    return np.concatenate([np.arange(W_COL_S5), W_COL_TQ + q, W_COL_TK + k,
                           np.arange(W_COL_TV, IN_COLS), np.arange(W_COL_S5, W_COL_HY)])


def _att_out_perm():
    at = np.concatenate([h * ATT_HD + np.arange(ATT_HD) for h in (0, 2, 1, 3)])
    return np.concatenate([np.arange(3 * GROUP_W), 3 * GROUP_W + at])


def _rope_tables(L):
    rows = L // GRID_W
    row = jnp.repeat(jnp.arange(rows), GRID_W).astype(F32)
    col = jnp.tile(jnp.arange(GRID_W), rows).astype(F32)
    axis_dim = ATT_HD // 2
    inv = 1.0 / (ROPE_BASE ** (jnp.arange(0, axis_dim, 2, dtype=F32) / axis_dim))
    ang = jnp.concatenate([row[:, None] * inv, col[:, None] * inv], axis=-1)
    c, s = jnp.cos(ang), jnp.sin(ang)
    return jnp.tile(jnp.concatenate([c, c], axis=1), (1, 2)), jnp.tile(jnp.concatenate([-s, s], axis=1), (1, 2))


def kernel(x, c, ctx, c_ctx, ada_w, ada_b, norm_mix_w, norm_ffn_w, w_in, hg_lb_logits, s5_a_re, s5_a_im, s5_log_step, s5_b_re, s5_b_im, s5_c_re, s5_c_im, s5_d, s5_glu_w, hy_conv_w, hy_conv_b, hy_w1, hy_b1, hy_freq, hy_w2, hy_b2, hy_w3, hy_bias, att_sink, merge_norm_w, w_out, ffn_w_up, ffn_conv_w, ffn_conv_b, ffn_w_down, final_norm_w):
    bsz, seq_len, _ = x.shape
    ctx_len = ctx.shape[1]

    cond = jnp.zeros((16, D_MODEL), F32).at[:bsz].set(c).at[bsz].set(c_ctx)
    ada = _ada_all(cond, ada_w, ada_b)

    def mods_of(m):
        sh1, sc1, g1, sh2, sc2, g2 = jnp.split(m, 6, axis=-1)
        z = jnp.zeros_like(sh1)
        return jnp.stack([1.0 + sc1, sh1, g1, 1.0 + sc2, sh2, g2, z, z], axis=1)

    sm = jax.nn.softmax(hg_lb_logits, axis=0)
    lower_bounds = jnp.cumsum(sm, axis=0) - sm[0:1]
    cos_t, sin_t = _rope_tables(seq_len)
    col_perm = _att_col_perm()
    out_perm = _att_out_perm()
    hg_zero = jnp.zeros((bsz, 2, 2, 128, 128), F32)
    hy_tw_lat, hy_wf, hy_wi = _hy_constants(2 * seq_len // HY_LANES)
    hy_tw_ctx, _, _ = _hy_constants(2 * ctx_len // HY_LANES)
    s5_zero = jnp.zeros((4, bsz, S5_NS), F32)

    xc = ctx
    for l in range(DEPTH):
        last = l == DEPTH - 1
        mods_lat = mods_of(ada[l, :bsz])
        mods_ctx = jnp.broadcast_to(mods_of(ada[l, bsz:bsz + 1]), (bsz, 8, D_MODEL))
        w_in_l = w_in[l][:, col_perm].astype(BF16)
        mw = merge_norm_w[l][out_perm]
        wo = w_out[l][out_perm, :].astype(BF16)
        wa = ffn_w_up[l][:, :D_FF].reshape(D_MODEL, N_FF_CHUNKS, FF_CHUNK).transpose(1, 0, 2).astype(BF16)
        wv = ffn_w_up[l][:, D_FF:].reshape(D_MODEL, N_FF_CHUNKS, FF_CHUNK).transpose(1, 0, 2).astype(BF16)
        wd = ffn_w_down[l].reshape(N_FF_CHUNKS, FF_CHUNK, D_MODEL).astype(BF16)
        cw = jnp.concatenate([ffn_conv_w[l], ffn_conv_b[l][None], jnp.zeros((4, D_FF), F32)], axis=0)
        cw = cw.reshape(8, N_FF_CHUNKS, FF_CHUNK).transpose(1, 0, 2)

        wh = w_in[l][:, W_COL_HY:W_COL_HY + 3 * HY_CH].T.astype(BF16)
        p_lat, u_lat, zh_lat = _inproj(x, mods_lat, norm_mix_w[l], w_in_l, wh, 256)
        p_ctx, u_ctx, zh_ctx = _inproj(xc, mods_ctx, norm_mix_w[l], w_in_l, wh, ctx_len)

        hgf_ctx, hgb_ctx, hg_state = _hgrn2(p_ctx, lower_bounds[l], hg_zero)
        hgf_lat, hgb_lat, _ = _hgrn2(p_lat, lower_bounds[l], hg_state)

        s5_wb, s5_wc, s5_a = _s5_weights(s5_a_re[l], s5_a_im[l], s5_log_step[l], s5_b_re[l],
                                         s5_b_im[l], s5_c_re[l], s5_c_im[l])
        glu = s5_glu_w[l].astype(BF16)
        yf_ctx, yb_ctx, s5_state = _s5_scan(u_ctx, s5_wb, s5_wc, s5_a, s5_zero)
        yf_lat, yb_lat, _ = _s5_scan(u_lat, s5_wb, s5_wc, s5_a, s5_state)

        hy_params = (hy_w1[l], hy_b1[l], hy_freq[l], hy_w2[l], hy_b2[l], hy_w3[l])
        hy_cw = jnp.concatenate([hy_conv_w[l], hy_conv_b[l][None]], axis=0)
        hy_cw = jnp.broadcast_to(hy_cw.reshape(4, 3, HY_CH).transpose(1, 0, 2)[..., None],
                                 (3, 4, HY_CH, HY_LANES))
        hy_bb = jnp.broadcast_to(hy_bias[l][..., None], (HY_ORDER, HY_CH, HY_LANES))
        kf_lat = _hy_spectrum(_hy_filter_taps(seq_len, *hy_params), hy_tw_lat, hy_wf)
        hy_lat = _hy_conv(zh_lat, hy_cw, hy_bb, kf_lat, hy_tw_lat, hy_wf, hy_wi)

        at_lat = _att_lat(p_lat, p_ctx, cos_t, sin_t, att_sink[l])

        x_mid, h_lat = _merge(hgf_lat, hgb_lat, yf_lat, yb_lat, u_lat, s5_d[l], glu, hy_lat, at_lat, p_lat, x,
                              mods_lat, mw, norm_ffn_w[l], wo, 256)
        x_new = _ffn(h_lat, x_mid, mods_lat, wa, wv, cw, wd, 512,
                     final_w=final_norm_w if last else None)
        if not last:
            kf_ctx = _hy_spectrum(_hy_filter_taps(ctx_len, *hy_params), hy_tw_ctx, hy_wf)
            hy_ctx = _hy_conv(zh_ctx, hy_cw, hy_bb, kf_ctx, hy_tw_ctx, hy_wf, hy_wi)
            at_ctx = _att_ctx(p_ctx, att_sink[l])
            xc_mid, h_ctx = _merge(hgf_ctx, hgb_ctx, yf_ctx, yb_ctx, u_ctx, s5_d[l], glu, hy_ctx, at_ctx, p_ctx,
                                   xc, mods_ctx, mw, norm_ffn_w[l], wo, ctx_len)
            xc = _ffn(h_ctx, xc_mid, mods_ctx, wa, wv, cw, wd, ctx_len)
        x = x_new
    return x
```

```python
import functools
import math

import jax
import jax.numpy as jnp
import numpy as np
from jax import lax
from jax.experimental import pallas as pl
from jax.experimental.pallas import tpu as pltpu

F32 = jnp.float32
BF16 = jnp.bfloat16

D_MODEL = 1024
DEPTH = 4
GRID_W = 64
GROUP_W = 256
HG_DK = 64
HG_DV = 64
HG_HEADS = 4
HG_CHUNK = 16
S5_GROUP_CH = 16
S5_GROUPS = 16
S5_STATE = 64
HY_CH = 256
HY_ORDER = 2
HY_EMB = 33
HY_MAX_DECAY = math.log(1e-2) / 0.3
HY_MIN_DECAY = math.log(1e-2) / 1.5
ATT_HD = 64
ATT_HEADS = 4
ATT_KV = 2
WINDOW = 128
ATT_BLOCK = 128
ATT_QB = 2
ATT_SCALE = 1.0 / math.sqrt(ATT_HD)
ROPE_BASE = 10000.0
D_FF = 2816
EPS = 1e-6
IN_COLS = 2816
W_COL_S5 = 1280
W_COL_HY = 1536
W_COL_TQ = 2304
W_COL_TK = 2560
W_COL_TV = 2688
P_COLS = 1792
COL_TQ = 1280
COL_TK = 1536
COL_TV = 1664

FF_CHUNK = 256
N_FF_CHUNKS = D_FF // FF_CHUNK
HALO = 16
VMEM_LIMIT = 56 * 1024 * 1024


def _cparams(sem):
    return pltpu.CompilerParams(dimension_semantics=sem, vmem_limit_bytes=VMEM_LIMIT)


def _ada_kernel(c_ref, w_ref, b_ref, o_ref):
    cond = c_ref[...]
    act = cond * jax.nn.sigmoid(cond)
    o_ref[...] = jnp.dot(act.astype(BF16), w_ref[...].astype(BF16),
                         preferred_element_type=F32) + b_ref[...]


def _ada_all(cond, ada_w, ada_b):
    tn = 1024
    n6 = 6 * D_MODEL
    return pl.pallas_call(
        _ada_kernel,
        out_shape=jax.ShapeDtypeStruct((DEPTH, 16, n6), F32),
        grid=(DEPTH, n6 // tn),
        in_specs=[
            pl.BlockSpec((16, D_MODEL), lambda l, j: (0, 0)),
            pl.BlockSpec((None, D_MODEL, tn), lambda l, j: (l, 0, j)),
            pl.BlockSpec((None, 1, tn), lambda l, j: (l, 0, j)),
        ],
        out_specs=pl.BlockSpec((None, 16, tn), lambda l, j: (l, 0, j)),
        compiler_params=_cparams(("arbitrary", "arbitrary")),
        name="ada_mod",
    )(cond, ada_w, ada_b.reshape(DEPTH, 1, n6))


def _inproj_kernel(x_ref, mod_ref, nw_ref, w_ref, wh_ref, o_ref, u_ref, zh_ref):
    x = x_ref[...]
    ms = jnp.mean(x * x, axis=-1, keepdims=True)
    y = x * lax.rsqrt(ms + EPS) * nw_ref[...]
    y = (y * mod_ref[0:1, :] + mod_ref[1:2, :]).astype(BF16)
    p = jnp.dot(y, w_ref[...], preferred_element_type=F32)
    o_ref[...] = p[:, :P_COLS]
    for c in range(2):
        u_ref.at[c][pl.ds(pl.program_id(1), x.shape[0], stride=8), :] = (
            p[:, P_COLS + 128 * c:P_COLS + 128 * (c + 1)])
    zt = lax.dot_general(wh_ref[...], y, _NT, preferred_element_type=F32)
    for s in range(zh_ref.shape[0]):
        zh_ref[s] = zt[:, 128 * s:128 * (s + 1)]


def _inproj(x, mods, nw, w_bf16, wh_bf16, tm):
    b_, L, _ = x.shape
    return pl.pallas_call(
        _inproj_kernel,
        out_shape=(jax.ShapeDtypeStruct((b_, L, P_COLS), F32),
                   jax.ShapeDtypeStruct((2, L * b_, 128), F32),
                   jax.ShapeDtypeStruct((b_, L // 128, 3 * HY_CH, 128), F32)),
        grid=(L // tm, b_),
        in_specs=[
            pl.BlockSpec((None, tm, D_MODEL), lambda i, b: (b, i, 0)),
            pl.BlockSpec((None, 8, D_MODEL), lambda i, b: (b, 0, 0)),
            pl.BlockSpec((1, D_MODEL), lambda i, b: (0, 0)),
            pl.BlockSpec((D_MODEL, P_COLS + GROUP_W), lambda i, b: (0, 0)),
            pl.BlockSpec((3 * HY_CH, D_MODEL), lambda i, b: (0, 0)),
        ],
        out_specs=(pl.BlockSpec((None, tm, P_COLS), lambda i, b: (b, i, 0)),
                   pl.BlockSpec((2, tm * b_, 128), lambda i, b: (0, i, 0)),
                   pl.BlockSpec((None, tm // 128, 3 * HY_CH, 128), lambda i, b: (b, i, 0, 0))),
        compiler_params=_cparams(("arbitrary", "arbitrary")),
        name="in_proj",
    )(x, mods, nw.reshape(1, D_MODEL), w_bf16, wh_bf16)


def _half_swap(t):
    w = t.shape[-1]
    lane = lax.broadcasted_iota(jnp.int32, t.shape, t.ndim - 1)
    return jnp.where((lane % ATT_HD) < ATT_HD // 2,
                     pltpu.roll(t, w - ATT_HD // 2, t.ndim - 1),
                     pltpu.roll(t, ATT_HD // 2, t.ndim - 1))


def _rope(t, cos, sin):
    return t * cos + _half_swap(t) * sin


def _attend(q, keys, vals, valid, sink_ref):
    lane = lax.broadcasted_iota(jnp.int32, (1, 128), 1)
    lo = lane < ATT_HD
    kb = keys.astype(BF16)
    v_lo = jnp.where(lo, vals, 0.0).astype(BF16)
    v_hi = jnp.where(lo, 0.0, vals).astype(BF16)
    v_cat = jnp.concatenate([v_lo, v_hi], axis=0)
    outs = []
    for m in range(2):
        qm = q[:, 128 * m:128 * (m + 1)]
        probs = []
        for half in range(2):
            head = (0, 2, 1, 3)[2 * m + half]
            qh = jnp.where(lo if half == 0 else jnp.logical_not(lo), qm, 0.0).astype(BF16)
            s = lax.dot_general(qh, kb, (((1,), (1,)), ((), ())), preferred_element_type=F32)
            if valid is not None:
                s = jnp.where(valid, s, -jnp.inf)
            sink = sink_ref[head]
            mx = jnp.maximum(jnp.max(s, axis=-1, keepdims=True), sink)
            e = jnp.exp(s - mx)
            den = jnp.sum(e, axis=-1, keepdims=True) + jnp.exp(sink - mx)
            probs.append((e / den).astype(BF16))
        p_cat = jnp.concatenate(probs, axis=1)
        outs.append(jnp.dot(p_cat, v_cat, preferred_element_type=F32))
    return jnp.concatenate(outs, axis=1)


def _att_lat_kernel(sink_ref, q_ref, kp_ref, kc_ref, kn_ref, vp_ref, vc_ref, vn_ref,
                    kx_ref, vx_ref, cp_ref, cc_ref, cn_ref, sp_ref, sc_ref, sn_ref, o_ref,
                    *, seq_len):
    n = pl.program_id(1) * ATT_QB
    cos_c, sin_c = cc_ref[...], sc_ref[...]
    q = _rope(q_ref[...], jnp.concatenate([cos_c, cos_c], axis=1),
              jnp.concatenate([sin_c, sin_c], axis=1)) * ATT_SCALE
    keys = jnp.concatenate([
        kx_ref[...],
        _rope(kp_ref[...], cp_ref[...], sp_ref[...]),
        _rope(kc_ref[...], cos_c, sin_c),
        _rope(kn_ref[...], cn_ref[...], sn_ref[...])], axis=0)
    vals = jnp.concatenate([vx_ref[...], vp_ref[...], vc_ref[...], vn_ref[...]], axis=0)
    lc = kx_ref.shape[0]
    s_tot = lc + (ATT_QB + 2) * ATT_BLOCK
    col = lax.broadcasted_iota(jnp.int32, (ATT_QB * ATT_BLOCK, s_tot), 1)
    row = lax.broadcasted_iota(jnp.int32, (ATT_QB * ATT_BLOCK, s_tot), 0)
    qpos = n * ATT_BLOCK + row
    kpos = (n - 1) * ATT_BLOCK + (col - lc)
    valid = (col < lc) | ((jnp.abs(kpos - qpos) <= WINDOW) & (kpos >= 0) & (kpos < seq_len))
    o_ref[...] = _attend(q, keys, vals, valid, sink_ref)


def _att_lat(p_lat, p_ctx, cos_t, sin_t, sink):
    b_, L, _ = p_lat.shape
    lc = p_ctx.shape[1]
    nb = L // ATT_BLOCK
    kcol, vcol = COL_TK // 128, COL_TV // 128
    qrows = ATT_QB * ATT_BLOCK
    prev = lambda b, n: jnp.maximum(n * ATT_QB - 1, 0)
    nxt = lambda b, n: jnp.minimum((n + 1) * ATT_QB, nb - 1)
    edge = lambda c, f: pl.BlockSpec((None, ATT_BLOCK, 128), lambda b, n: (b, f(b, n), c))
    cur = lambda c: pl.BlockSpec((None, qrows, 128), lambda b, n: (b, n, c))
    tab_edge = lambda f: pl.BlockSpec((ATT_BLOCK, 128), lambda b, n: (f(b, n), 0))
    tab_cur = lambda: pl.BlockSpec((qrows, 128), lambda b, n: (n, 0))
    return pl.pallas_call(
        functools.partial(_att_lat_kernel, seq_len=L),
        out_shape=jax.ShapeDtypeStruct((b_, L, GROUP_W), F32),
        grid=(b_, nb // ATT_QB),
        in_specs=[
            pl.BlockSpec(memory_space=pltpu.SMEM),
            pl.BlockSpec((None, qrows, 256), lambda b, n: (b, n, COL_TQ // 256)),
            edge(kcol, prev), cur(kcol), edge(kcol, nxt),
            edge(vcol, prev), cur(vcol), edge(vcol, nxt),
            pl.BlockSpec((None, lc, 128), lambda b, n: (b, 0, kcol)),
            pl.BlockSpec((None, lc, 128), lambda b, n: (b, 0, vcol)),
            tab_edge(prev), tab_cur(), tab_edge(nxt), tab_edge(prev), tab_cur(), tab_edge(nxt),
        ],
        out_specs=pl.BlockSpec((None, qrows, GROUP_W), lambda b, n: (b, n, 0)),
        compiler_params=_cparams(("arbitrary", "arbitrary")),
        name="att_latent",
    )(sink, p_lat, p_lat, p_lat, p_lat, p_lat, p_lat, p_lat, p_ctx, p_ctx,
      cos_t, cos_t, cos_t, sin_t, sin_t, sin_t)


def _att_ctx_kernel(sink_ref, q_ref, k_ref, v_ref, o_ref):
    o_ref[...] = _attend(q_ref[...] * ATT_SCALE, k_ref[...], v_ref[...], None, sink_ref)


def _att_ctx(p_ctx, sink):
    b_, lc, _ = p_ctx.shape
    return pl.pallas_call(
        _att_ctx_kernel,
        out_shape=jax.ShapeDtypeStruct((b_, lc, GROUP_W), F32),
        grid=(b_,),
        in_specs=[
            pl.BlockSpec(memory_space=pltpu.SMEM),
            pl.BlockSpec((None, lc, 256), lambda b: (b, 0, COL_TQ // 256)),
            pl.BlockSpec((None, lc, 128), lambda b: (b, 0, COL_TK // 128)),
            pl.BlockSpec((None, lc, 128), lambda b: (b, 0, COL_TV // 128)),
        ],
        out_specs=pl.BlockSpec((None, lc, GROUP_W), lambda b: (b, 0, 0)),
        compiler_params=_cparams(("arbitrary",)),
        name="att_context",
    )(sink, p_ctx, p_ctx, p_ctx)


HG_T = 128
HG_LEVELS = (64, 32, 16)
_NT = (((1,), (1,)), ((), ()))
_TN = (((0,), (0,)), ((), ()))


def _row_fill(ref, rows, blk):
    w = ref.shape[-1]
    return jnp.concatenate([jnp.broadcast_to(ref[r:r + 1, :], (blk, w)) for r in rows], axis=0)


def _gla_block(q_raw, z, v, lb, st_ref, cum_scr, rev):
    t = HG_T
    e = jnp.exp(-jnp.abs(z))
    big = 1.0 / (1.0 + e)
    small = e * big
    sg = jnp.where(z >= 0, big, small)
    g = jnp.log(lb + (1.0 - lb) * sg)
    k = (1.0 - lb) * jnp.where(z >= 0, small, big)
    q = q_raw * jax.nn.sigmoid(q_raw)

    g_hi = g.astype(BF16)
    r1 = g - g_hi.astype(F32)
    g_mid = r1.astype(BF16)
    g_lo = (r1 - g_mid.astype(F32)).astype(BF16)
    ti = lax.broadcasted_iota(jnp.int32, (t, t), 0)
    si = lax.broadcasted_iota(jnp.int32, (t, t), 1)
    tri = jnp.where((si >= ti) if rev else (si <= ti), 1.0, 0.0).astype(BF16)
    c3 = jnp.dot(tri, jnp.concatenate([g_hi, g_mid, g_lo], axis=1), preferred_element_type=F32)
    cum = c3[:, 0:256] + c3[:, 256:512] + c3[:, 512:768]
    cum_scr[...] = cum
    edge = 0 if rev else t - 1
    g_tot = cum_scr[edge:edge + 1, :]

    lane = lax.broadcasted_iota(jnp.int32, (1, 128), 1)
    lo = lane < HG_DK
    rowi = lax.broadcasted_iota(jnp.int32, (t, 1), 0)
    vi = lax.broadcasted_iota(jnp.int32, (128, 128), 0)
    ki = lax.broadcasted_iota(jnp.int32, (128, 128), 1)
    same_head = (vi < HG_DV) == (ki < HG_DK)

    lo_b = jnp.where(lo, 1.0, 0.0).astype(BF16)
    hi_b = jnp.where(lo, 0.0, 1.0).astype(BF16)

    def scores(qt, kt):
        q2 = jnp.concatenate([qt * lo_b, qt * hi_b], axis=0)
        return lax.dot_general(q2, kt, _NT, preferred_element_type=F32)

    outs = []
    for pr in range(2):
        sl = slice(128 * pr, 128 * (pr + 1))
        qp, kp, vp, cp, gp = q[:, sl], k[:, sl], v[:, sl], cum[:, sl], g_tot[:, sl]
        qb, kb = qp.astype(BF16), kp.astype(BF16)
        a0 = jnp.zeros((t, t), F32)
        a1 = jnp.zeros((t, t), F32)
        for sh, h in zip((6, 5, 4), HG_LEVELS):
            rows = [b * 2 * h + (h if rev else h - 1) for b in range(t // (2 * h))]
            e = jnp.exp(-jnp.abs(cp - _row_fill(cum_scr.at[:, sl], rows, 2 * h)))
            odd = (jnp.right_shift(rowi, sh) & 1) == 1
            q_on = jnp.logical_not(odd) if rev else odd
            sc = scores(qb * jnp.where(q_on, e, 0.0).astype(BF16),
                        kb * jnp.where(q_on, 0.0, e).astype(BF16))
            if 2 * h < t:
                keep = jnp.right_shift(ti, sh + 1) == jnp.right_shift(si, sh + 1)
                a0 += jnp.where(keep, sc[:t], 0.0)
                a1 += jnp.where(keep, sc[t:], 0.0)
            else:
                a0 += sc[:t]
                a1 += sc[t:]
        rows = [b * HG_CHUNK + (HG_CHUNK // 2 if rev else HG_CHUNK // 2 - 1) for b in range(t // HG_CHUNK)]
        dd = cp - _row_fill(cum_scr.at[:, sl], rows, HG_CHUNK)
        sc = scores((qp * jnp.exp(dd)).astype(BF16), (kp * jnp.exp(-dd)).astype(BF16))
        keep = (jnp.right_shift(ti, 4) == jnp.right_shift(si, 4)) & ((si >= ti) if rev else (si <= ti))
        a0 += jnp.where(keep, sc[:t], 0.0)
        a1 += jnp.where(keep, sc[t:], 0.0)

        v2 = jnp.concatenate([jnp.where(lo, vp, 0.0), jnp.where(lo, 0.0, vp)], axis=0).astype(BF16)
        o = jnp.dot(jnp.concatenate([a0, a1], axis=1).astype(BF16), v2, preferred_element_type=F32)
        st = st_ref[pr]
        o += lax.dot_general((qp * jnp.exp(cp)).astype(BF16), st.astype(BF16), _NT,
                             preferred_element_type=F32)
        upd = lax.dot_general(vp.astype(BF16), (kp * jnp.exp(gp - cp)).astype(BF16), _TN,
                              preferred_element_type=F32)
        st_ref[pr] = st * jnp.exp(gp) + jnp.where(same_head, upd, 0.0)
        outs.append(o)
    return jnp.concatenate(outs, axis=1)


def _hgrn2_kernel(qf_ref, zf_ref, vf_ref, qb_ref, zb_ref, vb_ref, lb_ref, s0_ref,
                  of_ref, ob_ref, sl_ref, st_f, st_b, cum_scr):
    j = pl.program_id(1)

    @pl.when(j == 0)
    def _():
        st_f[...] = s0_ref[0]
        st_b[...] = s0_ref[1]

    of_ref[...] = _gla_block(qf_ref[...], zf_ref[...], vf_ref[...], lb_ref[0:1, :], st_f, cum_scr, False)
    ob_ref[...] = _gla_block(qb_ref[...], zb_ref[...], vb_ref[...], lb_ref[1:2, :], st_b, cum_scr, True)
    sl_ref[0] = st_f[...]
    sl_ref[1] = st_b[...]


def _hgrn2(p, lb, s0):
    b_, L, _ = p.shape
    n = L // HG_T
    fwd = lambda c: pl.BlockSpec((None, HG_T, GROUP_W), lambda b, j: (b, j, c))
    bwd = lambda c: pl.BlockSpec((None, HG_T, GROUP_W), lambda b, j: (b, n - 1 - j, c))
    st_spec = pl.BlockSpec((None, 2, 2, 128, 128), lambda b, j: (b, 0, 0, 0, 0))
    o_shape = jax.ShapeDtypeStruct((b_, L, GROUP_W), F32)
    return pl.pallas_call(
        _hgrn2_kernel,
        out_shape=(o_shape, o_shape, jax.ShapeDtypeStruct((b_, 2, 2, 128, 128), F32)),
        grid=(b_, n),
        in_specs=[fwd(0), fwd(1), fwd(3), bwd(0), bwd(2), bwd(3),
                  pl.BlockSpec((2, GROUP_W), lambda b, j: (0, 0)), st_spec],
        out_specs=(fwd(0), bwd(0), st_spec),
        scratch_shapes=[pltpu.VMEM((2, 128, 128), F32), pltpu.VMEM((2, 128, 128), F32),
                        pltpu.VMEM((HG_T, GROUP_W), F32)],
        compiler_params=_cparams(("arbitrary", "arbitrary")),
        name="hgrn2",
    )(p, p, p, p, p, p, lb, s0)


S5_TT = 64
S5_NS = S5_GROUPS * S5_STATE
S5_LANES = 256


def _s5_kernel(uf_ref, ub_ref, wb_ref, wc_ref, a_ref, x0_ref, yf_ref, yb_ref, xl_ref,
               buf_f, buf_b, st):
    j = pl.program_id(0)

    @pl.when(j == 0)
    def _():
        st[...] = x0_ref[...]

    halves = lambda r: jnp.concatenate([r[0], r[1]], axis=1).astype(BF16)
    buf_f[...] = jnp.dot(halves(uf_ref), wb_ref[0], preferred_element_type=F32)
    buf_b[...] = jnp.dot(halves(ub_ref), wb_ref[1], preferred_element_type=F32)

    for cc in range(S5_NS // S5_LANES):
        re = slice(cc * S5_LANES, (cc + 1) * S5_LANES)
        im = slice(S5_NS + cc * S5_LANES, S5_NS + (cc + 1) * S5_LANES)
        arf, aif, arb, aib = a_ref[0, :, re], a_ref[1, :, re], a_ref[2, :, re], a_ref[3, :, re]

        def step(t, carry, re=re, im=im, arf=arf, aif=aif, arb=arb, aib=aib):
            xrf, xif, xrb, xib = carry
            rf = pl.multiple_of(t * 8, 8)
            rb = pl.multiple_of((S5_TT - 1 - t) * 8, 8)
            nrf = arf * xrf - aif * xif + buf_f[pl.ds(rf, 8), re]
            nif = arf * xif + aif * xrf + buf_f[pl.ds(rf, 8), im]
            nrb = arb * xrb - aib * xib + buf_b[pl.ds(rb, 8), re]
            nib = arb * xib + aib * xrb + buf_b[pl.ds(rb, 8), im]
            buf_f[pl.ds(rf, 8), re] = nrf
            buf_f[pl.ds(rf, 8), im] = nif
            buf_b[pl.ds(rb, 8), re] = nrb
            buf_b[pl.ds(rb, 8), im] = nib
            return nrf, nif, nrb, nib

        fin = lax.fori_loop(0, S5_TT, step,
                            (st[0, :, re], st[1, :, re], st[2, :, re], st[3, :, re]), unroll=4)
        for k in range(4):
            st[k, :, re] = fin[k]

    for y_ref, buf, d in ((yf_ref, buf_f, 0), (yb_ref, buf_b, 1)):
        y = jnp.dot(buf[...].astype(BF16), wc_ref[d], preferred_element_type=F32)
        y_ref[0] = y[:, :128]
        y_ref[1] = y[:, 128:]
    xl_ref[...] = st[...]


def _s5_scan(u2, wb, wc, a_bc, x0):
    L = u2.shape[1] // 8
    rows = S5_TT * 8
    n = L // S5_TT
    y_shape = jax.ShapeDtypeStruct((2, L * 8, 128), F32)
    return pl.pallas_call(
        _s5_kernel,
        out_shape=(y_shape, y_shape, jax.ShapeDtypeStruct((4, 8, S5_NS), F32)),
        grid=(n,),
        in_specs=[
            pl.BlockSpec((2, rows, 128), lambda j: (0, j, 0)),
            pl.BlockSpec((2, rows, 128), lambda j: (0, n - 1 - j, 0)),
            pl.BlockSpec((2, GROUP_W, 2 * S5_NS), lambda j: (0, 0, 0)),
            pl.BlockSpec((2, 2 * S5_NS, GROUP_W), lambda j: (0, 0, 0)),
            pl.BlockSpec((4, 8, S5_NS), lambda j: (0, 0, 0)),
            pl.BlockSpec((4, 8, S5_NS), lambda j: (0, 0, 0)),
        ],
        out_specs=(pl.BlockSpec((2, rows, 128), lambda j: (0, j, 0)),
                   pl.BlockSpec((2, rows, 128), lambda j: (0, n - 1 - j, 0)),
                   pl.BlockSpec((4, 8, S5_NS), lambda j: (0, 0, 0))),
        scratch_shapes=[pltpu.VMEM((rows, 2 * S5_NS), F32),
                        pltpu.VMEM((rows, 2 * S5_NS), F32),
                        pltpu.VMEM((4, 8, S5_NS), F32)],
        compiler_params=_cparams(("arbitrary",)),
        name="s5_scan",
    )(u2, u2, wb, wc, a_bc, x0)


def _s5_weights(a_re, a_im, log_step, b_re, b_im, c_re, c_im):
    eye = jnp.eye(S5_GROUPS, dtype=F32)
    wbs, wcs, abc = [], [], []
    for d in range(2):
        ab_re, ab_im, bb_re, bb_im = _s5_discretize(a_re[d], a_im[d], log_step[d], b_re, b_im)
        wb_re = jnp.einsum('gph,gk->ghkp', bb_re, eye).reshape(GROUP_W, S5_NS)
        wb_im = jnp.einsum('gph,gk->ghkp', bb_im, eye).reshape(GROUP_W, S5_NS)
        wbs.append(jnp.concatenate([wb_re, wb_im], axis=1))
        wc_re = jnp.einsum('ghp,gk->gpkh', c_re[d], eye).reshape(S5_NS, GROUP_W)
        wc_im = jnp.einsum('ghp,gk->gpkh', c_im[d], eye).reshape(S5_NS, GROUP_W)
        wcs.append(jnp.concatenate([wc_re, -wc_im], axis=0))
        abc += [jnp.broadcast_to(ab_re.reshape(1, S5_NS), (8, S5_NS)),
                jnp.broadcast_to(ab_im.reshape(1, S5_NS), (8, S5_NS))]
    return jnp.stack(wbs).astype(BF16), jnp.stack(wcs).astype(BF16), jnp.stack(abc)


def _group_norm(m, w):
    return m * lax.rsqrt(jnp.mean(m * m, axis=-1, keepdims=True) + EPS) * w


def _merge_kernel(hgf_ref, hgb_ref, yf_ref, yb_ref, u_ref, sd_ref, glu_ref, hy_ref, at_ref, gate_ref,
                  x_ref, mod_ref, mw_ref, fw_ref, wo_ref, xo_ref, ho_ref):
    gate = gate_ref[...]
    rows = pl.ds(pl.program_id(1), gate.shape[0], stride=8)
    pick = lambda r: jnp.concatenate([r.at[0][rows, :], r.at[1][rows, :]], axis=1)
    z = jax.nn.gelu(pick(yf_ref) + pick(yb_ref) + sd_ref[...] * pick(u_ref))
    s5 = z * jax.nn.sigmoid(jnp.dot(z.astype(BF16), glu_ref[...], preferred_element_type=F32))
    parts = [
        _group_norm(hgf_ref[...] + hgb_ref[...], mw_ref[:, 0:256]) * (gate * jax.nn.sigmoid(gate)),
        _group_norm(s5, mw_ref[:, 256:512]),
        _group_norm(jnp.concatenate([hy_ref[s].T for s in range(hy_ref.shape[0])], axis=0),
                    mw_ref[:, 512:768]),
        _group_norm(at_ref[...], mw_ref[:, 768:1024]),
    ]
    mix = jnp.concatenate(parts, axis=1).astype(BF16)
    xn = x_ref[...] + mod_ref[2:3, :] * jnp.dot(mix, wo_ref[...], preferred_element_type=F32)
    xo_ref[...] = xn
    h = xn * lax.rsqrt(jnp.mean(xn * xn, axis=-1, keepdims=True) + EPS) * fw_ref[...]
    ho_ref[...] = (h * mod_ref[3:4, :] + mod_ref[4:5, :]).astype(BF16)


def _merge(hg_f, hg_b, s5_yf, s5_yb, s5_u, s5_d, glu_bf16, hy, at, p, x, mods, mw, fw, wo_bf16, tm):
    b_, L, _ = x.shape
    grp = lambda: pl.BlockSpec((None, tm, GROUP_W), lambda i, b: (b, i, 0))
    allb = lambda: pl.BlockSpec((2, tm * b_, 128), lambda i, b: (0, i, 0))
    return pl.pallas_call(
        _merge_kernel,
        out_shape=(jax.ShapeDtypeStruct((b_, L, D_MODEL), F32),
                   jax.ShapeDtypeStruct((b_, L, D_MODEL), BF16)),
        grid=(L // tm, b_),
        in_specs=[
            grp(), grp(), allb(), allb(), allb(),
            pl.BlockSpec((1, GROUP_W), lambda i, b: (0, 0)),
            pl.BlockSpec((GROUP_W, GROUP_W), lambda i, b: (0, 0)),
            pl.BlockSpec((None, tm // 128, HY_CH, 128), lambda i, b: (b, i, 0, 0)),
            grp(),
            pl.BlockSpec((None, tm, GROUP_W), lambda i, b: (b, i, 4)),
            pl.BlockSpec((None, tm, D_MODEL), lambda i, b: (b, i, 0)),
            pl.BlockSpec((None, 8, D_MODEL), lambda i, b: (b, 0, 0)),
            pl.BlockSpec((1, D_MODEL), lambda i, b: (0, 0)),
            pl.BlockSpec((1, D_MODEL), lambda i, b: (0, 0)),
            pl.BlockSpec((D_MODEL, D_MODEL), lambda i, b: (0, 0)),
        ],
        out_specs=(pl.BlockSpec((None, tm, D_MODEL), lambda i, b: (b, i, 0)),
                   pl.BlockSpec((None, tm, D_MODEL), lambda i, b: (b, i, 0))),
        compiler_params=_cparams(("arbitrary", "arbitrary")),
        name="merge_out_proj",
    )(hg_f, hg_b, s5_yf, s5_yb, s5_u, s5_d.reshape(1, GROUP_W), glu_bf16, hy, at, p, x, mods,
      mw.reshape(1, D_MODEL), fw.reshape(1, D_MODEL), wo_bf16)


def _ffn_kernel(hp_ref, h_ref, hn_ref, x_ref, mod_ref, wa_ref, wv_ref, cw_ref, wd_ref, fw_ref, o_ref,
                acc_scr, *, tm, final_norm):
    i = pl.program_id(1)
    last = pl.num_programs(1) - 1
    h = h_ref[...]
    hp = jnp.where(i > 0, hp_ref[...], jnp.zeros_like(hp_ref))
    hn = jnp.where(i < last, hn_ref[...], jnp.zeros_like(hn_ref))
    h_ext = jnp.concatenate([hp, h, hn], axis=0)
    rows = tm + 2 * HALO

    def gated(j):
        a = jnp.dot(h_ext, wa_ref[j], preferred_element_type=F32)
        v = jnp.dot(h, wv_ref[j], preferred_element_type=F32)
        cw = cw_ref[j]
        conv = (pltpu.roll(a, 1, 0)[HALO:HALO + tm] * cw[0:1, :]
                + a[HALO:HALO + tm] * cw[1:2, :]
                + pltpu.roll(a, rows - 1, 0)[HALO:HALO + tm] * cw[2:3, :] + cw[3:4, :])
        return (conv * jax.nn.sigmoid(conv) * v).astype(BF16)

    for j in range(0, N_FF_CHUNKS, 2):
        n = min(2, N_FF_CHUNKS - j)
        g = jnp.concatenate([gated(j + d) for d in range(n)], axis=1)
        wd = wd_ref[j:j + n].reshape(n * FF_CHUNK, D_MODEL)
        part = jnp.dot(g, wd, preferred_element_type=F32)
        if j == 0:
            acc_scr[...] = part
        elif j + n < N_FF_CHUNKS:
            acc_scr[...] += part
        else:
            xo = x_ref[...] + mod_ref[5:6, :] * (acc_scr[...] + part)
            if final_norm:
                xo = xo * lax.rsqrt(jnp.mean(xo * xo, axis=-1, keepdims=True) + EPS) * fw_ref[...]
            o_ref[...] = xo


def _ffn(h, x, mods, wa, wv, cw, wd, tm, final_w=None):
    b_, L, _ = x.shape
    nh = L // HALO
    r = tm // HALO
    const3 = lambda b, i: (0, 0, 0)
    fw = jnp.ones((1, D_MODEL), F32) if final_w is None else final_w.reshape(1, D_MODEL)
    return pl.pallas_call(
        functools.partial(_ffn_kernel, tm=tm, final_norm=final_w is not None),
        out_shape=jax.ShapeDtypeStruct((b_, L, D_MODEL), F32),
        grid=(b_, L // tm),
        in_specs=[
            pl.BlockSpec((None, HALO, D_MODEL), lambda b, i: (b, jnp.maximum(i * r - 1, 0), 0)),
            pl.BlockSpec((None, tm, D_MODEL), lambda b, i: (b, i, 0)),
            pl.BlockSpec((None, HALO, D_MODEL), lambda b, i: (b, jnp.minimum((i + 1) * r, nh - 1), 0)),
            pl.BlockSpec((None, tm, D_MODEL), lambda b, i: (b, i, 0)),
            pl.BlockSpec((None, 8, D_MODEL), lambda b, i: (b, 0, 0)),
            pl.BlockSpec((N_FF_CHUNKS, D_MODEL, FF_CHUNK), const3, pipeline_mode=pl.Buffered(1)),
            pl.BlockSpec((N_FF_CHUNKS, D_MODEL, FF_CHUNK), const3, pipeline_mode=pl.Buffered(1)),
            pl.BlockSpec((N_FF_CHUNKS, 8, FF_CHUNK), const3),
            pl.BlockSpec((N_FF_CHUNKS, FF_CHUNK, D_MODEL), const3, pipeline_mode=pl.Buffered(1)),
            pl.BlockSpec((1, D_MODEL), lambda b, i: (0, 0)),
        ],
        out_specs=pl.BlockSpec((None, tm, D_MODEL), lambda b, i: (b, i, 0)),
        scratch_shapes=[pltpu.VMEM((tm, D_MODEL), F32)],
        compiler_params=_cparams(("arbitrary", "arbitrary")),
        name="conv_ffn",
    )(h, h, h, x, mods, wa, wv, cw, wd, fw)


def _s5_discretize(a_re, a_im, log_step, b_re, b_im):
    dt = jnp.exp(log_step)[:, None]
    mag = jnp.exp(a_re * dt)
    ang = a_im * dt
    ab_re, ab_im = mag * jnp.cos(ang), mag * jnp.sin(ang)
    den = a_re * a_re + a_im * a_im
    nr, ni = ab_re - 1.0, ab_im
    fr = (nr * a_re + ni * a_im) / den
    fi = (ni * a_re - nr * a_im) / den
    bb_re = fr[..., None] * b_re - fi[..., None] * b_im
    bb_im = fr[..., None] * b_im + fi[..., None] * b_re
    return ab_re, ab_im, bb_re, bb_im


HY_LANES = 128
HY_CB_LATENT = 16
HY_CB_CONTEXT = 64
HY_EMB_PAD = 128


def _vadd(a, b):
    return b if a is None else a if b is None else a + b


def _vsub(a, b):
    return (None if b is None else -b) if a is None else a if b is None else a - b


def _vscale(a, c):
    return None if (a is None or c == 0.0) else a if c == 1.0 else -a if c == -1.0 else a * c


def _cmul_const(x, wr, wi):
    re, im = x
    return (_vsub(_vscale(re, wr), _vscale(im, wi)), _vadd(_vscale(im, wr), _vscale(re, wi)))


def _unit_root(k, n, sign):
    k %= n
    if (4 * k) % n == 0:
        return ((1.0, 0.0), (0.0, float(sign)), (-1.0, 0.0), (0.0, float(-sign)))[4 * k // n]
    ang = sign * 2.0 * math.pi * k / n
    return math.cos(ang), math.sin(ang)


def _fft_slabs(x, sign, n_out=None):
    n = len(x)
    if n == 1:
        return list(x)
    n_out = n if n_out is None else n_out
    ev = _fft_slabs(x[0::2], sign)
    od = _fft_slabs(x[1::2], sign)
    out = [None] * n_out
    for k in range(n // 2):
        t = _cmul_const(od[k], *_unit_root(k, n, sign))
        if k < n_out:
            out[k] = (_vadd(ev[k][0], t[0]), _vadd(ev[k][1], t[1]))
        if k + n // 2 < n_out:
            out[k + n // 2] = (_vsub(ev[k][0], t[0]), _vsub(ev[k][1], t[1]))
    return out


def _twiddle(slabs, tw_ref, conj):
    out = [slabs[0]]
    for k in range(1, len(slabs)):
        re, im = slabs[k]
        tr = tw_ref[0, k]
        ti = -tw_ref[1, k] if conj else tw_ref[1, k]
        if im is None:
            out.append((re * tr, re * ti))
        else:
            out.append((re * tr - im * ti, re * ti + im * tr))
    return out


def _dot_split(x, w_ref):
    hi = x.astype(BF16)
    lo = (x - hi.astype(F32)).astype(BF16)
    return (jnp.dot(hi, w_ref[0], preferred_element_type=F32)
            + jnp.dot(lo, w_ref[0], preferred_element_type=F32)
            + jnp.dot(hi, w_ref[1], preferred_element_type=F32))


def _lane_dft(slabs, w_ref, cb):
    zero = jnp.zeros((cb, HY_LANES), F32)
    rows = jnp.concatenate(
        [jnp.concatenate([zero if re is None else re, zero if im is None else im], axis=1)
         for re, im in slabs], axis=0)
    out = _dot_split(rows, w_ref)
    return [(out[cb * i:cb * (i + 1), :HY_LANES], out[cb * i:cb * (i + 1), HY_LANES:])
            for i in range(len(slabs))]


def _short_conv(ref, part, cw_ref, ch, lane):
    n = ref.shape[1]
    w0, w1, w2, bb = cw_ref[ch, 0], cw_ref[ch, 1], cw_ref[ch, 2], cw_ref[ch, 3]
    right = [pltpu.roll(ref[part, s], 1, 1) for s in range(n)]
    left = [pltpu.roll(ref[part, s], HY_LANES - 1, 1) for s in range(n)]
    zero = jnp.zeros(lane.shape, F32)
    out = []
    for s in range(n):
        xm = jnp.where(lane == 0, right[s - 1] if s > 0 else zero, right[s])
        xp = jnp.where(lane == HY_LANES - 1, left[s + 1] if s < n - 1 else zero, left[s])
        out.append(xm * w0 + ref[part, s] * w1 + xp * w2 + bb)
    return out


def _hy_conv_kernel(x1_ref, x2_ref, z_ref, cw_ref, bias_ref, kf_ref, tw_ref, wf_ref, wi_ref, o_ref,
                    *, n1):
    nh = n1 // 2
    cb = z_ref.shape[2]
    lane = lax.broadcasted_iota(jnp.int32, (cb, HY_LANES), 1)
    gates = [[_short_conv(r, part, cw_ref, ch, lane) for part in range(2)]
             for ch, r in ((0, x1_ref), (1, x2_ref))]
    z = [_short_conv(z_ref, part, cw_ref, 2, lane) for part in range(2)]
    for o in range(HY_ORDER):
        spec = _fft_slabs([(z[0][s], z[1][s]) for s in range(nh)] + [(None, None)] * nh, -1)
        spec = _lane_dft(_twiddle(spec, tw_ref, False), wf_ref, cb)
        spec = [(re * kf_ref[o, 0, k] - im * kf_ref[o, 1, k], re * kf_ref[o, 1, k] + im * kf_ref[o, 0, k])
                for k, (re, im) in enumerate(spec)]
        spec = _twiddle(_lane_dft(spec, wi_ref, cb), tw_ref, True)
        y = _fft_slabs(spec, 1, n_out=nh)
        bo = bias_ref[o]
        z = [[gates[o][part][s] * (y[s][part] + bo * z[part][s]) for s in range(nh)]
             for part in range(2)]
    for part in range(2):
        for s in range(nh):
            o_ref[part, s] = z[part][s]


def _hy_conv(zh, cw, bias_b, kf, tw, wf, wi):
    b_, nh = zh.shape[0], zh.shape[1]
    cb = tw.shape[2]
    half, ncb = b_ // 2, HY_CH // cb
    n1 = 2 * nh
    zspec = lambda ch: pl.BlockSpec((2, None, nh, cb, HY_LANES),
                                    lambda c, b: (0, b, 0, ch * ncb + c, 0))
    const3 = lambda c, b: (0, 0, 0)
    z5 = zh.reshape(2, half, nh, 3 * HY_CH, HY_LANES)
    out = pl.pallas_call(
        functools.partial(_hy_conv_kernel, n1=n1),
        out_shape=jax.ShapeDtypeStruct((2, half, nh, HY_CH, HY_LANES), F32),
        grid=(ncb, half),
        in_specs=[
            zspec(0), zspec(1), zspec(2),
            pl.BlockSpec((3, 4, cb, HY_LANES), lambda c, b: (0, 0, c, 0)),
            pl.BlockSpec((HY_ORDER, cb, HY_LANES), lambda c, b: (0, c, 0)),
            pl.BlockSpec((HY_ORDER, 2, None, n1, cb, HY_LANES), lambda c, b: (0, 0, c, 0, 0, 0)),
            pl.BlockSpec((2, n1, cb, HY_LANES), lambda c, b: (0, 0, 0, 0)),
            pl.BlockSpec((2, 256, 256), const3),
            pl.BlockSpec((2, 256, 256), const3),
        ],
        out_specs=pl.BlockSpec((2, None, nh, cb, HY_LANES), lambda c, b: (0, b, 0, c, 0)),
        compiler_params=_cparams(("arbitrary", "arbitrary")),
        name="hyena_conv",
    )(z5, z5, z5, cw, bias_b, kf, tw, wf, wi)
    return out.reshape(b_, nh, HY_CH, HY_LANES)


def _hy_spectrum_kernel(k_ref, tw_ref, wf_ref, o_ref, *, n1):
    spec = _fft_slabs([(k_ref[s], None) for s in range(n1)], -1)
    spec = _lane_dft(_twiddle(spec, tw_ref, False), wf_ref, k_ref.shape[1])
    scale = 1.0 / (n1 * HY_LANES)
    for k in range(n1):
        o_ref[0, k] = spec[k][0] * scale
        o_ref[1, k] = spec[k][1] * scale


def _hy_spectrum(taps, tw, wf):
    n1, cb = taps.shape[1], tw.shape[2]
    return pl.pallas_call(
        functools.partial(_hy_spectrum_kernel, n1=n1),
        out_shape=jax.ShapeDtypeStruct((HY_ORDER, 2, HY_CH // cb, n1, cb, HY_LANES), F32),
        grid=(HY_ORDER, HY_CH // cb),
        in_specs=[
            pl.BlockSpec((None, n1, cb, HY_LANES), lambda o, c: (o, 0, c, 0)),
            pl.BlockSpec((2, n1, cb, HY_LANES), lambda o, c: (0, 0, 0, 0)),
            pl.BlockSpec((2, 256, 256), lambda o, c: (0, 0, 0)),
        ],
        out_specs=pl.BlockSpec((None, 2, None, n1, cb, HY_LANES), lambda o, c: (o, 0, c, 0, 0, 0)),
        compiler_params=_cparams(("arbitrary", "arbitrary")),
        name="hyena_spectrum",
    )(taps, tw, wf)


def _dot_f32(a, b, dims=(((1,), (0,)), ((), ()))):
    a_hi = a.astype(BF16)
    a_lo = (a - a_hi.astype(F32)).astype(BF16)
    b_hi = b.astype(BF16)
    b_lo = (b - b_hi.astype(F32)).astype(BF16)
    dot = lambda p, q: lax.dot_general(p, q, dims, preferred_element_type=F32)
    return dot(a_hi, b_hi) + dot(a_lo, b_hi) + dot(a_hi, b_lo)


def _hy_mlp_kernel(emb_ref, w1_ref, w2_ref, w3t_ref, vec_ref, pos_ref, dl_ref, o_ref):
    z = emb_ref[...]
    h = jnp.sin(vec_ref[2:3, :] * (_dot_f32(z, w1_ref[...]) + vec_ref[0:1, :]))
    h = jnp.sin(vec_ref[3:4, :] * (_dot_f32(h, w2_ref[...]) + vec_ref[1:2, :]))
    ht = _dot_f32(w3t_ref[...], h, _NT)
    dl = dl_ref[...]
    for s in range(o_ref.shape[1]):
        lanes = slice(HY_LANES * s, HY_LANES * (s + 1))
        win = jnp.exp(-pos_ref[0:1, lanes] * dl)
        wf, wb = pos_ref[1:2, lanes] * win, pos_ref[2:3, lanes] * win
        for o in range(HY_ORDER):
            base = 2 * HY_CH * o
            o_ref[o, s] = (ht[base:base + HY_CH, lanes] * wf
                           + ht[base + HY_CH:base + 2 * HY_CH, lanes] * wb)


def _hy_filter_taps(L, w1, b1, freq, w2, b2, w3):
    pad = HY_EMB_PAD
    n = 2 * L
    t01 = np.linspace(0.0, 1.0, L, dtype=np.float32)[:, None]
    w = (np.float32(2.0 * math.pi) * np.arange(L, dtype=np.float32)[:, None]) / np.float32(L)
    bands = (HY_EMB - 1) // 2
    fr = np.linspace(1e-4, bands - 1, bands, dtype=np.float32)[None, :]
    arg = (fr * w).astype(np.float64)
    emb = np.zeros((L, pad), np.float32)
    emb[:, 0:1] = t01
    emb[:, 1:1 + bands] = np.cos(arg)
    emb[:, 1 + bands:HY_EMB] = -np.sin(arg)
    pos_idx = np.arange(n)
    lag = np.where(pos_idx < L, pos_idx, (n - pos_idx) % L)
    pos = np.zeros((8, n), np.float32)
    pos[0] = t01[lag, 0]
    pos[1] = pos_idx < L
    pos[2] = (pos_idx > L) | (pos_idx == 0)
    hid = w1.shape[1]
    n_out = w3.shape[1]
    w1p = jnp.zeros((pad, pad), F32).at[:HY_EMB, :hid].set(w1)
    w2p = jnp.zeros((pad, pad), F32).at[:hid, :hid].set(w2)
    w3t = jnp.zeros((n_out, pad), F32).at[:, :hid].set(w3.T)
    vec = jnp.zeros((8, pad), F32).at[0, :hid].set(b1).at[1, :hid].set(b2)
    vec = vec.at[2, :hid].set(freq[0]).at[3, :hid].set(freq[1])
    deltas = np.abs(np.linspace(HY_MIN_DECAY, HY_MAX_DECAY, HY_CH, dtype=np.float32))
    dl = np.broadcast_to(deltas[:, None], (HY_CH, HY_LANES))
    tl = min(n, 512)
    return pl.pallas_call(
        _hy_mlp_kernel,
        out_shape=jax.ShapeDtypeStruct((HY_ORDER, n // HY_LANES, HY_CH, HY_LANES), F32),
        grid=(n // tl,),
        in_specs=[
            pl.BlockSpec((tl, pad), lambda i: (i, 0)),
            pl.BlockSpec((pad, pad), lambda i: (0, 0)),
            pl.BlockSpec((pad, pad), lambda i: (0, 0)),
            pl.BlockSpec((n_out, pad), lambda i: (0, 0)),
            pl.BlockSpec((8, pad), lambda i: (0, 0)),
            pl.BlockSpec((8, tl), lambda i: (0, i)),
            pl.BlockSpec((HY_CH, HY_LANES), lambda i: (0, 0)),
        ],
        out_specs=pl.BlockSpec((HY_ORDER, tl // HY_LANES, HY_CH, HY_LANES), lambda i: (0, i, 0, 0)),
        compiler_params=_cparams(("arbitrary",)),
        name="hyena_filter_mlp",
    )(jnp.asarray(emb[lag]), w1p, w2p, w3t, vec, jnp.asarray(pos), jnp.asarray(dl))


def _hy_constants(n1, cb):
    n = n1 * HY_LANES
    ang = -2.0 * np.pi * np.outer(np.arange(n1), np.arange(HY_LANES)) / n
    tw = np.stack([np.cos(ang), np.sin(ang)])[:, :, None, :].repeat(cb, axis=2).astype(np.float32)
    a2 = -2.0 * np.pi * np.outer(np.arange(HY_LANES), np.arange(HY_LANES)) / HY_LANES
    fr, fi = np.cos(a2), np.sin(a2)
    fwd = np.block([[fr, fi], [-fi, fr]])
    inv = np.block([[fr, -fi], [fi, fr]])

    def split(m):
        hi = m.astype(BF16)
        lo = (m - hi.astype(np.float64)).astype(BF16)
        return jnp.asarray(np.stack([hi, lo]))

    return jnp.asarray(tw), split(fwd), split(inv)


def _regroup_in_proj(w):
    d = w.shape[0]
    half = ATT_HD // 2
    q = w[:, W_COL_TQ:W_COL_TK].reshape(d, ATT_KV, 2, half, 2)
    q = q.transpose(0, 2, 1, 4, 3).reshape(d, ATT_HEADS * ATT_HD)
    k = w[:, W_COL_TK:W_COL_TV].reshape(d, ATT_KV, half, 2).transpose(0, 1, 3, 2).reshape(d, ATT_KV * ATT_HD)
    return jnp.concatenate([w[:, :W_COL_S5], q, k, w[:, W_COL_TV:], w[:, W_COL_S5:W_COL_HY]], axis=1)


def _regroup_att_rows(m):
    at = m[3 * GROUP_W:].reshape((2, 2, ATT_HD) + m.shape[1:])
    at = jnp.swapaxes(at, 0, 1).reshape((GROUP_W,) + m.shape[1:])
    return jnp.concatenate([m[:3 * GROUP_W], at], axis=0)


def _rope_tables(L):
    rows = L // GRID_W
    row = jnp.repeat(jnp.arange(rows), GRID_W).astype(F32)
    col = jnp.tile(jnp.arange(GRID_W), rows).astype(F32)
    axis_dim = ATT_HD // 2
    inv = 1.0 / (ROPE_BASE ** (jnp.arange(0, axis_dim, 2, dtype=F32) / axis_dim))
    ang = jnp.concatenate([row[:, None] * inv, col[:, None] * inv], axis=-1)
    c, s = jnp.cos(ang), jnp.sin(ang)
    return jnp.tile(jnp.concatenate([c, c], axis=1), (1, 2)), jnp.tile(jnp.concatenate([-s, s], axis=1), (1, 2))


def kernel(x, c, ctx, c_ctx, ada_w, ada_b, norm_mix_w, norm_ffn_w, w_in, hg_lb_logits, s5_a_re, s5_a_im, s5_log_step, s5_b_re, s5_b_im, s5_c_re, s5_c_im, s5_d, s5_glu_w, hy_conv_w, hy_conv_b, hy_w1, hy_b1, hy_freq, hy_w2, hy_b2, hy_w3, hy_bias, att_sink, merge_norm_w, w_out, ffn_w_up, ffn_conv_w, ffn_conv_b, ffn_w_down, final_norm_w):
    bsz, seq_len, _ = x.shape
    ctx_len = ctx.shape[1]

    cond = jnp.zeros((16, D_MODEL), F32).at[:bsz].set(c).at[bsz].set(c_ctx)
    ada = _ada_all(cond, ada_w, ada_b)

    def mods_of(m):
        sh1, sc1, g1, sh2, sc2, g2 = jnp.split(m, 6, axis=-1)
        z = jnp.zeros_like(sh1)
        return jnp.stack([1.0 + sc1, sh1, g1, 1.0 + sc2, sh2, g2, z, z], axis=1)

    sm = jax.nn.softmax(hg_lb_logits, axis=0)
    lower_bounds = jnp.cumsum(sm, axis=0) - sm[0:1]
    cos_t, sin_t = _rope_tables(seq_len)
    hg_zero = jnp.zeros((bsz, 2, 2, 128, 128), F32)
    hy_tw_lat, hy_wf, hy_wi = _hy_constants(2 * seq_len // HY_LANES, HY_CB_LATENT)
    hy_tw_ctx, _, _ = _hy_constants(2 * ctx_len // HY_LANES, HY_CB_CONTEXT)
    s5_zero = jnp.zeros((4, bsz, S5_NS), F32)

    xc = ctx
    for l in range(DEPTH):
        last = l == DEPTH - 1
        mods_lat = mods_of(ada[l, :bsz])
        mods_ctx = jnp.broadcast_to(mods_of(ada[l, bsz:bsz + 1]), (bsz, 8, D_MODEL))
        w_in_l = _regroup_in_proj(w_in[l]).astype(BF16)
        mw = _regroup_att_rows(merge_norm_w[l])
        wo = _regroup_att_rows(w_out[l]).astype(BF16)
        wa = ffn_w_up[l][:, :D_FF].reshape(D_MODEL, N_FF_CHUNKS, FF_CHUNK).transpose(1, 0, 2).astype(BF16)
        wv = ffn_w_up[l][:, D_FF:].reshape(D_MODEL, N_FF_CHUNKS, FF_CHUNK).transpose(1, 0, 2).astype(BF16)
        wd = ffn_w_down[l].reshape(N_FF_CHUNKS, FF_CHUNK, D_MODEL).astype(BF16)
        cw = jnp.concatenate([ffn_conv_w[l], ffn_conv_b[l][None], jnp.zeros((4, D_FF), F32)], axis=0)
        cw = cw.reshape(8, N_FF_CHUNKS, FF_CHUNK).transpose(1, 0, 2)

        wh = w_in[l][:, W_COL_HY:W_COL_HY + 3 * HY_CH].T.astype(BF16)
        p_lat, u_lat, zh_lat = _inproj(x, mods_lat, norm_mix_w[l], w_in_l, wh, 256)
        p_ctx, u_ctx, zh_ctx = _inproj(xc, mods_ctx, norm_mix_w[l], w_in_l, wh, ctx_len)

        hgf_ctx, hgb_ctx, hg_state = _hgrn2(p_ctx, lower_bounds[l], hg_zero)
        hgf_lat, hgb_lat, _ = _hgrn2(p_lat, lower_bounds[l], hg_state)

        s5_wb, s5_wc, s5_a = _s5_weights(s5_a_re[l], s5_a_im[l], s5_log_step[l], s5_b_re[l],
                                         s5_b_im[l], s5_c_re[l], s5_c_im[l])
        glu = s5_glu_w[l].astype(BF16)
        yf_ctx, yb_ctx, s5_state = _s5_scan(u_ctx, s5_wb, s5_wc, s5_a, s5_zero)
        yf_lat, yb_lat, _ = _s5_scan(u_lat, s5_wb, s5_wc, s5_a, s5_state)

        hy_params = (hy_w1[l], hy_b1[l], hy_freq[l], hy_w2[l], hy_b2[l], hy_w3[l])
        hy_cw = jnp.concatenate([hy_conv_w[l], hy_conv_b[l][None]], axis=0)
        hy_cw = jnp.broadcast_to(hy_cw.reshape(4, 3, HY_CH).transpose(1, 0, 2)[..., None],
                                 (3, 4, HY_CH, HY_LANES))
        hy_bb = jnp.broadcast_to(hy_bias[l][..., None], (HY_ORDER, HY_CH, HY_LANES))
        kf_lat = _hy_spectrum(_hy_filter_taps(seq_len, *hy_params), hy_tw_lat, hy_wf)
        hy_lat = _hy_conv(zh_lat, hy_cw, hy_bb, kf_lat, hy_tw_lat, hy_wf, hy_wi)

        at_lat = _att_lat(p_lat, p_ctx, cos_t, sin_t, att_sink[l])

        x_mid, h_lat = _merge(hgf_lat, hgb_lat, yf_lat, yb_lat, u_lat, s5_d[l], glu, hy_lat, at_lat, p_lat, x,
                              mods_lat, mw, norm_ffn_w[l], wo, 256)
        x_new = _ffn(h_lat, x_mid, mods_lat, wa, wv, cw, wd, 512,
                     final_w=final_norm_w if last else None)
        if not last:
            kf_ctx = _hy_spectrum(_hy_filter_taps(ctx_len, *hy_params), hy_tw_ctx, hy_wf)
            hy_ctx = _hy_conv(zh_ctx, hy_cw, hy_bb, kf_ctx, hy_tw_ctx, hy_wf, hy_wi)
            at_ctx = _att_ctx(p_ctx, att_sink[l])
            xc_mid, h_ctx = _merge(hgf_ctx, hgb_ctx, yf_ctx, yb_ctx, u_ctx, s5_d[l], glu, hy_ctx, at_ctx, p_ctx,
                                   xc, mods_ctx, mw, norm_ffn_w[l], wo, ctx_len)
            xc = _ffn(h_ctx, xc_mid, mods_ctx, wa, wv, cw, wd, ctx_len)
        x = x_new
    return x
```

```python
import functools
import math

import jax
import jax.numpy as jnp
import numpy as np
from jax import lax
from jax.experimental import pallas as pl
from jax.experimental.pallas import tpu as pltpu

F32 = jnp.float32
BF16 = jnp.bfloat16

D_MODEL = 1024
DEPTH = 4
GRID_W = 64
GROUP_W = 256
HG_DK = 64
HG_DV = 64
HG_HEADS = 4
HG_CHUNK = 16
S5_GROUP_CH = 16
S5_GROUPS = 16
S5_STATE = 64
HY_CH = 256
HY_ORDER = 2
HY_EMB = 33
HY_MAX_DECAY = math.log(1e-2) / 0.3
HY_MIN_DECAY = math.log(1e-2) / 1.5
ATT_HD = 64
ATT_HEADS = 4
ATT_KV = 2
WINDOW = 128
ATT_BLOCK = 128
ATT_QB = 2
ATT_SCALE = 1.0 / math.sqrt(ATT_HD)
ROPE_BASE = 10000.0
D_FF = 2816
EPS = 1e-6
IN_COLS = 2816
W_COL_S5 = 1280
W_COL_HY = 1536
W_COL_TQ = 2304
W_COL_TK = 2560
W_COL_TV = 2688
P_COLS = 1792
COL_TQ = 1280
COL_TK = 1536
COL_TV = 1664

FF_CHUNK = 256
N_FF_CHUNKS = D_FF // FF_CHUNK
HALO = 16
VMEM_LIMIT = 56 * 1024 * 1024


def _cparams(sem):
    return pltpu.CompilerParams(dimension_semantics=sem, vmem_limit_bytes=VMEM_LIMIT)


def _ada_kernel(c_ref, w_ref, b_ref, o_ref):
    cond = c_ref[...]
    act = cond * jax.nn.sigmoid(cond)
    o_ref[...] = jnp.dot(act.astype(BF16), w_ref[...].astype(BF16),
                         preferred_element_type=F32) + b_ref[...]


def _ada_all(cond, ada_w, ada_b):
    tn = 1024
    n6 = 6 * D_MODEL
    return pl.pallas_call(
        _ada_kernel,
        out_shape=jax.ShapeDtypeStruct((DEPTH, 16, n6), F32),
        grid=(DEPTH, n6 // tn),
        in_specs=[
            pl.BlockSpec((16, D_MODEL), lambda l, j: (0, 0)),
            pl.BlockSpec((None, D_MODEL, tn), lambda l, j: (l, 0, j)),
            pl.BlockSpec((None, 1, tn), lambda l, j: (l, 0, j)),
        ],
        out_specs=pl.BlockSpec((None, 16, tn), lambda l, j: (l, 0, j)),
        compiler_params=_cparams(("arbitrary", "arbitrary")),
        name="ada_mod",
    )(cond, ada_w, ada_b.reshape(DEPTH, 1, n6))


def _inproj_kernel(x_ref, mod_ref, nw_ref, w_ref, wh_ref, o_ref, u_ref, zh_ref):
    x = x_ref[...]
    ms = jnp.mean(x * x, axis=-1, keepdims=True)
    y = x * lax.rsqrt(ms + EPS) * nw_ref[...]
    y = (y * mod_ref[0:1, :] + mod_ref[1:2, :]).astype(BF16)
    p = jnp.dot(y, w_ref[...], preferred_element_type=F32)
    o_ref[...] = p
    for c in range(2):
        u_ref.at[c][pl.ds(pl.program_id(1), x.shape[0], stride=8), :] = (
            p[:, P_COLS + 128 * c:P_COLS + 128 * (c + 1)])
    zt = lax.dot_general(wh_ref[...], y, _NT, preferred_element_type=F32)
    for s in range(zh_ref.shape[0]):
        zh_ref[s] = zt[:, 128 * s:128 * (s + 1)]


def _inproj(x, mods, nw, w_bf16, wh_bf16, tm):
    b_, L, _ = x.shape
    return pl.pallas_call(
        _inproj_kernel,
        out_shape=(jax.ShapeDtypeStruct((b_, L, P_COLS + GROUP_W), F32),
                   jax.ShapeDtypeStruct((2, L * b_, 128), F32),
                   jax.ShapeDtypeStruct((b_, L // 128, 3 * HY_CH, 128), F32)),
        grid=(L // tm, b_),
        in_specs=[
            pl.BlockSpec((None, tm, D_MODEL), lambda i, b: (b, i, 0)),
            pl.BlockSpec((None, 8, D_MODEL), lambda i, b: (b, 0, 0)),
            pl.BlockSpec((1, D_MODEL), lambda i, b: (0, 0)),
            pl.BlockSpec((D_MODEL, P_COLS + GROUP_W), lambda i, b: (0, 0)),
            pl.BlockSpec((3 * HY_CH, D_MODEL), lambda i, b: (0, 0)),
        ],
        out_specs=(pl.BlockSpec((None, tm, P_COLS + GROUP_W), lambda i, b: (b, i, 0)),
                   pl.BlockSpec((2, tm * b_, 128), lambda i, b: (0, i, 0)),
                   pl.BlockSpec((None, tm // 128, 3 * HY_CH, 128), lambda i, b: (b, i, 0, 0))),
        compiler_params=_cparams(("arbitrary", "arbitrary")),
        name="in_proj",
    )(x, mods, nw.reshape(1, D_MODEL), w_bf16, wh_bf16)


def _half_swap(t):
    w = t.shape[-1]
    lane = lax.broadcasted_iota(jnp.int32, t.shape, t.ndim - 1)
    return jnp.where((lane % ATT_HD) < ATT_HD // 2,
                     pltpu.roll(t, w - ATT_HD // 2, t.ndim - 1),
                     pltpu.roll(t, ATT_HD // 2, t.ndim - 1))


def _rope(t, cos, sin):
    return t * cos + _half_swap(t) * sin


def _attend(q, keys, vals, valid, sink_ref):
    lane = lax.broadcasted_iota(jnp.int32, (1, 128), 1)
    lo = lane < ATT_HD
    kb = keys.astype(BF16)
    v_lo = jnp.where(lo, vals, 0.0).astype(BF16)
    v_hi = jnp.where(lo, 0.0, vals).astype(BF16)
    v_cat = jnp.concatenate([v_lo, v_hi], axis=0)
    outs = []
    for m in range(2):
        qm = q[:, 128 * m:128 * (m + 1)]
        probs = []
        for half in range(2):
            head = (0, 2, 1, 3)[2 * m + half]
            qh = jnp.where(lo if half == 0 else jnp.logical_not(lo), qm, 0.0).astype(BF16)
            s = lax.dot_general(qh, kb, (((1,), (1,)), ((), ())), preferred_element_type=F32)
            if valid is not None:
                s = jnp.where(valid, s, -jnp.inf)
            sink = sink_ref[head]
            mx = jnp.maximum(jnp.max(s, axis=-1, keepdims=True), sink)
            e = jnp.exp(s - mx)
            den = jnp.sum(e, axis=-1, keepdims=True) + jnp.exp(sink - mx)
            probs.append((e / den).astype(BF16))
        p_cat = jnp.concatenate(probs, axis=1)
        outs.append(jnp.dot(p_cat, v_cat, preferred_element_type=F32))
    return jnp.concatenate(outs, axis=1)


def _att_lat_kernel(sink_ref, q_ref, kp_ref, kc_ref, kn_ref, vp_ref, vc_ref, vn_ref,
                    kx_ref, vx_ref, cp_ref, cc_ref, cn_ref, sp_ref, sc_ref, sn_ref, o_ref,
                    *, seq_len):
    n = pl.program_id(1) * ATT_QB
    cos_c, sin_c = cc_ref[...], sc_ref[...]
    q = _rope(q_ref[...], jnp.concatenate([cos_c, cos_c], axis=1),
              jnp.concatenate([sin_c, sin_c], axis=1)) * ATT_SCALE
    keys = jnp.concatenate([
        kx_ref[...],
        _rope(kp_ref[...], cp_ref[...], sp_ref[...]),
        _rope(kc_ref[...], cos_c, sin_c),
        _rope(kn_ref[...], cn_ref[...], sn_ref[...])], axis=0)
    vals = jnp.concatenate([vx_ref[...], vp_ref[...], vc_ref[...], vn_ref[...]], axis=0)
    lc = kx_ref.shape[0]
    s_tot = lc + (ATT_QB + 2) * ATT_BLOCK
    col = lax.broadcasted_iota(jnp.int32, (ATT_QB * ATT_BLOCK, s_tot), 1)
    row = lax.broadcasted_iota(jnp.int32, (ATT_QB * ATT_BLOCK, s_tot), 0)
    qpos = n * ATT_BLOCK + row
    kpos = (n - 1) * ATT_BLOCK + (col - lc)
    valid = (col < lc) | ((jnp.abs(kpos - qpos) <= WINDOW) & (kpos >= 0) & (kpos < seq_len))
    o_ref[...] = _attend(q, keys, vals, valid, sink_ref)


def _att_lat(p_lat, p_ctx, cos_t, sin_t, sink):
    b_, L, _ = p_lat.shape
    lc = p_ctx.shape[1]
    nb = L // ATT_BLOCK
    kcol, vcol = COL_TK // 128, COL_TV // 128
    qrows = ATT_QB * ATT_BLOCK
    prev = lambda b, n: jnp.maximum(n * ATT_QB - 1, 0)
    nxt = lambda b, n: jnp.minimum((n + 1) * ATT_QB, nb - 1)
    edge = lambda c, f: pl.BlockSpec((None, ATT_BLOCK, 128), lambda b, n: (b, f(b, n), c))
    cur = lambda c: pl.BlockSpec((None, qrows, 128), lambda b, n: (b, n, c))
    tab_edge = lambda f: pl.BlockSpec((ATT_BLOCK, 128), lambda b, n: (f(b, n), 0))
    tab_cur = lambda: pl.BlockSpec((qrows, 128), lambda b, n: (n, 0))
    return pl.pallas_call(
        functools.partial(_att_lat_kernel, seq_len=L),
        out_shape=jax.ShapeDtypeStruct((b_, L, GROUP_W), F32),
        grid=(b_, nb // ATT_QB),
        in_specs=[
            pl.BlockSpec(memory_space=pltpu.SMEM),
            pl.BlockSpec((None, qrows, 256), lambda b, n: (b, n, COL_TQ // 256)),
            edge(kcol, prev), cur(kcol), edge(kcol, nxt),
            edge(vcol, prev), cur(vcol), edge(vcol, nxt),
            pl.BlockSpec((None, lc, 128), lambda b, n: (b, 0, kcol)),
            pl.BlockSpec((None, lc, 128), lambda b, n: (b, 0, vcol)),
            tab_edge(prev), tab_cur(), tab_edge(nxt), tab_edge(prev), tab_cur(), tab_edge(nxt),
        ],
        out_specs=pl.BlockSpec((None, qrows, GROUP_W), lambda b, n: (b, n, 0)),
        compiler_params=_cparams(("arbitrary", "arbitrary")),
        name="att_latent",
    )(sink, p_lat, p_lat, p_lat, p_lat, p_lat, p_lat, p_lat, p_ctx, p_ctx,
      cos_t, cos_t, cos_t, sin_t, sin_t, sin_t)


def _att_ctx_kernel(sink_ref, q_ref, k_ref, v_ref, o_ref):
    o_ref[...] = _attend(q_ref[...] * ATT_SCALE, k_ref[...], v_ref[...], None, sink_ref)


def _att_ctx(p_ctx, sink):
    b_, lc, _ = p_ctx.shape
    return pl.pallas_call(
        _att_ctx_kernel,
        out_shape=jax.ShapeDtypeStruct((b_, lc, GROUP_W), F32),
        grid=(b_,),
        in_specs=[
            pl.BlockSpec(memory_space=pltpu.SMEM),
            pl.BlockSpec((None, lc, 256), lambda b: (b, 0, COL_TQ // 256)),
            pl.BlockSpec((None, lc, 128), lambda b: (b, 0, COL_TK // 128)),
            pl.BlockSpec((None, lc, 128), lambda b: (b, 0, COL_TV // 128)),
        ],
        out_specs=pl.BlockSpec((None, lc, GROUP_W), lambda b: (b, 0, 0)),
        compiler_params=_cparams(("arbitrary",)),
        name="att_context",
    )(sink, p_ctx, p_ctx, p_ctx)


HG_T = 128
HG_SUB = 2
HG_LEVELS = (64, 32, 16)
_NT = (((1,), (1,)), ((), ()))
_TN = (((0,), (0,)), ((), ()))


def _row_fill(ref, rows, blk):
    w = ref.shape[-1]
    return jnp.concatenate([jnp.broadcast_to(ref[r:r + 1, :], (blk, w)) for r in rows], axis=0)


def _gla_block(q_raw, z, v, lb, st_ref, cum_scr, rev):
    t = HG_T
    e = jnp.exp(-jnp.abs(z))
    big = 1.0 / (1.0 + e)
    small = e * big
    sg = jnp.where(z >= 0, big, small)
    g = jnp.log(lb + (1.0 - lb) * sg)
    k = (1.0 - lb) * jnp.where(z >= 0, small, big)
    q = q_raw * jax.nn.sigmoid(q_raw)

    g_hi = g.astype(BF16)
    r1 = g - g_hi.astype(F32)
    g_mid = r1.astype(BF16)
    g_lo = (r1 - g_mid.astype(F32)).astype(BF16)
    ti = lax.broadcasted_iota(jnp.int32, (t, t), 0)
    si = lax.broadcasted_iota(jnp.int32, (t, t), 1)
    tri = jnp.where((si >= ti) if rev else (si <= ti), 1.0, 0.0).astype(BF16)
    c3 = jnp.dot(tri, jnp.concatenate([g_hi, g_mid, g_lo], axis=1), preferred_element_type=F32)
    cum = c3[:, 0:256] + c3[:, 256:512] + c3[:, 512:768]
    cum_scr[...] = cum
    edge = 0 if rev else t - 1
    g_tot = cum_scr[edge:edge + 1, :]

    lane = lax.broadcasted_iota(jnp.int32, (1, 128), 1)
    lo = lane < HG_DK
    rowi = lax.broadcasted_iota(jnp.int32, (t, 1), 0)
    vi = lax.broadcasted_iota(jnp.int32, (128, 128), 0)
    ki = lax.broadcasted_iota(jnp.int32, (128, 128), 1)
    same_head = (vi < HG_DV) == (ki < HG_DK)

    lo_b = jnp.where(lo, 1.0, 0.0).astype(BF16)
    hi_b = jnp.where(lo, 0.0, 1.0).astype(BF16)

    def scores(qt, kt):
        q2 = jnp.concatenate([qt * lo_b, qt * hi_b], axis=0)
        return lax.dot_general(q2, kt, _NT, preferred_element_type=F32)

    outs = []
    for pr in range(2):
        sl = slice(128 * pr, 128 * (pr + 1))
        qp, kp, vp, cp, gp = q[:, sl], k[:, sl], v[:, sl], cum[:, sl], g_tot[:, sl]
        qb, kb = qp.astype(BF16), kp.astype(BF16)
        a0 = jnp.zeros((t, t), F32)
        a1 = jnp.zeros((t, t), F32)
        for sh, h in zip((6, 5, 4), HG_LEVELS):
            rows = [b * 2 * h + (h if rev else h - 1) for b in range(t // (2 * h))]
            e = jnp.exp(-jnp.abs(cp - _row_fill(cum_scr.at[:, sl], rows, 2 * h)))
            odd = (jnp.right_shift(rowi, sh) & 1) == 1
            q_on = jnp.logical_not(odd) if rev else odd
            sc = scores(qb * jnp.where(q_on, e, 0.0).astype(BF16),
                        kb * jnp.where(q_on, 0.0, e).astype(BF16))
            if 2 * h < t:
                keep = jnp.right_shift(ti, sh + 1) == jnp.right_shift(si, sh + 1)
                a0 += jnp.where(keep, sc[:t], 0.0)
                a1 += jnp.where(keep, sc[t:], 0.0)
            else:
                a0 += sc[:t]
                a1 += sc[t:]
        rows = [b * HG_CHUNK + (HG_CHUNK // 2 if rev else HG_CHUNK // 2 - 1) for b in range(t // HG_CHUNK)]
        dd = cp - _row_fill(cum_scr.at[:, sl], rows, HG_CHUNK)
        sc = scores((qp * jnp.exp(dd)).astype(BF16), (kp * jnp.exp(-dd)).astype(BF16))
        keep = (jnp.right_shift(ti, 4) == jnp.right_shift(si, 4)) & ((si >= ti) if rev else (si <= ti))
        a0 += jnp.where(keep, sc[:t], 0.0)
        a1 += jnp.where(keep, sc[t:], 0.0)

        v2 = jnp.concatenate([jnp.where(lo, vp, 0.0), jnp.where(lo, 0.0, vp)], axis=0).astype(BF16)
        o = jnp.dot(jnp.concatenate([a0, a1], axis=1).astype(BF16), v2, preferred_element_type=F32)
        st = st_ref[pr]
        o += lax.dot_general((qp * jnp.exp(cp)).astype(BF16), st.astype(BF16), _NT,
                             preferred_element_type=F32)
        upd = lax.dot_general(vp.astype(BF16), (kp * jnp.exp(gp - cp)).astype(BF16), _TN,
                              preferred_element_type=F32)
        st_ref[pr] = st * jnp.exp(gp) + jnp.where(same_head, upd, 0.0)
        outs.append(o)
    return jnp.concatenate(outs, axis=1)


def _hgrn2_kernel(qf_ref, zf_ref, vf_ref, qb_ref, zb_ref, vb_ref, lb_ref, s0_ref,
                  of_ref, ob_ref, sl_ref, st_f, st_b, cum_scr):
    j = pl.program_id(1)

    @pl.when(j == 0)
    def _():
        st_f[...] = s0_ref[0]
        st_b[...] = s0_ref[1]

    for i in range(HG_SUB):
        rf = slice(HG_T * i, HG_T * (i + 1))
        of_ref[rf, :] = _gla_block(qf_ref[rf, :], zf_ref[rf, :], vf_ref[rf, :], lb_ref[0:1, :],
                                   st_f, cum_scr.at[0, i], False)
        rb = slice(HG_T * (HG_SUB - 1 - i), HG_T * (HG_SUB - i))
        ob_ref[rb, :] = _gla_block(qb_ref[rb, :], zb_ref[rb, :], vb_ref[rb, :], lb_ref[1:2, :],
                                   st_b, cum_scr.at[1, i], True)
    sl_ref[0] = st_f[...]
    sl_ref[1] = st_b[...]


def _hgrn2(p, lb, s0):
    b_, L, _ = p.shape
    rows = HG_T * HG_SUB
    n = L // rows
    fwd = lambda c: pl.BlockSpec((None, rows, GROUP_W), lambda b, j: (b, j, c))
    bwd = lambda c: pl.BlockSpec((None, rows, GROUP_W), lambda b, j: (b, n - 1 - j, c))
    st_spec = pl.BlockSpec((None, 2, 2, 128, 128), lambda b, j: (b, 0, 0, 0, 0))
    o_shape = jax.ShapeDtypeStruct((b_, L, GROUP_W), F32)
    return pl.pallas_call(
        _hgrn2_kernel,
        out_shape=(o_shape, o_shape, jax.ShapeDtypeStruct((b_, 2, 2, 128, 128), F32)),
        grid=(b_, n),
        in_specs=[fwd(0), fwd(1), fwd(3), bwd(0), bwd(2), bwd(3),
                  pl.BlockSpec((2, GROUP_W), lambda b, j: (0, 0)), st_spec],
        out_specs=(fwd(0), bwd(0), st_spec),
        scratch_shapes=[pltpu.VMEM((2, 128, 128), F32), pltpu.VMEM((2, 128, 128), F32),
                        pltpu.VMEM((2, HG_SUB, HG_T, GROUP_W), F32)],
        compiler_params=_cparams(("arbitrary", "arbitrary")),
        name="hgrn2",
    )(p, p, p, p, p, p, lb, s0)


S5_TT = 128
S5_NS = S5_GROUPS * S5_STATE
S5_LANES = 256


def _s5_kernel(uf_ref, ub_ref, wb_ref, wc_ref, a_ref, x0_ref, yf_ref, yb_ref, xl_ref,
               buf_f, buf_b, st, y_scr):
    j = pl.program_id(0)

    @pl.when(j == 0)
    def _():
        st[...] = x0_ref[...]

    halves = lambda r: jnp.concatenate([r[0], r[1]], axis=1).astype(BF16)
    buf_f[...] = jnp.dot(halves(uf_ref), wb_ref[0], preferred_element_type=F32)
    buf_b[...] = jnp.dot(halves(ub_ref), wb_ref[1], preferred_element_type=F32)

    for cc in range(S5_NS // S5_LANES):
        re = slice(cc * S5_LANES, (cc + 1) * S5_LANES)
        im = slice(S5_NS + cc * S5_LANES, S5_NS + (cc + 1) * S5_LANES)
        arf, aif, arb, aib = a_ref[0, :, re], a_ref[1, :, re], a_ref[2, :, re], a_ref[3, :, re]

        def step(t, carry, re=re, im=im, arf=arf, aif=aif, arb=arb, aib=aib):
            xrf, xif, xrb, xib = carry
            rf = pl.multiple_of(t * 8, 8)
            rb = pl.multiple_of((S5_TT - 1 - t) * 8, 8)
            nrf = arf * xrf - aif * xif + buf_f[pl.ds(rf, 8), re]
            nif = arf * xif + aif * xrf + buf_f[pl.ds(rf, 8), im]
            nrb = arb * xrb - aib * xib + buf_b[pl.ds(rb, 8), re]
            nib = arb * xib + aib * xrb + buf_b[pl.ds(rb, 8), im]
            buf_f[pl.ds(rf, 8), re] = nrf
            buf_f[pl.ds(rf, 8), im] = nif
            buf_b[pl.ds(rb, 8), re] = nrb
            buf_b[pl.ds(rb, 8), im] = nib
            return nrf, nif, nrb, nib

        fin = lax.fori_loop(0, S5_TT, step,
                            (st[0, :, re], st[1, :, re], st[2, :, re], st[3, :, re]), unroll=4)
        for k in range(4):
            st[k, :, re] = fin[k]

    for y_ref, buf, d in ((yf_ref, buf_f, 0), (yb_ref, buf_b, 1)):
        y = jnp.dot(buf[...].astype(BF16), wc_ref[d], preferred_element_type=F32)
        for c in range(2):
            y_scr[c] = y[:, 128 * c:128 * (c + 1)]
        for b in range(8):
            for c in range(2):
                y_ref[b, :, 128 * c:128 * (c + 1)] = y_scr.at[c][pl.ds(b, S5_TT, stride=8), :]
    xl_ref[...] = st[...]


def _s5_scan(u2, wb, wc, a_bc, x0):
    L = u2.shape[1] // 8
    rows = S5_TT * 8
    n = L // S5_TT
    y_shape = jax.ShapeDtypeStruct((8, L, GROUP_W), F32)
    return pl.pallas_call(
        _s5_kernel,
        out_shape=(y_shape, y_shape, jax.ShapeDtypeStruct((4, 8, S5_NS), F32)),
        grid=(n,),
        in_specs=[
            pl.BlockSpec((2, rows, 128), lambda j: (0, j, 0)),
            pl.BlockSpec((2, rows, 128), lambda j: (0, n - 1 - j, 0)),
            pl.BlockSpec((2, GROUP_W, 2 * S5_NS), lambda j: (0, 0, 0)),
            pl.BlockSpec((2, 2 * S5_NS, GROUP_W), lambda j: (0, 0, 0)),
            pl.BlockSpec((4, 8, S5_NS), lambda j: (0, 0, 0)),
            pl.BlockSpec((4, 8, S5_NS), lambda j: (0, 0, 0)),
        ],
        out_specs=(pl.BlockSpec((8, S5_TT, GROUP_W), lambda j: (0, j, 0)),
                   pl.BlockSpec((8, S5_TT, GROUP_W), lambda j: (0, n - 1 - j, 0)),
                   pl.BlockSpec((4, 8, S5_NS), lambda j: (0, 0, 0))),
        scratch_shapes=[pltpu.VMEM((rows, 2 * S5_NS), F32),
                        pltpu.VMEM((rows, 2 * S5_NS), F32),
                        pltpu.VMEM((4, 8, S5_NS), F32),
                        pltpu.VMEM((2, rows, 128), F32)],
        compiler_params=_cparams(("arbitrary",)),
        name="s5_scan",
    )(u2, u2, wb, wc, a_bc, x0)


def _s5_weights(a_re, a_im, log_step, b_re, b_im, c_re, c_im):
    eye = jnp.eye(S5_GROUPS, dtype=F32)
    wbs, wcs, abc = [], [], []
    for d in range(2):
        ab_re, ab_im, bb_re, bb_im = _s5_discretize(a_re[d], a_im[d], log_step[d], b_re, b_im)
        wb_re = jnp.einsum('gph,gk->ghkp', bb_re, eye).reshape(GROUP_W, S5_NS)
        wb_im = jnp.einsum('gph,gk->ghkp', bb_im, eye).reshape(GROUP_W, S5_NS)
        wbs.append(jnp.concatenate([wb_re, wb_im], axis=1))
        wc_re = jnp.einsum('ghp,gk->gpkh', c_re[d], eye).reshape(S5_NS, GROUP_W)
        wc_im = jnp.einsum('ghp,gk->gpkh', c_im[d], eye).reshape(S5_NS, GROUP_W)
        wcs.append(jnp.concatenate([wc_re, -wc_im], axis=0))
        abc += [jnp.broadcast_to(ab_re.reshape(1, S5_NS), (8, S5_NS)),
                jnp.broadcast_to(ab_im.reshape(1, S5_NS), (8, S5_NS))]
    return jnp.stack(wbs).astype(BF16), jnp.stack(wcs).astype(BF16), jnp.stack(abc)


def _group_norm(m, w):
    return m * lax.rsqrt(jnp.mean(m * m, axis=-1, keepdims=True) + EPS) * w


def _merge_kernel(hgf_ref, hgb_ref, yf_ref, yb_ref, u_ref, sd_ref, glu_ref, hy_ref, at_ref, gate_ref,
                  x_ref, mod_ref, mw_ref, fw_ref, wo_ref, xo_ref, ho_ref):
    gate = gate_ref[...]
    z = jax.nn.gelu(yf_ref[...] + yb_ref[...] + sd_ref[...] * u_ref[...])
    s5 = z * jax.nn.sigmoid(jnp.dot(z.astype(BF16), glu_ref[...], preferred_element_type=F32))
    parts = [
        _group_norm(hgf_ref[...] + hgb_ref[...], mw_ref[:, 0:256]) * (gate * jax.nn.sigmoid(gate)),
        _group_norm(s5, mw_ref[:, 256:512]),
        _group_norm(jnp.concatenate([hy_ref[s].T for s in range(hy_ref.shape[0])], axis=0),
                    mw_ref[:, 512:768]),
        _group_norm(at_ref[...], mw_ref[:, 768:1024]),
    ]
    mix = jnp.concatenate(parts, axis=1).astype(BF16)
    xn = x_ref[...] + mod_ref[2:3, :] * jnp.dot(mix, wo_ref[...], preferred_element_type=F32)
    xo_ref[...] = xn
    h = xn * lax.rsqrt(jnp.mean(xn * xn, axis=-1, keepdims=True) + EPS) * fw_ref[...]
    ho_ref[...] = (h * mod_ref[3:4, :] + mod_ref[4:5, :]).astype(BF16)


def _merge(hg_f, hg_b, s5_yf, s5_yb, s5_d, glu_bf16, hy, at, p, x, mods, mw, fw, wo_bf16, tm):
    b_, L, _ = x.shape
    grp = lambda: pl.BlockSpec((None, tm, GROUP_W), lambda i, b: (b, i, 0))
    pcol = lambda c: pl.BlockSpec((None, tm, GROUP_W), lambda i, b: (b, i, c))
    return pl.pallas_call(
        _merge_kernel,
        out_shape=(jax.ShapeDtypeStruct((b_, L, D_MODEL), F32),
                   jax.ShapeDtypeStruct((b_, L, D_MODEL), BF16)),
        grid=(L // tm, b_),
        in_specs=[
            grp(), grp(), grp(), grp(), pcol(P_COLS // GROUP_W),
            pl.BlockSpec((1, GROUP_W), lambda i, b: (0, 0)),
            pl.BlockSpec((GROUP_W, GROUP_W), lambda i, b: (0, 0)),
            pl.BlockSpec((None, tm // 128, HY_CH, 128), lambda i, b: (b, i, 0, 0)),
            grp(),
            pl.BlockSpec((None, tm, GROUP_W), lambda i, b: (b, i, 4)),
            pl.BlockSpec((None, tm, D_MODEL), lambda i, b: (b, i, 0)),
            pl.BlockSpec((None, 8, D_MODEL), lambda i, b: (b, 0, 0)),
            pl.BlockSpec((1, D_MODEL), lambda i, b: (0, 0)),
            pl.BlockSpec((1, D_MODEL), lambda i, b: (0, 0)),
            pl.BlockSpec((D_MODEL, D_MODEL), lambda i, b: (0, 0)),
        ],
        out_specs=(pl.BlockSpec((None, tm, D_MODEL), lambda i, b: (b, i, 0)),
                   pl.BlockSpec((None, tm, D_MODEL), lambda i, b: (b, i, 0))),
        compiler_params=_cparams(("arbitrary", "arbitrary")),
        name="merge_out_proj",
    )(hg_f, hg_b, s5_yf, s5_yb, p, s5_d.reshape(1, GROUP_W), glu_bf16, hy, at, p, x, mods,
      mw.reshape(1, D_MODEL), fw.reshape(1, D_MODEL), wo_bf16)


def _ffn_kernel(hp_ref, h_ref, hn_ref, x_ref, mod_ref, wa_ref, wv_ref, cw_ref, wd_ref, fw_ref, o_ref,
                acc_scr, *, tm, final_norm):
    i = pl.program_id(1)
    last = pl.num_programs(1) - 1
    h = h_ref[...]
    hp = jnp.where(i > 0, hp_ref[...], jnp.zeros_like(hp_ref))
    hn = jnp.where(i < last, hn_ref[...], jnp.zeros_like(hn_ref))
    h_ext = jnp.concatenate([hp, h, hn], axis=0)
    rows = tm + 2 * HALO

    def gated(j):
        a = jnp.dot(h_ext, wa_ref[j], preferred_element_type=F32)
        v = jnp.dot(h, wv_ref[j], preferred_element_type=F32)
        cw = cw_ref[j]
        conv = (pltpu.roll(a, 1, 0)[HALO:HALO + tm] * cw[0:1, :]
                + a[HALO:HALO + tm] * cw[1:2, :]
                + pltpu.roll(a, rows - 1, 0)[HALO:HALO + tm] * cw[2:3, :] + cw[3:4, :])
        return (conv * jax.nn.sigmoid(conv) * v).astype(BF16)

    for j in range(0, N_FF_CHUNKS, 2):
        n = min(2, N_FF_CHUNKS - j)
        g = jnp.concatenate([gated(j + d) for d in range(n)], axis=1)
        wd = wd_ref[j:j + n].reshape(n * FF_CHUNK, D_MODEL)
        part = jnp.dot(g, wd, preferred_element_type=F32)
        if j == 0:
            acc_scr[...] = part
        elif j + n < N_FF_CHUNKS:
            acc_scr[...] += part
        else:
            xo = x_ref[...] + mod_ref[5:6, :] * (acc_scr[...] + part)
            if final_norm:
                xo = xo * lax.rsqrt(jnp.mean(xo * xo, axis=-1, keepdims=True) + EPS) * fw_ref[...]
            o_ref[...] = xo


def _ffn(h, x, mods, wa, wv, cw, wd, tm, final_w=None):
    b_, L, _ = x.shape
    nh = L // HALO
    r = tm // HALO
    const3 = lambda b, i: (0, 0, 0)
    fw = jnp.ones((1, D_MODEL), F32) if final_w is None else final_w.reshape(1, D_MODEL)
    return pl.pallas_call(
        functools.partial(_ffn_kernel, tm=tm, final_norm=final_w is not None),
        out_shape=jax.ShapeDtypeStruct((b_, L, D_MODEL), F32),
        grid=(b_, L // tm),
        in_specs=[
            pl.BlockSpec((None, HALO, D_MODEL), lambda b, i: (b, jnp.maximum(i * r - 1, 0), 0)),
            pl.BlockSpec((None, tm, D_MODEL), lambda b, i: (b, i, 0)),
            pl.BlockSpec((None, HALO, D_MODEL), lambda b, i: (b, jnp.minimum((i + 1) * r, nh - 1), 0)),
            pl.BlockSpec((None, tm, D_MODEL), lambda b, i: (b, i, 0)),
            pl.BlockSpec((None, 8, D_MODEL), lambda b, i: (b, 0, 0)),
            pl.BlockSpec((N_FF_CHUNKS, D_MODEL, FF_CHUNK), const3, pipeline_mode=pl.Buffered(1)),
            pl.BlockSpec((N_FF_CHUNKS, D_MODEL, FF_CHUNK), const3, pipeline_mode=pl.Buffered(1)),
            pl.BlockSpec((N_FF_CHUNKS, 8, FF_CHUNK), const3),
            pl.BlockSpec((N_FF_CHUNKS, FF_CHUNK, D_MODEL), const3, pipeline_mode=pl.Buffered(1)),
            pl.BlockSpec((1, D_MODEL), lambda b, i: (0, 0)),
        ],
        out_specs=pl.BlockSpec((None, tm, D_MODEL), lambda b, i: (b, i, 0)),
        scratch_shapes=[pltpu.VMEM((tm, D_MODEL), F32)],
        compiler_params=_cparams(("arbitrary", "arbitrary")),
        name="conv_ffn",
    )(h, h, h, x, mods, wa, wv, cw, wd, fw)


def _s5_discretize(a_re, a_im, log_step, b_re, b_im):
    dt = jnp.exp(log_step)[:, None]
    mag = jnp.exp(a_re * dt)
    ang = a_im * dt
    ab_re, ab_im = mag * jnp.cos(ang), mag * jnp.sin(ang)
    den = a_re * a_re + a_im * a_im
    nr, ni = ab_re - 1.0, ab_im
    fr = (nr * a_re + ni * a_im) / den
    fi = (ni * a_re - nr * a_im) / den
    bb_re = fr[..., None] * b_re - fi[..., None] * b_im
    bb_im = fr[..., None] * b_im + fi[..., None] * b_re
    return ab_re, ab_im, bb_re, bb_im


HY_LANES = 128
HY_CB_LATENT = 16
HY_CB_CONTEXT = 64
HY_EMB_PAD = 128


def _vadd(a, b):
    return b if a is None else a if b is None else a + b


def _vsub(a, b):
    return (None if b is None else -b) if a is None else a if b is None else a - b


def _vscale(a, c):
    return None if (a is None or c == 0.0) else a if c == 1.0 else -a if c == -1.0 else a * c


def _cmul_const(x, wr, wi):
    re, im = x
    return (_vsub(_vscale(re, wr), _vscale(im, wi)), _vadd(_vscale(im, wr), _vscale(re, wi)))


def _unit_root(k, n, sign):
    k %= n
    if (4 * k) % n == 0:
        return ((1.0, 0.0), (0.0, float(sign)), (-1.0, 0.0), (0.0, float(-sign)))[4 * k // n]
    ang = sign * 2.0 * math.pi * k / n
    return math.cos(ang), math.sin(ang)


def _fft_slabs(x, sign, n_out=None):
    n = len(x)
    if n == 1:
        return list(x)
    n_out = n if n_out is None else n_out
    ev = _fft_slabs(x[0::2], sign)
    od = _fft_slabs(x[1::2], sign)
    out = [None] * n_out
    for k in range(n // 2):
        t = _cmul_const(od[k], *_unit_root(k, n, sign))
        if k < n_out:
            out[k] = (_vadd(ev[k][0], t[0]), _vadd(ev[k][1], t[1]))
        if k + n // 2 < n_out:
            out[k + n // 2] = (_vsub(ev[k][0], t[0]), _vsub(ev[k][1], t[1]))
    return out


def _twiddle(slabs, tw_ref, conj):
    out = [slabs[0]]
    for k in range(1, len(slabs)):
        re, im = slabs[k]
        tr = tw_ref[0, k]
        ti = -tw_ref[1, k] if conj else tw_ref[1, k]
        if im is None:
            out.append((re * tr, re * ti))
        else:
            out.append((re * tr - im * ti, re * ti + im * tr))
    return out


def _dot_split(x, w_ref):
    hi = x.astype(BF16)
    lo = (x - hi.astype(F32)).astype(BF16)
    return (jnp.dot(hi, w_ref[0], preferred_element_type=F32)
            + jnp.dot(lo, w_ref[0], preferred_element_type=F32)
            + jnp.dot(hi, w_ref[1], preferred_element_type=F32))


def _lane_dft(slabs, w_ref, cb):
    zero = jnp.zeros((cb, HY_LANES), F32)
    rows = jnp.concatenate(
        [jnp.concatenate([zero if re is None else re, zero if im is None else im], axis=1)
         for re, im in slabs], axis=0)
    out = _dot_split(rows, w_ref)
    return [(out[cb * i:cb * (i + 1), :HY_LANES], out[cb * i:cb * (i + 1), HY_LANES:])
            for i in range(len(slabs))]


def _short_conv(ref, part, cw_ref, ch, lane):
    n = ref.shape[1]
    w0, w1, w2, bb = cw_ref[ch, 0], cw_ref[ch, 1], cw_ref[ch, 2], cw_ref[ch, 3]
    right = [pltpu.roll(ref[part, s], 1, 1) for s in range(n)]
    left = [pltpu.roll(ref[part, s], HY_LANES - 1, 1) for s in range(n)]
    zero = jnp.zeros(lane.shape, F32)
    out = []
    for s in range(n):
        xm = jnp.where(lane == 0, right[s - 1] if s > 0 else zero, right[s])
        xp = jnp.where(lane == HY_LANES - 1, left[s + 1] if s < n - 1 else zero, left[s])
        out.append(xm * w0 + ref[part, s] * w1 + xp * w2 + bb)
    return out


def _hy_conv_kernel(x1_ref, x2_ref, z_ref, cw_ref, bias_ref, kf_ref, tw_ref, wf_ref, wi_ref, o_ref,
                    *, n1):
    nh = n1 // 2
    cb = z_ref.shape[2]
    lane = lax.broadcasted_iota(jnp.int32, (cb, HY_LANES), 1)
    gates = [[_short_conv(r, part, cw_ref, ch, lane) for part in range(2)]
             for ch, r in ((0, x1_ref), (1, x2_ref))]
    z = [_short_conv(z_ref, part, cw_ref, 2, lane) for part in range(2)]
    for o in range(HY_ORDER):
        spec = _fft_slabs([(z[0][s], z[1][s]) for s in range(nh)] + [(None, None)] * nh, -1)
        spec = _lane_dft(_twiddle(spec, tw_ref, False), wf_ref, cb)
        spec = [(re * kf_ref[o, 0, k] - im * kf_ref[o, 1, k], re * kf_ref[o, 1, k] + im * kf_ref[o, 0, k])
                for k, (re, im) in enumerate(spec)]
        spec = _twiddle(_lane_dft(spec, wi_ref, cb), tw_ref, True)
        y = _fft_slabs(spec, 1, n_out=nh)
        bo = bias_ref[o]
        z = [[gates[o][part][s] * (y[s][part] + bo * z[part][s]) for s in range(nh)]
             for part in range(2)]
    for part in range(2):
        for s in range(nh):
            o_ref[part, s] = z[part][s]


def _hy_conv(zh, cw, bias_b, kf, tw, wf, wi):
    b_, nh = zh.shape[0], zh.shape[1]
    cb = tw.shape[2]
    half, ncb = b_ // 2, HY_CH // cb
    n1 = 2 * nh
    zspec = lambda ch: pl.BlockSpec((2, None, nh, cb, HY_LANES),
                                    lambda c, b: (0, b, 0, ch * ncb + c, 0))
    const3 = lambda c, b: (0, 0, 0)
    z5 = zh.reshape(2, half, nh, 3 * HY_CH, HY_LANES)
    out = pl.pallas_call(
        functools.partial(_hy_conv_kernel, n1=n1),
        out_shape=jax.ShapeDtypeStruct((2, half, nh, HY_CH, HY_LANES), F32),
        grid=(ncb, half),
        in_specs=[
            zspec(0), zspec(1), zspec(2),
            pl.BlockSpec((3, 4, cb, HY_LANES), lambda c, b: (0, 0, c, 0)),
            pl.BlockSpec((HY_ORDER, cb, HY_LANES), lambda c, b: (0, c, 0)),
            pl.BlockSpec((HY_ORDER, 2, None, n1, cb, HY_LANES), lambda c, b: (0, 0, c, 0, 0, 0)),
            pl.BlockSpec((2, n1, cb, HY_LANES), lambda c, b: (0, 0, 0, 0)),
            pl.BlockSpec((2, 256, 256), const3),
            pl.BlockSpec((2, 256, 256), const3),
        ],
        out_specs=pl.BlockSpec((2, None, nh, cb, HY_LANES), lambda c, b: (0, b, 0, c, 0)),
        compiler_params=_cparams(("arbitrary", "arbitrary")),
        name="hyena_conv",
    )(z5, z5, z5, cw, bias_b, kf, tw, wf, wi)
    return out.reshape(b_, nh, HY_CH, HY_LANES)


def _hy_spectrum_kernel(k_ref, tw_ref, wf_ref, o_ref, *, n1):
    spec = _fft_slabs([(k_ref[s], None) for s in range(n1)], -1)
    spec = _lane_dft(_twiddle(spec, tw_ref, False), wf_ref, k_ref.shape[1])
    scale = 1.0 / (n1 * HY_LANES)
    for k in range(n1):
        o_ref[0, k] = spec[k][0] * scale
        o_ref[1, k] = spec[k][1] * scale


def _hy_spectrum(taps, tw, wf):
    n1, cb = taps.shape[1], tw.shape[2]
    return pl.pallas_call(
        functools.partial(_hy_spectrum_kernel, n1=n1),
        out_shape=jax.ShapeDtypeStruct((HY_ORDER, 2, HY_CH // cb, n1, cb, HY_LANES), F32),
        grid=(HY_ORDER, HY_CH // cb),
        in_specs=[
            pl.BlockSpec((None, n1, cb, HY_LANES), lambda o, c: (o, 0, c, 0)),
            pl.BlockSpec((2, n1, cb, HY_LANES), lambda o, c: (0, 0, 0, 0)),
            pl.BlockSpec((2, 256, 256), lambda o, c: (0, 0, 0)),
        ],
        out_specs=pl.BlockSpec((None, 2, None, n1, cb, HY_LANES), lambda o, c: (o, 0, c, 0, 0, 0)),
        compiler_params=_cparams(("arbitrary", "arbitrary")),
        name="hyena_spectrum",
    )(taps, tw, wf)


def _dot_f32(a, b, dims=(((1,), (0,)), ((), ()))):
    a_hi = a.astype(BF16)
    a_lo = (a - a_hi.astype(F32)).astype(BF16)
    b_hi = b.astype(BF16)
    b_lo = (b - b_hi.astype(F32)).astype(BF16)
    dot = lambda p, q: lax.dot_general(p, q, dims, preferred_element_type=F32)
    return dot(a_hi, b_hi) + dot(a_lo, b_hi) + dot(a_hi, b_lo)


def _hy_mlp_kernel(emb_ref, w1_ref, w2_ref, w3t_ref, vec_ref, pos_ref, dl_ref, o_ref):
    z = emb_ref[...]
    h = jnp.sin(vec_ref[2:3, :] * (_dot_f32(z, w1_ref[...]) + vec_ref[0:1, :]))
    h = jnp.sin(vec_ref[3:4, :] * (_dot_f32(h, w2_ref[...]) + vec_ref[1:2, :]))
    ht = _dot_f32(w3t_ref[...], h, _NT)
    dl = dl_ref[...]
    for s in range(o_ref.shape[1]):
        lanes = slice(HY_LANES * s, HY_LANES * (s + 1))
        win = jnp.exp(-pos_ref[0:1, lanes] * dl)
        wf, wb = pos_ref[1:2, lanes] * win, pos_ref[2:3, lanes] * win
        for o in range(HY_ORDER):
            base = 2 * HY_CH * o
            o_ref[o, s] = (ht[base:base + HY_CH, lanes] * wf
                           + ht[base + HY_CH:base + 2 * HY_CH, lanes] * wb)


def _hy_filter_taps(L, w1, b1, freq, w2, b2, w3):
    pad = HY_EMB_PAD
    n = 2 * L
    t01 = np.linspace(0.0, 1.0, L, dtype=np.float32)[:, None]
    w = (np.float32(2.0 * math.pi) * np.arange(L, dtype=np.float32)[:, None]) / np.float32(L)
    bands = (HY_EMB - 1) // 2
    fr = np.linspace(1e-4, bands - 1, bands, dtype=np.float32)[None, :]
    arg = (fr * w).astype(np.float64)
    emb = np.zeros((L, pad), np.float32)
    emb[:, 0:1] = t01
    emb[:, 1:1 + bands] = np.cos(arg)
    emb[:, 1 + bands:HY_EMB] = -np.sin(arg)
    pos_idx = np.arange(n)
    lag = np.where(pos_idx < L, pos_idx, (n - pos_idx) % L)
    pos = np.zeros((8, n), np.float32)
    pos[0] = t01[lag, 0]
    pos[1] = pos_idx < L
    pos[2] = (pos_idx > L) | (pos_idx == 0)
    hid = w1.shape[1]
    n_out = w3.shape[1]
    w1p = jnp.zeros((pad, pad), F32).at[:HY_EMB, :hid].set(w1)
    w2p = jnp.zeros((pad, pad), F32).at[:hid, :hid].set(w2)
    w3t = jnp.zeros((n_out, pad), F32).at[:, :hid].set(w3.T)
    vec = jnp.zeros((8, pad), F32).at[0, :hid].set(b1).at[1, :hid].set(b2)
    vec = vec.at[2, :hid].set(freq[0]).at[3, :hid].set(freq[1])
    deltas = np.abs(np.linspace(HY_MIN_DECAY, HY_MAX_DECAY, HY_CH, dtype=np.float32))
    dl = np.broadcast_to(deltas[:, None], (HY_CH, HY_LANES))
    tl = min(n, 512)
    return pl.pallas_call(
        _hy_mlp_kernel,
        out_shape=jax.ShapeDtypeStruct((HY_ORDER, n // HY_LANES, HY_CH, HY_LANES), F32),
        grid=(n // tl,),
        in_specs=[
            pl.BlockSpec((tl, pad), lambda i: (i, 0)),
            pl.BlockSpec((pad, pad), lambda i: (0, 0)),
            pl.BlockSpec((pad, pad), lambda i: (0, 0)),
            pl.BlockSpec((n_out, pad), lambda i: (0, 0)),
            pl.BlockSpec((8, pad), lambda i: (0, 0)),
            pl.BlockSpec((8, tl), lambda i: (0, i)),
            pl.BlockSpec((HY_CH, HY_LANES), lambda i: (0, 0)),
        ],
        out_specs=pl.BlockSpec((HY_ORDER, tl // HY_LANES, HY_CH, HY_LANES), lambda i: (0, i, 0, 0)),
        compiler_params=_cparams(("arbitrary",)),
        name="hyena_filter_mlp",
    )(jnp.asarray(emb[lag]), w1p, w2p, w3t, vec, jnp.asarray(pos), jnp.asarray(dl))


def _hy_constants(n1, cb):
    n = n1 * HY_LANES
    ang = -2.0 * np.pi * np.outer(np.arange(n1), np.arange(HY_LANES)) / n
    tw = np.stack([np.cos(ang), np.sin(ang)])[:, :, None, :].repeat(cb, axis=2).astype(np.float32)
    a2 = -2.0 * np.pi * np.outer(np.arange(HY_LANES), np.arange(HY_LANES)) / HY_LANES
    fr, fi = np.cos(a2), np.sin(a2)
    fwd = np.block([[fr, fi], [-fi, fr]])
    inv = np.block([[fr, -fi], [fi, fr]])

    def split(m):
        hi = m.astype(BF16)
        lo = (m - hi.astype(np.float64)).astype(BF16)
        return jnp.asarray(np.stack([hi, lo]))

    return jnp.asarray(tw), split(fwd), split(inv)


def _regroup_in_proj(w):
    d = w.shape[0]
    half = ATT_HD // 2
    q = w[:, W_COL_TQ:W_COL_TK].reshape(d, ATT_KV, 2, half, 2)
    q = q.transpose(0, 2, 1, 4, 3).reshape(d, ATT_HEADS * ATT_HD)
    k = w[:, W_COL_TK:W_COL_TV].reshape(d, ATT_KV, half, 2).transpose(0, 1, 3, 2).reshape(d, ATT_KV * ATT_HD)
    return jnp.concatenate([w[:, :W_COL_S5], q, k, w[:, W_COL_TV:], w[:, W_COL_S5:W_COL_HY]], axis=1)


def _regroup_att_rows(m):
    at = m[3 * GROUP_W:].reshape((2, 2, ATT_HD) + m.shape[1:])
    at = jnp.swapaxes(at, 0, 1).reshape((GROUP_W,) + m.shape[1:])
    return jnp.concatenate([m[:3 * GROUP_W], at], axis=0)


def _rope_tables(L):
    rows = L // GRID_W
    row = jnp.repeat(jnp.arange(rows), GRID_W).astype(F32)
    col = jnp.tile(jnp.arange(GRID_W), rows).astype(F32)
    axis_dim = ATT_HD // 2
    inv = 1.0 / (ROPE_BASE ** (jnp.arange(0, axis_dim, 2, dtype=F32) / axis_dim))
    ang = jnp.concatenate([row[:, None] * inv, col[:, None] * inv], axis=-1)
    c, s = jnp.cos(ang), jnp.sin(ang)
    return jnp.tile(jnp.concatenate([c, c], axis=1), (1, 2)), jnp.tile(jnp.concatenate([-s, s], axis=1), (1, 2))


def kernel(x, c, ctx, c_ctx, ada_w, ada_b, norm_mix_w, norm_ffn_w, w_in, hg_lb_logits, s5_a_re, s5_a_im, s5_log_step, s5_b_re, s5_b_im, s5_c_re, s5_c_im, s5_d, s5_glu_w, hy_conv_w, hy_conv_b, hy_w1, hy_b1, hy_freq, hy_w2, hy_b2, hy_w3, hy_bias, att_sink, merge_norm_w, w_out, ffn_w_up, ffn_conv_w, ffn_conv_b, ffn_w_down, final_norm_w):
    bsz, seq_len, _ = x.shape
    ctx_len = ctx.shape[1]

    cond = jnp.zeros((16, D_MODEL), F32).at[:bsz].set(c).at[bsz].set(c_ctx)
    ada = _ada_all(cond, ada_w, ada_b)

    def mods_of(m):
        sh1, sc1, g1, sh2, sc2, g2 = jnp.split(m, 6, axis=-1)
        z = jnp.zeros_like(sh1)
        return jnp.stack([1.0 + sc1, sh1, g1, 1.0 + sc2, sh2, g2, z, z], axis=1)

    sm = jax.nn.softmax(hg_lb_logits, axis=0)
    lower_bounds = jnp.cumsum(sm, axis=0) - sm[0:1]
    cos_t, sin_t = _rope_tables(seq_len)
    hg_zero = jnp.zeros((bsz, 2, 2, 128, 128), F32)
    hy_tw_lat, hy_wf, hy_wi = _hy_constants(2 * seq_len // HY_LANES, HY_CB_LATENT)
    hy_tw_ctx, _, _ = _hy_constants(2 * ctx_len // HY_LANES, HY_CB_CONTEXT)
    s5_zero = jnp.zeros((4, bsz, S5_NS), F32)

    xc = ctx
    for l in range(DEPTH):
        last = l == DEPTH - 1
        mods_lat = mods_of(ada[l, :bsz])
        mods_ctx = jnp.broadcast_to(mods_of(ada[l, bsz:bsz + 1]), (bsz, 8, D_MODEL))
        w_in_l = _regroup_in_proj(w_in[l]).astype(BF16)
        mw = _regroup_att_rows(merge_norm_w[l])
        wo = _regroup_att_rows(w_out[l]).astype(BF16)
        wa = ffn_w_up[l][:, :D_FF].reshape(D_MODEL, N_FF_CHUNKS, FF_CHUNK).transpose(1, 0, 2).astype(BF16)
        wv = ffn_w_up[l][:, D_FF:].reshape(D_MODEL, N_FF_CHUNKS, FF_CHUNK).transpose(1, 0, 2).astype(BF16)
        wd = ffn_w_down[l].reshape(N_FF_CHUNKS, FF_CHUNK, D_MODEL).astype(BF16)
        cw = jnp.concatenate([ffn_conv_w[l], ffn_conv_b[l][None], jnp.zeros((4, D_FF), F32)], axis=0)
        cw = cw.reshape(8, N_FF_CHUNKS, FF_CHUNK).transpose(1, 0, 2)

        wh = w_in[l][:, W_COL_HY:W_COL_HY + 3 * HY_CH].T.astype(BF16)
        p_lat, u_lat, zh_lat = _inproj(x, mods_lat, norm_mix_w[l], w_in_l, wh, 512)
        p_ctx, u_ctx, zh_ctx = _inproj(xc, mods_ctx, norm_mix_w[l], w_in_l, wh, ctx_len)

        hgf_ctx, hgb_ctx, hg_state = _hgrn2(p_ctx, lower_bounds[l], hg_zero)
        hgf_lat, hgb_lat, _ = _hgrn2(p_lat, lower_bounds[l], hg_state)

        s5_wb, s5_wc, s5_a = _s5_weights(s5_a_re[l], s5_a_im[l], s5_log_step[l], s5_b_re[l],
                                         s5_b_im[l], s5_c_re[l], s5_c_im[l])
        glu = s5_glu_w[l].astype(BF16)
        yf_ctx, yb_ctx, s5_state = _s5_scan(u_ctx, s5_wb, s5_wc, s5_a, s5_zero)
        yf_lat, yb_lat, _ = _s5_scan(u_lat, s5_wb, s5_wc, s5_a, s5_state)

        hy_params = (hy_w1[l], hy_b1[l], hy_freq[l], hy_w2[l], hy_b2[l], hy_w3[l])
        hy_cw = jnp.concatenate([hy_conv_w[l], hy_conv_b[l][None]], axis=0)
        hy_cw = jnp.broadcast_to(hy_cw.reshape(4, 3, HY_CH).transpose(1, 0, 2)[..., None],
                                 (3, 4, HY_CH, HY_LANES))
        hy_bb = jnp.broadcast_to(hy_bias[l][..., None], (HY_ORDER, HY_CH, HY_LANES))
        kf_lat = _hy_spectrum(_hy_filter_taps(seq_len, *hy_params), hy_tw_lat, hy_wf)
        hy_lat = _hy_conv(zh_lat, hy_cw, hy_bb, kf_lat, hy_tw_lat, hy_wf, hy_wi)

        at_lat = _att_lat(p_lat, p_ctx, cos_t, sin_t, att_sink[l])

        x_mid, h_lat = _merge(hgf_lat, hgb_lat, yf_lat, yb_lat, s5_d[l], glu, hy_lat, at_lat, p_lat, x,
                              mods_lat, mw, norm_ffn_w[l], wo, 512)
        x_new = _ffn(h_lat, x_mid, mods_lat, wa, wv, cw, wd, 512,
                     final_w=final_norm_w if last else None)
        if not last:
            kf_ctx = _hy_spectrum(_hy_filter_taps(ctx_len, *hy_params), hy_tw_ctx, hy_wf)
            hy_ctx = _hy_conv(zh_ctx, hy_cw, hy_bb, kf_ctx, hy_tw_ctx, hy_wf, hy_wi)
            at_ctx = _att_ctx(p_ctx, att_sink[l])
            xc_mid, h_ctx = _merge(hgf_ctx, hgb_ctx, yf_ctx, yb_ctx, s5_d[l], glu, hy_ctx, at_ctx, p_ctx,
                                   xc, mods_ctx, mw, norm_ffn_w[l], wo, ctx_len)
            xc = _ffn(h_ctx, xc_mid, mods_ctx, wa, wv, cw, wd, ctx_len)
        x = x_new
    return x
```

```python
import functools
import math

import jax
import jax.numpy as jnp
import numpy as np
from jax import lax
from jax.experimental import pallas as pl
from jax.experimental.pallas import tpu as pltpu

F32 = jnp.float32
BF16 = jnp.bfloat16

D_MODEL = 1024
DEPTH = 4
GRID_W = 64
GROUP_W = 256
HG_DK = 64
HG_DV = 64
HG_HEADS = 4
HG_CHUNK = 16
S5_GROUP_CH = 16
S5_GROUPS = 16
S5_STATE = 64
HY_CH = 256
HY_ORDER = 2
HY_EMB = 33
HY_MAX_DECAY = math.log(1e-2) / 0.3
HY_MIN_DECAY = math.log(1e-2) / 1.5
ATT_HD = 64
ATT_HEADS = 4
ATT_KV = 2
WINDOW = 128
ATT_BLOCK = 128
ATT_QB = 2
ATT_SCALE = 1.0 / math.sqrt(ATT_HD)
ROPE_BASE = 10000.0
D_FF = 2816
EPS = 1e-6
IN_COLS = 2816
W_COL_S5 = 1280
W_COL_HY = 1536
W_COL_TQ = 2304
W_COL_TK = 2560
W_COL_TV = 2688
P_COLS = 1792
COL_TQ = 1280
COL_TK = 1536
COL_TV = 1664

FF_CHUNK = 256
N_FF_CHUNKS = D_FF // FF_CHUNK
HALO = 16
VMEM_LIMIT = 56 * 1024 * 1024


def _cparams(sem):
    return pltpu.CompilerParams(dimension_semantics=sem, vmem_limit_bytes=VMEM_LIMIT)


def _ada_kernel(c_ref, w_ref, b_ref, o_ref):
    cond = c_ref[...]
    act = cond * jax.nn.sigmoid(cond)
    o_ref[...] = jnp.dot(act.astype(BF16), w_ref[...].astype(BF16),
                         preferred_element_type=F32) + b_ref[...]


def _ada_all(cond, ada_w, ada_b):
    tn = 1024
    n6 = 6 * D_MODEL
    return pl.pallas_call(
        _ada_kernel,
        out_shape=jax.ShapeDtypeStruct((DEPTH, 16, n6), F32),
        grid=(DEPTH, n6 // tn),
        in_specs=[
            pl.BlockSpec((16, D_MODEL), lambda l, j: (0, 0)),
            pl.BlockSpec((None, D_MODEL, tn), lambda l, j: (l, 0, j)),
            pl.BlockSpec((None, 1, tn), lambda l, j: (l, 0, j)),
        ],
        out_specs=pl.BlockSpec((None, 16, tn), lambda l, j: (l, 0, j)),
        compiler_params=_cparams(("arbitrary", "arbitrary")),
        name="ada_mod",
    )(cond, ada_w, ada_b.reshape(DEPTH, 1, n6))


def _inproj_kernel(x_ref, mod_ref, nw_ref, w_ref, wh_ref, o_ref, u_ref, zh_ref):
    x = x_ref[...]
    ms = jnp.mean(x * x, axis=-1, keepdims=True)
    y = x * lax.rsqrt(ms + EPS) * nw_ref[...]
    y = (y * mod_ref[0:1, :] + mod_ref[1:2, :]).astype(BF16)
    p = jnp.dot(y, w_ref[...], preferred_element_type=F32)
    o_ref[...] = p
    for c in range(2):
        u_ref.at[c][pl.ds(pl.program_id(1), x.shape[0], stride=8), :] = (
            p[:, P_COLS + 128 * c:P_COLS + 128 * (c + 1)])
    zt = lax.dot_general(wh_ref[...], y, _NT, preferred_element_type=F32)
    for s in range(zh_ref.shape[0]):
        zh_ref[s] = zt[:, 128 * s:128 * (s + 1)]


def _inproj(x, mods, nw, w_bf16, wh_bf16, tm):
    b_, L, _ = x.shape
    return pl.pallas_call(
        _inproj_kernel,
        out_shape=(jax.ShapeDtypeStruct((b_, L, P_COLS + GROUP_W), F32),
                   jax.ShapeDtypeStruct((2, L * b_, 128), F32),
                   jax.ShapeDtypeStruct((b_, L // 128, 3 * HY_CH, 128), F32)),
        grid=(L // tm, b_),
        in_specs=[
            pl.BlockSpec((None, tm, D_MODEL), lambda i, b: (b, i, 0)),
            pl.BlockSpec((None, 8, D_MODEL), lambda i, b: (b, 0, 0)),
            pl.BlockSpec((1, D_MODEL), lambda i, b: (0, 0)),
            pl.BlockSpec((D_MODEL, P_COLS + GROUP_W), lambda i, b: (0, 0)),
            pl.BlockSpec((3 * HY_CH, D_MODEL), lambda i, b: (0, 0)),
        ],
        out_specs=(pl.BlockSpec((None, tm, P_COLS + GROUP_W), lambda i, b: (b, i, 0)),
                   pl.BlockSpec((2, tm * b_, 128), lambda i, b: (0, i, 0)),
                   pl.BlockSpec((None, tm // 128, 3 * HY_CH, 128), lambda i, b: (b, i, 0, 0))),
        compiler_params=_cparams(("arbitrary", "arbitrary")),
        name="in_proj",
    )(x, mods, nw.reshape(1, D_MODEL), w_bf16, wh_bf16)


def _half_swap(t):
    w = t.shape[-1]
    lane = lax.broadcasted_iota(jnp.int32, t.shape, t.ndim - 1)
    return jnp.where((lane % ATT_HD) < ATT_HD // 2,
                     pltpu.roll(t, w - ATT_HD // 2, t.ndim - 1),
                     pltpu.roll(t, ATT_HD // 2, t.ndim - 1))


def _rope(t, cos, sin):
    return t * cos + _half_swap(t) * sin


def _attend(q, keys, vals, valid, sink_ref):
    lane = lax.broadcasted_iota(jnp.int32, (1, 128), 1)
    lo = lane < ATT_HD
    kb = keys.astype(BF16)
    v_lo = jnp.where(lo, vals, 0.0).astype(BF16)
    v_hi = jnp.where(lo, 0.0, vals).astype(BF16)
    v_cat = jnp.concatenate([v_lo, v_hi], axis=0)
    outs = []
    for m in range(2):
        qm = q[:, 128 * m:128 * (m + 1)]
        probs = []
        for half in range(2):
            head = (0, 2, 1, 3)[2 * m + half]
            qh = jnp.where(lo if half == 0 else jnp.logical_not(lo), qm, 0.0).astype(BF16)
            s = lax.dot_general(qh, kb, (((1,), (1,)), ((), ())), preferred_element_type=F32)
            if valid is not None:
                s = jnp.where(valid, s, -jnp.inf)
            sink = sink_ref[head]
            mx = jnp.maximum(jnp.max(s, axis=-1, keepdims=True), sink)
            e = jnp.exp(s - mx)
            den = jnp.sum(e, axis=-1, keepdims=True) + jnp.exp(sink - mx)
            probs.append((e / den).astype(BF16))
        p_cat = jnp.concatenate(probs, axis=1)
        outs.append(jnp.dot(p_cat, v_cat, preferred_element_type=F32))
    return jnp.concatenate(outs, axis=1)


def _att_lat_kernel(sink_ref, q_ref, kp_ref, kc_ref, kn_ref, vp_ref, vc_ref, vn_ref,
                    kx_ref, vx_ref, cp_ref, cc_ref, cn_ref, sp_ref, sc_ref, sn_ref, o_ref,
                    *, seq_len):
    n = pl.program_id(1) * ATT_QB
    cos_c, sin_c = cc_ref[...], sc_ref[...]
    q = _rope(q_ref[...], jnp.concatenate([cos_c, cos_c], axis=1),
              jnp.concatenate([sin_c, sin_c], axis=1)) * ATT_SCALE
    keys = jnp.concatenate([
        kx_ref[...],
        _rope(kp_ref[...], cp_ref[...], sp_ref[...]),
        _rope(kc_ref[...], cos_c, sin_c),
        _rope(kn_ref[...], cn_ref[...], sn_ref[...])], axis=0)
    vals = jnp.concatenate([vx_ref[...], vp_ref[...], vc_ref[...], vn_ref[...]], axis=0)
    lc = kx_ref.shape[0]
    s_tot = lc + (ATT_QB + 2) * ATT_BLOCK
    col = lax.broadcasted_iota(jnp.int32, (ATT_QB * ATT_BLOCK, s_tot), 1)
    row = lax.broadcasted_iota(jnp.int32, (ATT_QB * ATT_BLOCK, s_tot), 0)
    qpos = n * ATT_BLOCK + row
    kpos = (n - 1) * ATT_BLOCK + (col - lc)
    valid = (col < lc) | ((jnp.abs(kpos - qpos) <= WINDOW) & (kpos >= 0) & (kpos < seq_len))
    o_ref[...] = _attend(q, keys, vals, valid, sink_ref)


def _att_lat(p_lat, p_ctx, cos_t, sin_t, sink):
    b_, L, _ = p_lat.shape
    lc = p_ctx.shape[1]
    nb = L // ATT_BLOCK
    kcol, vcol = COL_TK // 128, COL_TV // 128
    qrows = ATT_QB * ATT_BLOCK
    prev = lambda b, n: jnp.maximum(n * ATT_QB - 1, 0)
    nxt = lambda b, n: jnp.minimum((n + 1) * ATT_QB, nb - 1)
    edge = lambda c, f: pl.BlockSpec((None, ATT_BLOCK, 128), lambda b, n: (b, f(b, n), c))
    cur = lambda c: pl.BlockSpec((None, qrows, 128), lambda b, n: (b, n, c))
    tab_edge = lambda f: pl.BlockSpec((ATT_BLOCK, 128), lambda b, n: (f(b, n), 0))
    tab_cur = lambda: pl.BlockSpec((qrows, 128), lambda b, n: (n, 0))
    return pl.pallas_call(
        functools.partial(_att_lat_kernel, seq_len=L),
        out_shape=jax.ShapeDtypeStruct((b_, L, GROUP_W), F32),
        grid=(b_, nb // ATT_QB),
        in_specs=[
            pl.BlockSpec(memory_space=pltpu.SMEM),
            pl.BlockSpec((None, qrows, 256), lambda b, n: (b, n, COL_TQ // 256)),
            edge(kcol, prev), cur(kcol), edge(kcol, nxt),
            edge(vcol, prev), cur(vcol), edge(vcol, nxt),
            pl.BlockSpec((None, lc, 128), lambda b, n: (b, 0, kcol)),
            pl.BlockSpec((None, lc, 128), lambda b, n: (b, 0, vcol)),
            tab_edge(prev), tab_cur(), tab_edge(nxt), tab_edge(prev), tab_cur(), tab_edge(nxt),
        ],
        out_specs=pl.BlockSpec((None, qrows, GROUP_W), lambda b, n: (b, n, 0)),
        compiler_params=_cparams(("arbitrary", "arbitrary")),
        name="att_latent",
    )(sink, p_lat, p_lat, p_lat, p_lat, p_lat, p_lat, p_lat, p_ctx, p_ctx,
      cos_t, cos_t, cos_t, sin_t, sin_t, sin_t)


def _att_ctx_kernel(sink_ref, q_ref, k_ref, v_ref, o_ref):
    o_ref[...] = _attend(q_ref[...] * ATT_SCALE, k_ref[...], v_ref[...], None, sink_ref)


def _att_ctx(p_ctx, sink):
    b_, lc, _ = p_ctx.shape
    return pl.pallas_call(
        _att_ctx_kernel,
        out_shape=jax.ShapeDtypeStruct((b_, lc, GROUP_W), F32),
        grid=(b_,),
        in_specs=[
            pl.BlockSpec(memory_space=pltpu.SMEM),
            pl.BlockSpec((None, lc, 256), lambda b: (b, 0, COL_TQ // 256)),
            pl.BlockSpec((None, lc, 128), lambda b: (b, 0, COL_TK // 128)),
            pl.BlockSpec((None, lc, 128), lambda b: (b, 0, COL_TV // 128)),
        ],
        out_specs=pl.BlockSpec((None, lc, GROUP_W), lambda b: (b, 0, 0)),
        compiler_params=_cparams(("arbitrary",)),
        name="att_context",
    )(sink, p_ctx, p_ctx, p_ctx)


HG_T = 128
HG_SUB = 4
HG_LEVELS = (64, 32, 16)
_NT = (((1,), (1,)), ((), ()))
_TN = (((0,), (0,)), ((), ()))


def _row_fill(ref, rows, blk):
    w = ref.shape[-1]
    return jnp.concatenate([jnp.broadcast_to(ref[r:r + 1, :], (blk, w)) for r in rows], axis=0)


def _gla_block(q_raw, z, v, lb, st_ref, cum_scr, rev):
    t = HG_T
    e = jnp.exp(-jnp.abs(z))
    big = 1.0 / (1.0 + e)
    small = e * big
    sg = jnp.where(z >= 0, big, small)
    g = jnp.log(lb + (1.0 - lb) * sg)
    k = (1.0 - lb) * jnp.where(z >= 0, small, big)
    q = q_raw * jax.nn.sigmoid(q_raw)

    g_hi = g.astype(BF16)
    r1 = g - g_hi.astype(F32)
    g_mid = r1.astype(BF16)
    g_lo = (r1 - g_mid.astype(F32)).astype(BF16)
    ti = lax.broadcasted_iota(jnp.int32, (t, t), 0)
    si = lax.broadcasted_iota(jnp.int32, (t, t), 1)
    tri = jnp.where((si >= ti) if rev else (si <= ti), 1.0, 0.0).astype(BF16)
    c3 = jnp.dot(tri, jnp.concatenate([g_hi, g_mid, g_lo], axis=1), preferred_element_type=F32)
    cum = c3[:, 0:256] + c3[:, 256:512] + c3[:, 512:768]
    cum_scr[...] = cum
    edge = 0 if rev else t - 1
    g_tot = cum_scr[edge:edge + 1, :]

    lane = lax.broadcasted_iota(jnp.int32, (1, 128), 1)
    lo = lane < HG_DK
    rowi = lax.broadcasted_iota(jnp.int32, (t, 1), 0)
    vi = lax.broadcasted_iota(jnp.int32, (128, 128), 0)
    ki = lax.broadcasted_iota(jnp.int32, (128, 128), 1)
    same_head = (vi < HG_DV) == (ki < HG_DK)

    lo_b = jnp.where(lo, 1.0, 0.0).astype(BF16)
    hi_b = jnp.where(lo, 0.0, 1.0).astype(BF16)

    def scores(qt, kt):
        q2 = jnp.concatenate([qt * lo_b, qt * hi_b], axis=0)
        return lax.dot_general(q2, kt, _NT, preferred_element_type=F32)

    outs = []
    for pr in range(2):
        sl = slice(128 * pr, 128 * (pr + 1))
        qp, kp, vp, cp, gp = q[:, sl], k[:, sl], v[:, sl], cum[:, sl], g_tot[:, sl]
        qb, kb = qp.astype(BF16), kp.astype(BF16)
        a0 = jnp.zeros((t, t), F32)
        a1 = jnp.zeros((t, t), F32)
        for sh, h in zip((6, 5, 4), HG_LEVELS):
            rows = [b * 2 * h + (h if rev else h - 1) for b in range(t // (2 * h))]
            e = jnp.exp(-jnp.abs(cp - _row_fill(cum_scr.at[:, sl], rows, 2 * h)))
            odd = (jnp.right_shift(rowi, sh) & 1) == 1
            q_on = jnp.logical_not(odd) if rev else odd
            sc = scores(qb * jnp.where(q_on, e, 0.0).astype(BF16),
                        kb * jnp.where(q_on, 0.0, e).astype(BF16))
            if 2 * h < t:
                keep = jnp.right_shift(ti, sh + 1) == jnp.right_shift(si, sh + 1)
                a0 += jnp.where(keep, sc[:t], 0.0)
                a1 += jnp.where(keep, sc[t:], 0.0)
            else:
                a0 += sc[:t]
                a1 += sc[t:]
        rows = [b * HG_CHUNK + (HG_CHUNK // 2 if rev else HG_CHUNK // 2 - 1) for b in range(t // HG_CHUNK)]
        dd = cp - _row_fill(cum_scr.at[:, sl], rows, HG_CHUNK)
        sc = scores((qp * jnp.exp(dd)).astype(BF16), (kp * jnp.exp(-dd)).astype(BF16))
        keep = (jnp.right_shift(ti, 4) == jnp.right_shift(si, 4)) & ((si >= ti) if rev else (si <= ti))
        a0 += jnp.where(keep, sc[:t], 0.0)
        a1 += jnp.where(keep, sc[t:], 0.0)

        v2 = jnp.concatenate([jnp.where(lo, vp, 0.0), jnp.where(lo, 0.0, vp)], axis=0).astype(BF16)
        o = jnp.dot(jnp.concatenate([a0, a1], axis=1).astype(BF16), v2, preferred_element_type=F32)
        st = st_ref[pr]
        o += lax.dot_general((qp * jnp.exp(cp)).astype(BF16), st.astype(BF16), _NT,
                             preferred_element_type=F32)
        upd = lax.dot_general(vp.astype(BF16), (kp * jnp.exp(gp - cp)).astype(BF16), _TN,
                              preferred_element_type=F32)
        st_ref[pr] = st * jnp.exp(gp) + jnp.where(same_head, upd, 0.0)
        outs.append(o)
    return jnp.concatenate(outs, axis=1)


def _hgrn2_kernel(qf_ref, zf_ref, vf_ref, qb_ref, zb_ref, vb_ref, lb_ref, s0_ref,
                  of_ref, ob_ref, sl_ref, st_f, st_b, cum_scr):
    j = pl.program_id(1)

    @pl.when(j == 0)
    def _():
        st_f[...] = s0_ref[0]
        st_b[...] = s0_ref[1]

    sub = qf_ref.shape[0] // HG_T
    for i in range(sub):
        rf = slice(HG_T * i, HG_T * (i + 1))
        of_ref[rf, :] = _gla_block(qf_ref[rf, :], zf_ref[rf, :], vf_ref[rf, :], lb_ref[0:1, :],
                                   st_f, cum_scr.at[0, i], False)
        rb = slice(HG_T * (sub - 1 - i), HG_T * (sub - i))
        ob_ref[rb, :] = _gla_block(qb_ref[rb, :], zb_ref[rb, :], vb_ref[rb, :], lb_ref[1:2, :],
                                   st_b, cum_scr.at[1, i], True)
    sl_ref[0] = st_f[...]
    sl_ref[1] = st_b[...]


def _hgrn2(p, lb, s0):
    b_, L, _ = p.shape
    sub = min(HG_SUB, L // HG_T)
    rows = HG_T * sub
    n = L // rows
    fwd = lambda c: pl.BlockSpec((None, rows, GROUP_W), lambda b, j: (b, j, c))
    bwd = lambda c: pl.BlockSpec((None, rows, GROUP_W), lambda b, j: (b, n - 1 - j, c))
    st_spec = pl.BlockSpec((None, 2, 2, 128, 128), lambda b, j: (b, 0, 0, 0, 0))
    o_shape = jax.ShapeDtypeStruct((b_, L, GROUP_W), F32)
    return pl.pallas_call(
        _hgrn2_kernel,
        out_shape=(o_shape, o_shape, jax.ShapeDtypeStruct((b_, 2, 2, 128, 128), F32)),
        grid=(b_, n),
        in_specs=[fwd(0), fwd(1), fwd(3), bwd(0), bwd(2), bwd(3),
                  pl.BlockSpec((2, GROUP_W), lambda b, j: (0, 0)), st_spec],
        out_specs=(fwd(0), bwd(0), st_spec),
        scratch_shapes=[pltpu.VMEM((2, 128, 128), F32), pltpu.VMEM((2, 128, 128), F32),
                        pltpu.VMEM((2, sub, HG_T, GROUP_W), F32)],
        compiler_params=_cparams(("arbitrary", "arbitrary")),
        name="hgrn2",
    )(p, p, p, p, p, p, lb, s0)


S5_TT = 128
S5_NS = S5_GROUPS * S5_STATE
S5_LANES = 256


def _s5_kernel(uf_ref, ub_ref, wb_ref, wc_ref, a_ref, x0_ref, yf_ref, yb_ref, xl_ref,
               buf_f, buf_b, st, y_scr):
    j = pl.program_id(0)

    @pl.when(j == 0)
    def _():
        st[...] = x0_ref[...]

    halves = lambda r: jnp.concatenate([r[0], r[1]], axis=1).astype(BF16)
    buf_f[...] = jnp.dot(halves(uf_ref), wb_ref[0], preferred_element_type=F32)
    buf_b[...] = jnp.dot(halves(ub_ref), wb_ref[1], preferred_element_type=F32)

    for cc in range(S5_NS // S5_LANES):
        re = slice(cc * S5_LANES, (cc + 1) * S5_LANES)
        im = slice(S5_NS + cc * S5_LANES, S5_NS + (cc + 1) * S5_LANES)
        arf, aif, arb, aib = a_ref[0, :, re], a_ref[1, :, re], a_ref[2, :, re], a_ref[3, :, re]

        def step(t, carry, re=re, im=im, arf=arf, aif=aif, arb=arb, aib=aib):
            xrf, xif, xrb, xib = carry
            rf = pl.multiple_of(t * 8, 8)
            rb = pl.multiple_of((S5_TT - 1 - t) * 8, 8)
            nrf = arf * xrf - aif * xif + buf_f[pl.ds(rf, 8), re]
            nif = arf * xif + aif * xrf + buf_f[pl.ds(rf, 8), im]
            nrb = arb * xrb - aib * xib + buf_b[pl.ds(rb, 8), re]
            nib = arb * xib + aib * xrb + buf_b[pl.ds(rb, 8), im]
            buf_f[pl.ds(rf, 8), re] = nrf
            buf_f[pl.ds(rf, 8), im] = nif
            buf_b[pl.ds(rb, 8), re] = nrb
            buf_b[pl.ds(rb, 8), im] = nib
            return nrf, nif, nrb, nib

        fin = lax.fori_loop(0, S5_TT, step,
                            (st[0, :, re], st[1, :, re], st[2, :, re], st[3, :, re]), unroll=4)
        for k in range(4):
            st[k, :, re] = fin[k]

    for y_ref, buf, d in ((yf_ref, buf_f, 0), (yb_ref, buf_b, 1)):
        y = jnp.dot(buf[...].astype(BF16), wc_ref[d], preferred_element_type=F32)
        for c in range(2):
            y_scr[c] = y[:, 128 * c:128 * (c + 1)]
        for b in range(8):
            for c in range(2):
                y_ref[b, :, 128 * c:128 * (c + 1)] = y_scr.at[c][pl.ds(b, S5_TT, stride=8), :]
    xl_ref[...] = st[...]


def _s5_scan(u2, wb, wc, a_bc, x0):
    L = u2.shape[1] // 8
    rows = S5_TT * 8
    n = L // S5_TT
    y_shape = jax.ShapeDtypeStruct((8, L, GROUP_W), F32)
    return pl.pallas_call(
        _s5_kernel,
        out_shape=(y_shape, y_shape, jax.ShapeDtypeStruct((4, 8, S5_NS), F32)),
        grid=(n,),
        in_specs=[
            pl.BlockSpec((2, rows, 128), lambda j: (0, j, 0)),
            pl.BlockSpec((2, rows, 128), lambda j: (0, n - 1 - j, 0)),
            pl.BlockSpec((2, GROUP_W, 2 * S5_NS), lambda j: (0, 0, 0)),
            pl.BlockSpec((2, 2 * S5_NS, GROUP_W), lambda j: (0, 0, 0)),
            pl.BlockSpec((4, 8, S5_NS), lambda j: (0, 0, 0)),
            pl.BlockSpec((4, 8, S5_NS), lambda j: (0, 0, 0)),
        ],
        out_specs=(pl.BlockSpec((8, S5_TT, GROUP_W), lambda j: (0, j, 0)),
                   pl.BlockSpec((8, S5_TT, GROUP_W), lambda j: (0, n - 1 - j, 0)),
                   pl.BlockSpec((4, 8, S5_NS), lambda j: (0, 0, 0))),
        scratch_shapes=[pltpu.VMEM((rows, 2 * S5_NS), F32),
                        pltpu.VMEM((rows, 2 * S5_NS), F32),
                        pltpu.VMEM((4, 8, S5_NS), F32),
                        pltpu.VMEM((2, rows, 128), F32)],
        compiler_params=_cparams(("arbitrary",)),
        name="s5_scan",
    )(u2, u2, wb, wc, a_bc, x0)


def _s5_weights(a_re, a_im, log_step, b_re, b_im, c_re, c_im):
    eye = jnp.eye(S5_GROUPS, dtype=F32)
    wbs, wcs, abc = [], [], []
    for d in range(2):
        ab_re, ab_im, bb_re, bb_im = _s5_discretize(a_re[d], a_im[d], log_step[d], b_re, b_im)
        wb_re = jnp.einsum('gph,gk->ghkp', bb_re, eye).reshape(GROUP_W, S5_NS)
        wb_im = jnp.einsum('gph,gk->ghkp', bb_im, eye).reshape(GROUP_W, S5_NS)
        wbs.append(jnp.concatenate([wb_re, wb_im], axis=1))
        wc_re = jnp.einsum('ghp,gk->gpkh', c_re[d], eye).reshape(S5_NS, GROUP_W)
        wc_im = jnp.einsum('ghp,gk->gpkh', c_im[d], eye).reshape(S5_NS, GROUP_W)
        wcs.append(jnp.concatenate([wc_re, -wc_im], axis=0))
        abc += [jnp.broadcast_to(ab_re.reshape(1, S5_NS), (8, S5_NS)),
                jnp.broadcast_to(ab_im.reshape(1, S5_NS), (8, S5_NS))]
    return jnp.stack(wbs).astype(BF16), jnp.stack(wcs).astype(BF16), jnp.stack(abc)


def _group_norm(m, w):
    return m * lax.rsqrt(jnp.mean(m * m, axis=-1, keepdims=True) + EPS) * w


def _merge_kernel(hgf_ref, hgb_ref, yf_ref, yb_ref, u_ref, sd_ref, glu_ref, hy_ref, at_ref, gate_ref,
                  x_ref, mod_ref, mw_ref, fw_ref, wo_ref, xo_ref, ho_ref):
    gate = gate_ref[...]
    z = jax.nn.gelu(yf_ref[...] + yb_ref[...] + sd_ref[...] * u_ref[...])
    s5 = z * jax.nn.sigmoid(jnp.dot(z.astype(BF16), glu_ref[...], preferred_element_type=F32))
    parts = [
        _group_norm(hgf_ref[...] + hgb_ref[...], mw_ref[:, 0:256]) * (gate * jax.nn.sigmoid(gate)),
        _group_norm(s5, mw_ref[:, 256:512]),
        _group_norm(jnp.concatenate([hy_ref[s].T for s in range(hy_ref.shape[0])], axis=0),
                    mw_ref[:, 512:768]),
        _group_norm(at_ref[...], mw_ref[:, 768:1024]),
    ]
    mix = jnp.concatenate(parts, axis=1).astype(BF16)
    xn = x_ref[...] + mod_ref[2:3, :] * jnp.dot(mix, wo_ref[...], preferred_element_type=F32)
    xo_ref[...] = xn
    h = xn * lax.rsqrt(jnp.mean(xn * xn, axis=-1, keepdims=True) + EPS) * fw_ref[...]
    ho_ref[...] = (h * mod_ref[3:4, :] + mod_ref[4:5, :]).astype(BF16)


def _merge(hg_f, hg_b, s5_yf, s5_yb, s5_d, glu_bf16, hy, at, p, x, mods, mw, fw, wo_bf16, tm):
    b_, L, _ = x.shape
    grp = lambda: pl.BlockSpec((None, tm, GROUP_W), lambda i, b: (b, i, 0))
    pcol = lambda c: pl.BlockSpec((None, tm, GROUP_W), lambda i, b: (b, i, c))
    return pl.pallas_call(
        _merge_kernel,
        out_shape=(jax.ShapeDtypeStruct((b_, L, D_MODEL), F32),
                   jax.ShapeDtypeStruct((b_, L, D_MODEL), BF16)),
        grid=(L // tm, b_),
        in_specs=[
            grp(), grp(), grp(), grp(), pcol(P_COLS // GROUP_W),
            pl.BlockSpec((1, GROUP_W), lambda i, b: (0, 0)),
            pl.BlockSpec((GROUP_W, GROUP_W), lambda i, b: (0, 0)),
            pl.BlockSpec((None, tm // 128, HY_CH, 128), lambda i, b: (b, i, 0, 0)),
            grp(),
            pl.BlockSpec((None, tm, GROUP_W), lambda i, b: (b, i, 4)),
            pl.BlockSpec((None, tm, D_MODEL), lambda i, b: (b, i, 0)),
            pl.BlockSpec((None, 8, D_MODEL), lambda i, b: (b, 0, 0)),
            pl.BlockSpec((1, D_MODEL), lambda i, b: (0, 0)),
            pl.BlockSpec((1, D_MODEL), lambda i, b: (0, 0)),
            pl.BlockSpec((D_MODEL, D_MODEL), lambda i, b: (0, 0)),
        ],
        out_specs=(pl.BlockSpec((None, tm, D_MODEL), lambda i, b: (b, i, 0)),
                   pl.BlockSpec((None, tm, D_MODEL), lambda i, b: (b, i, 0))),
        compiler_params=_cparams(("arbitrary", "arbitrary")),
        name="merge_out_proj",
    )(hg_f, hg_b, s5_yf, s5_yb, p, s5_d.reshape(1, GROUP_W), glu_bf16, hy, at, p, x, mods,
      mw.reshape(1, D_MODEL), fw.reshape(1, D_MODEL), wo_bf16)


def _ffn_kernel(hp_ref, h_ref, hn_ref, x_ref, mod_ref, wu_ref, cw_ref, wd_ref, fw_ref, o_ref,
                acc_scr, *, tm, final_norm):
    i = pl.program_id(1)
    last = pl.num_programs(1) - 1
    h = h_ref[...]
    hp = jnp.where(i > 0, hp_ref[...], jnp.zeros_like(hp_ref))
    hn = jnp.where(i < last, hn_ref[...], jnp.zeros_like(hn_ref))
    h_ext = jnp.concatenate([hp, h, hn], axis=0)
    rows = tm + 2 * HALO

    def gated(j):
        cols = slice(FF_CHUNK * j, FF_CHUNK * (j + 1))
        a = jnp.dot(h_ext, wu_ref[:, cols], preferred_element_type=F32)
        v = jnp.dot(h, wu_ref[:, D_FF + FF_CHUNK * j:D_FF + FF_CHUNK * (j + 1)],
                    preferred_element_type=F32)
        cw = cw_ref[:, cols]
        conv = (pltpu.roll(a, 1, 0)[HALO:HALO + tm] * cw[0:1, :]
                + a[HALO:HALO + tm] * cw[1:2, :]
                + pltpu.roll(a, rows - 1, 0)[HALO:HALO + tm] * cw[2:3, :] + cw[3:4, :])
        return (conv * jax.nn.sigmoid(conv) * v).astype(BF16)

    for j in range(0, N_FF_CHUNKS, 2):
        n = min(2, N_FF_CHUNKS - j)
        g = jnp.concatenate([gated(j + d) for d in range(n)], axis=1)
        part = jnp.dot(g, wd_ref[FF_CHUNK * j:FF_CHUNK * (j + n), :], preferred_element_type=F32)
        if j == 0:
            acc_scr[...] = part
        elif j + n < N_FF_CHUNKS:
            acc_scr[...] += part
        else:
            xo = x_ref[...] + mod_ref[5:6, :] * (acc_scr[...] + part)
            if final_norm:
                xo = xo * lax.rsqrt(jnp.mean(xo * xo, axis=-1, keepdims=True) + EPS) * fw_ref[...]
            o_ref[...] = xo


def _ffn(h, x, mods, w_up, cw, w_down, layer, tm, final_w=None):
    b_, L, _ = x.shape
    nh = L // HALO
    r = tm // HALO
    fw = jnp.ones((1, D_MODEL), F32) if final_w is None else final_w.reshape(1, D_MODEL)
    return pl.pallas_call(
        functools.partial(_ffn_kernel, tm=tm, final_norm=final_w is not None),
        out_shape=jax.ShapeDtypeStruct((b_, L, D_MODEL), F32),
        grid=(b_, L // tm),
        in_specs=[
            pl.BlockSpec((None, HALO, D_MODEL), lambda b, i: (b, jnp.maximum(i * r - 1, 0), 0)),
            pl.BlockSpec((None, tm, D_MODEL), lambda b, i: (b, i, 0)),
            pl.BlockSpec((None, HALO, D_MODEL), lambda b, i: (b, jnp.minimum((i + 1) * r, nh - 1), 0)),
            pl.BlockSpec((None, tm, D_MODEL), lambda b, i: (b, i, 0)),
            pl.BlockSpec((None, 8, D_MODEL), lambda b, i: (b, 0, 0)),
            pl.BlockSpec((None, D_MODEL, 2 * D_FF), lambda b, i: (layer, 0, 0),
                         pipeline_mode=pl.Buffered(1)),
            pl.BlockSpec((8, D_FF), lambda b, i: (0, 0)),
            pl.BlockSpec((None, D_FF, D_MODEL), lambda b, i: (layer, 0, 0),
                         pipeline_mode=pl.Buffered(1)),
            pl.BlockSpec((1, D_MODEL), lambda b, i: (0, 0)),
        ],
        out_specs=pl.BlockSpec((None, tm, D_MODEL), lambda b, i: (b, i, 0)),
        scratch_shapes=[pltpu.VMEM((tm, D_MODEL), F32)],
        compiler_params=_cparams(("arbitrary", "arbitrary")),
        name="conv_ffn",
    )(h, h, h, x, mods, w_up, cw, w_down, fw)


def _s5_discretize(a_re, a_im, log_step, b_re, b_im):
    dt = jnp.exp(log_step)[:, None]
    mag = jnp.exp(a_re * dt)
    ang = a_im * dt
    ab_re, ab_im = mag * jnp.cos(ang), mag * jnp.sin(ang)
    den = a_re * a_re + a_im * a_im
    nr, ni = ab_re - 1.0, ab_im
    fr = (nr * a_re + ni * a_im) / den
    fi = (ni * a_re - nr * a_im) / den
    bb_re = fr[..., None] * b_re - fi[..., None] * b_im
    bb_im = fr[..., None] * b_im + fi[..., None] * b_re
    return ab_re, ab_im, bb_re, bb_im


HY_LANES = 128
HY_CB_LATENT = 16
HY_CB_CONTEXT = 64
HY_EMB_PAD = 128


def _vadd(a, b):
    return b if a is None else a if b is None else a + b


def _vsub(a, b):
    return (None if b is None else -b) if a is None else a if b is None else a - b


def _vscale(a, c):
    return None if (a is None or c == 0.0) else a if c == 1.0 else -a if c == -1.0 else a * c


def _cmul_const(x, wr, wi):
    re, im = x
    return (_vsub(_vscale(re, wr), _vscale(im, wi)), _vadd(_vscale(im, wr), _vscale(re, wi)))


def _unit_root(k, n, sign):
    k %= n
    if (4 * k) % n == 0:
        return ((1.0, 0.0), (0.0, float(sign)), (-1.0, 0.0), (0.0, float(-sign)))[4 * k // n]
    ang = sign * 2.0 * math.pi * k / n
    return math.cos(ang), math.sin(ang)


def _fft_slabs(x, sign, n_out=None):
    n = len(x)
    if n == 1:
        return list(x)
    n_out = n if n_out is None else n_out
    ev = _fft_slabs(x[0::2], sign)
    od = _fft_slabs(x[1::2], sign)
    out = [None] * n_out
    for k in range(n // 2):
        t = _cmul_const(od[k], *_unit_root(k, n, sign))
        if k < n_out:
            out[k] = (_vadd(ev[k][0], t[0]), _vadd(ev[k][1], t[1]))
        if k + n // 2 < n_out:
            out[k + n // 2] = (_vsub(ev[k][0], t[0]), _vsub(ev[k][1], t[1]))
    return out


def _twiddle(slabs, tw_ref, conj):
    out = [slabs[0]]
    for k in range(1, len(slabs)):
        re, im = slabs[k]
        tr = tw_ref[0, k]
        ti = -tw_ref[1, k] if conj else tw_ref[1, k]
        if im is None:
            out.append((re * tr, re * ti))
        else:
            out.append((re * tr - im * ti, re * ti + im * tr))
    return out


def _dot_split(x, w_ref):
    hi = x.astype(BF16)
    lo = (x - hi.astype(F32)).astype(BF16)
    return (jnp.dot(hi, w_ref[0], preferred_element_type=F32)
            + jnp.dot(lo, w_ref[0], preferred_element_type=F32)
            + jnp.dot(hi, w_ref[1], preferred_element_type=F32))


def _lane_dft(slabs, w_ref, cb):
    zero = jnp.zeros((cb, HY_LANES), F32)
    rows = jnp.concatenate(
        [jnp.concatenate([zero if re is None else re, zero if im is None else im], axis=1)
         for re, im in slabs], axis=0)
    out = _dot_split(rows, w_ref)
    return [(out[cb * i:cb * (i + 1), :HY_LANES], out[cb * i:cb * (i + 1), HY_LANES:])
            for i in range(len(slabs))]


def _short_conv(ref, part, cw_ref, ch, lane):
    n = ref.shape[1]
    w0, w1, w2, bb = cw_ref[ch, 0], cw_ref[ch, 1], cw_ref[ch, 2], cw_ref[ch, 3]
    right = [pltpu.roll(ref[part, s], 1, 1) for s in range(n)]
    left = [pltpu.roll(ref[part, s], HY_LANES - 1, 1) for s in range(n)]
    zero = jnp.zeros(lane.shape, F32)
    out = []
    for s in range(n):
        xm = jnp.where(lane == 0, right[s - 1] if s > 0 else zero, right[s])
        xp = jnp.where(lane == HY_LANES - 1, left[s + 1] if s < n - 1 else zero, left[s])
        out.append(xm * w0 + ref[part, s] * w1 + xp * w2 + bb)
    return out


def _hy_conv_kernel(x1_ref, x2_ref, z_ref, cw_ref, bias_ref, kf_ref, tw_ref, wf_ref, wi_ref, o_ref,
                    *, n1):
    nh = n1 // 2
    cb = z_ref.shape[2]
    lane = lax.broadcasted_iota(jnp.int32, (cb, HY_LANES), 1)
    gates = [[_short_conv(r, part, cw_ref, ch, lane) for part in range(2)]
             for ch, r in ((0, x1_ref), (1, x2_ref))]
    z = [_short_conv(z_ref, part, cw_ref, 2, lane) for part in range(2)]
    for o in range(HY_ORDER):
        spec = _fft_slabs([(z[0][s], z[1][s]) for s in range(nh)] + [(None, None)] * nh, -1)
        spec = _lane_dft(_twiddle(spec, tw_ref, False), wf_ref, cb)
        spec = [(re * kf_ref[o, 0, k] - im * kf_ref[o, 1, k], re * kf_ref[o, 1, k] + im * kf_ref[o, 0, k])
                for k, (re, im) in enumerate(spec)]
        spec = _twiddle(_lane_dft(spec, wi_ref, cb), tw_ref, True)
        y = _fft_slabs(spec, 1, n_out=nh)
        bo = bias_ref[o]
        z = [[gates[o][part][s] * (y[s][part] + bo * z[part][s]) for s in range(nh)]
             for part in range(2)]
    for part in range(2):
        for s in range(nh):
            o_ref[part, s] = z[part][s]


def _hy_conv(zh, cw, bias_b, kf, tw, wf, wi):
    b_, nh = zh.shape[0], zh.shape[1]
    cb = tw.shape[2]
    half, ncb = b_ // 2, HY_CH // cb
    n1 = 2 * nh
    zspec = lambda ch: pl.BlockSpec((2, None, nh, cb, HY_LANES),
                                    lambda c, b: (0, b, 0, ch * ncb + c, 0))
    const3 = lambda c, b: (0, 0, 0)
    z5 = zh.reshape(2, half, nh, 3 * HY_CH, HY_LANES)
    out = pl.pallas_call(
        functools.partial(_hy_conv_kernel, n1=n1),
        out_shape=jax.ShapeDtypeStruct((2, half, nh, HY_CH, HY_LANES), F32),
        grid=(ncb, half),
        in_specs=[
            zspec(0), zspec(1), zspec(2),
            pl.BlockSpec((3, 4, cb, HY_LANES), lambda c, b: (0, 0, c, 0)),
            pl.BlockSpec((HY_ORDER, cb, HY_LANES), lambda c, b: (0, c, 0)),
            pl.BlockSpec((HY_ORDER, 2, None, n1, cb, HY_LANES), lambda c, b: (0, 0, c, 0, 0, 0)),
            pl.BlockSpec((2, n1, cb, HY_LANES), lambda c, b: (0, 0, 0, 0)),
            pl.BlockSpec((2, 256, 256), const3),
            pl.BlockSpec((2, 256, 256), const3),
        ],
        out_specs=pl.BlockSpec((2, None, nh, cb, HY_LANES), lambda c, b: (0, b, 0, c, 0)),
        compiler_params=_cparams(("arbitrary", "arbitrary")),
        name="hyena_conv",
    )(z5, z5, z5, cw, bias_b, kf, tw, wf, wi)
    return out.reshape(b_, nh, HY_CH, HY_LANES)


def _hy_spectrum_kernel(k_ref, tw_ref, wf_ref, o_ref, *, n1):
    spec = _fft_slabs([(k_ref[s], None) for s in range(n1)], -1)
    spec = _lane_dft(_twiddle(spec, tw_ref, False), wf_ref, k_ref.shape[1])
    scale = 1.0 / (n1 * HY_LANES)
    for k in range(n1):
        o_ref[0, k] = spec[k][0] * scale
        o_ref[1, k] = spec[k][1] * scale


def _hy_spectrum(taps, tw, wf):
    n1, cb = taps.shape[1], tw.shape[2]
    return pl.pallas_call(
        functools.partial(_hy_spectrum_kernel, n1=n1),
        out_shape=jax.ShapeDtypeStruct((HY_ORDER, 2, HY_CH // cb, n1, cb, HY_LANES), F32),
        grid=(HY_ORDER, HY_CH // cb),
        in_specs=[
            pl.BlockSpec((None, n1, cb, HY_LANES), lambda o, c: (o, 0, c, 0)),
            pl.BlockSpec((2, n1, cb, HY_LANES), lambda o, c: (0, 0, 0, 0)),
            pl.BlockSpec((2, 256, 256), lambda o, c: (0, 0, 0)),
        ],
        out_specs=pl.BlockSpec((None, 2, None, n1, cb, HY_LANES), lambda o, c: (o, 0, c, 0, 0, 0)),
        compiler_params=_cparams(("arbitrary", "arbitrary")),
        name="hyena_spectrum",
    )(taps, tw, wf)


def _dot_f32(a, b, dims=(((1,), (0,)), ((), ()))):
    a_hi = a.astype(BF16)
    a_lo = (a - a_hi.astype(F32)).astype(BF16)
    b_hi = b.astype(BF16)
    b_lo = (b - b_hi.astype(F32)).astype(BF16)
    dot = lambda p, q: lax.dot_general(p, q, dims, preferred_element_type=F32)
    return dot(a_hi, b_hi) + dot(a_lo, b_hi) + dot(a_hi, b_lo)


def _hy_mlp_kernel(emb_ref, w1_ref, w2_ref, w3t_ref, vec_ref, pos_ref, dl_ref, o_ref):
    z = emb_ref[...]
    h = jnp.sin(vec_ref[2:3, :] * (_dot_f32(z, w1_ref[...]) + vec_ref[0:1, :]))
    h = jnp.sin(vec_ref[3:4, :] * (_dot_f32(h, w2_ref[...]) + vec_ref[1:2, :]))
    ht = _dot_f32(w3t_ref[...], h, _NT)
    dl = dl_ref[...]
    for s in range(o_ref.shape[1]):
        lanes = slice(HY_LANES * s, HY_LANES * (s + 1))
        win = jnp.exp(-pos_ref[0:1, lanes] * dl)
        wf, wb = pos_ref[1:2, lanes] * win, pos_ref[2:3, lanes] * win
        for o in range(HY_ORDER):
            base = 2 * HY_CH * o
            o_ref[o, s] = (ht[base:base + HY_CH, lanes] * wf
                           + ht[base + HY_CH:base + 2 * HY_CH, lanes] * wb)


def _hy_filter_taps(L, w1, b1, freq, w2, b2, w3):
    pad = HY_EMB_PAD
    n = 2 * L
    t01 = np.linspace(0.0, 1.0, L, dtype=np.float32)[:, None]
    w = (np.float32(2.0 * math.pi) * np.arange(L, dtype=np.float32)[:, None]) / np.float32(L)
    bands = (HY_EMB - 1) // 2
    fr = np.linspace(1e-4, bands - 1, bands, dtype=np.float32)[None, :]
    arg = (fr * w).astype(np.float64)
    emb = np.zeros((L, pad), np.float32)
    emb[:, 0:1] = t01
    emb[:, 1:1 + bands] = np.cos(arg)
    emb[:, 1 + bands:HY_EMB] = -np.sin(arg)
    pos_idx = np.arange(n)
    lag = np.where(pos_idx < L, pos_idx, (n - pos_idx) % L)
    pos = np.zeros((8, n), np.float32)
    pos[0] = t01[lag, 0]
    pos[1] = pos_idx < L
    pos[2] = (pos_idx > L) | (pos_idx == 0)
    hid = w1.shape[1]
    n_out = w3.shape[1]
    w1p = jnp.zeros((pad, pad), F32).at[:HY_EMB, :hid].set(w1)
    w2p = jnp.zeros((pad, pad), F32).at[:hid, :hid].set(w2)
    w3t = jnp.zeros((n_out, pad), F32).at[:, :hid].set(w3.T)
    vec = jnp.zeros((8, pad), F32).at[0, :hid].set(b1).at[1, :hid].set(b2)
    vec = vec.at[2, :hid].set(freq[0]).at[3, :hid].set(freq[1])
    deltas = np.abs(np.linspace(HY_MIN_DECAY, HY_MAX_DECAY, HY_CH, dtype=np.float32))
    dl = np.broadcast_to(deltas[:, None], (HY_CH, HY_LANES))
    tl = min(n, 512)
    return pl.pallas_call(
        _hy_mlp_kernel,
        out_shape=jax.ShapeDtypeStruct((HY_ORDER, n // HY_LANES, HY_CH, HY_LANES), F32),
        grid=(n // tl,),
        in_specs=[
            pl.BlockSpec((tl, pad), lambda i: (i, 0)),
            pl.BlockSpec((pad, pad), lambda i: (0, 0)),
            pl.BlockSpec((pad, pad), lambda i: (0, 0)),
            pl.BlockSpec((n_out, pad), lambda i: (0, 0)),
            pl.BlockSpec((8, pad), lambda i: (0, 0)),
            pl.BlockSpec((8, tl), lambda i: (0, i)),
            pl.BlockSpec((HY_CH, HY_LANES), lambda i: (0, 0)),
        ],
        out_specs=pl.BlockSpec((HY_ORDER, tl // HY_LANES, HY_CH, HY_LANES), lambda i: (0, i, 0, 0)),
        compiler_params=_cparams(("arbitrary",)),
        name="hyena_filter_mlp",
    )(jnp.asarray(emb[lag]), w1p, w2p, w3t, vec, jnp.asarray(pos), jnp.asarray(dl))


def _hy_constants(n1, cb):
    n = n1 * HY_LANES
    ang = -2.0 * np.pi * np.outer(np.arange(n1), np.arange(HY_LANES)) / n
    tw = np.stack([np.cos(ang), np.sin(ang)])[:, :, None, :].repeat(cb, axis=2).astype(np.float32)
    a2 = -2.0 * np.pi * np.outer(np.arange(HY_LANES), np.arange(HY_LANES)) / HY_LANES
    fr, fi = np.cos(a2), np.sin(a2)
    fwd = np.block([[fr, fi], [-fi, fr]])
    inv = np.block([[fr, -fi], [fi, fr]])

    def split(m):
        hi = m.astype(BF16)
        lo = (m - hi.astype(np.float64)).astype(BF16)
        return jnp.asarray(np.stack([hi, lo]))

    return jnp.asarray(tw), split(fwd), split(inv)


def _regroup_in_proj(w):
    d = w.shape[0]
    half = ATT_HD // 2
    q = w[:, W_COL_TQ:W_COL_TK].reshape(d, ATT_KV, 2, half, 2)
    q = q.transpose(0, 2, 1, 4, 3).reshape(d, ATT_HEADS * ATT_HD)
    k = w[:, W_COL_TK:W_COL_TV].reshape(d, ATT_KV, half, 2).transpose(0, 1, 3, 2).reshape(d, ATT_KV * ATT_HD)
    return jnp.concatenate([w[:, :W_COL_S5], q, k, w[:, W_COL_TV:], w[:, W_COL_S5:W_COL_HY]], axis=1)


def _regroup_att_rows(m):
    at = m[3 * GROUP_W:].reshape((2, 2, ATT_HD) + m.shape[1:])
    at = jnp.swapaxes(at, 0, 1).reshape((GROUP_W,) + m.shape[1:])
    return jnp.concatenate([m[:3 * GROUP_W], at], axis=0)


def _rope_tables(L):
    rows = L // GRID_W
    row = jnp.repeat(jnp.arange(rows), GRID_W).astype(F32)
    col = jnp.tile(jnp.arange(GRID_W), rows).astype(F32)
    axis_dim = ATT_HD // 2
    inv = 1.0 / (ROPE_BASE ** (jnp.arange(0, axis_dim, 2, dtype=F32) / axis_dim))
    ang = jnp.concatenate([row[:, None] * inv, col[:, None] * inv], axis=-1)
    c, s = jnp.cos(ang), jnp.sin(ang)
    return jnp.tile(jnp.concatenate([c, c], axis=1), (1, 2)), jnp.tile(jnp.concatenate([-s, s], axis=1), (1, 2))


def kernel(x, c, ctx, c_ctx, ada_w, ada_b, norm_mix_w, norm_ffn_w, w_in, hg_lb_logits, s5_a_re, s5_a_im, s5_log_step, s5_b_re, s5_b_im, s5_c_re, s5_c_im, s5_d, s5_glu_w, hy_conv_w, hy_conv_b, hy_w1, hy_b1, hy_freq, hy_w2, hy_b2, hy_w3, hy_bias, att_sink, merge_norm_w, w_out, ffn_w_up, ffn_conv_w, ffn_conv_b, ffn_w_down, final_norm_w):
    bsz, seq_len, _ = x.shape
    ctx_len = ctx.shape[1]

    cond = jnp.zeros((16, D_MODEL), F32).at[:bsz].set(c).at[bsz].set(c_ctx)
    ada = _ada_all(cond, ada_w, ada_b)

    def mods_of(m):
        sh1, sc1, g1, sh2, sc2, g2 = jnp.split(m, 6, axis=-1)
        z = jnp.zeros_like(sh1)
        return jnp.stack([1.0 + sc1, sh1, g1, 1.0 + sc2, sh2, g2, z, z], axis=1)

    sm = jax.nn.softmax(hg_lb_logits, axis=0)
    lower_bounds = jnp.cumsum(sm, axis=0) - sm[0:1]
    cos_t, sin_t = _rope_tables(seq_len)
    hg_zero = jnp.zeros((bsz, 2, 2, 128, 128), F32)
    hy_tw_lat, hy_wf, hy_wi = _hy_constants(2 * seq_len // HY_LANES, HY_CB_LATENT)
    hy_tw_ctx, _, _ = _hy_constants(2 * ctx_len // HY_LANES, HY_CB_CONTEXT)
    s5_zero = jnp.zeros((4, bsz, S5_NS), F32)
    w_up_bf = ffn_w_up.astype(BF16)
    w_down_bf = ffn_w_down.astype(BF16)

    xc = ctx
    for l in range(DEPTH):
        last = l == DEPTH - 1
        mods_lat = mods_of(ada[l, :bsz])
        mods_ctx = jnp.broadcast_to(mods_of(ada[l, bsz:bsz + 1]), (bsz, 8, D_MODEL))
        w_in_l = _regroup_in_proj(w_in[l]).astype(BF16)
        mw = _regroup_att_rows(merge_norm_w[l])
        wo = _regroup_att_rows(w_out[l]).astype(BF16)
        cw = jnp.concatenate([ffn_conv_w[l], ffn_conv_b[l][None], jnp.zeros((4, D_FF), F32)], axis=0)

        wh = w_in[l][:, W_COL_HY:W_COL_HY + 3 * HY_CH].T.astype(BF16)
        p_lat, u_lat, zh_lat = _inproj(x, mods_lat, norm_mix_w[l], w_in_l, wh, 512)
        p_ctx, u_ctx, zh_ctx = _inproj(xc, mods_ctx, norm_mix_w[l], w_in_l, wh, ctx_len)

        hgf_ctx, hgb_ctx, hg_state = _hgrn2(p_ctx, lower_bounds[l], hg_zero)
        hgf_lat, hgb_lat, _ = _hgrn2(p_lat, lower_bounds[l], hg_state)

        s5_wb, s5_wc, s5_a = _s5_weights(s5_a_re[l], s5_a_im[l], s5_log_step[l], s5_b_re[l],
                                         s5_b_im[l], s5_c_re[l], s5_c_im[l])
        glu = s5_glu_w[l].astype(BF16)
        yf_ctx, yb_ctx, s5_state = _s5_scan(u_ctx, s5_wb, s5_wc, s5_a, s5_zero)
        yf_lat, yb_lat, _ = _s5_scan(u_lat, s5_wb, s5_wc, s5_a, s5_state)

        hy_params = (hy_w1[l], hy_b1[l], hy_freq[l], hy_w2[l], hy_b2[l], hy_w3[l])
        hy_cw = jnp.concatenate([hy_conv_w[l], hy_conv_b[l][None]], axis=0)
        hy_cw = jnp.broadcast_to(hy_cw.reshape(4, 3, HY_CH).transpose(1, 0, 2)[..., None],
                                 (3, 4, HY_CH, HY_LANES))
        hy_bb = jnp.broadcast_to(hy_bias[l][..., None], (HY_ORDER, HY_CH, HY_LANES))
        kf_lat = _hy_spectrum(_hy_filter_taps(seq_len, *hy_params), hy_tw_lat, hy_wf)
        hy_lat = _hy_conv(zh_lat, hy_cw, hy_bb, kf_lat, hy_tw_lat, hy_wf, hy_wi)

        at_lat = _att_lat(p_lat, p_ctx, cos_t, sin_t, att_sink[l])

        x_mid, h_lat = _merge(hgf_lat, hgb_lat, yf_lat, yb_lat, s5_d[l], glu, hy_lat, at_lat, p_lat, x,
                              mods_lat, mw, norm_ffn_w[l], wo, 512)
        x_new = _ffn(h_lat, x_mid, mods_lat, w_up_bf, cw, w_down_bf, l, 512,
                     final_w=final_norm_w if last else None)
        if not last:
            kf_ctx = _hy_spectrum(_hy_filter_taps(ctx_len, *hy_params), hy_tw_ctx, hy_wf)
            hy_ctx = _hy_conv(zh_ctx, hy_cw, hy_bb, kf_ctx, hy_tw_ctx, hy_wf, hy_wi)
            at_ctx = _att_ctx(p_ctx, att_sink[l])
            xc_mid, h_ctx = _merge(hgf_ctx, hgb_ctx, yf_ctx, yb_ctx, s5_d[l], glu, hy_ctx, at_ctx, p_ctx,
                                   xc, mods_ctx, mw, norm_ffn_w[l], wo, ctx_len)
            xc = _ffn(h_ctx, xc_mid, mods_ctx, w_up_bf, cw, w_down_bf, l, ctx_len)
        x = x_new
    return x
```

```python
import functools
import math

import jax
import jax.numpy as jnp
import numpy as np
from jax import lax
from jax.experimental import pallas as pl
from jax.experimental.pallas import tpu as pltpu

F32 = jnp.float32
BF16 = jnp.bfloat16

D_MODEL = 1024
DEPTH = 4
GRID_W = 64
GROUP_W = 256
HG_DK = 64
HG_DV = 64
HG_HEADS = 4
HG_CHUNK = 16
S5_GROUP_CH = 16
S5_GROUPS = 16
S5_STATE = 64
HY_CH = 256
HY_ORDER = 2
HY_EMB = 33
HY_MAX_DECAY = math.log(1e-2) / 0.3
HY_MIN_DECAY = math.log(1e-2) / 1.5
ATT_HD = 64
ATT_HEADS = 4
ATT_KV = 2
WINDOW = 128
ATT_BLOCK = 128
ATT_QB = 2
ATT_SCALE = 1.0 / math.sqrt(ATT_HD)
ROPE_BASE = 10000.0
D_FF = 2816
EPS = 1e-6
IN_COLS = 2816
W_COL_S5 = 1280
W_COL_HY = 1536
W_COL_TQ = 2304
W_COL_TK = 2560
W_COL_TV = 2688
P_COLS = 1792
COL_TQ = 1280
COL_TK = 1536
COL_TV = 1664

FF_CHUNK = 256
N_FF_CHUNKS = D_FF // FF_CHUNK
HALO = 16
VMEM_LIMIT = 56 * 1024 * 1024


def _cparams(sem):
    return pltpu.CompilerParams(dimension_semantics=sem, vmem_limit_bytes=VMEM_LIMIT)


def _ada_kernel(c_ref, w_ref, b_ref, o_ref):
    cond = c_ref[...]
    act = cond * jax.nn.sigmoid(cond)
    o_ref[...] = jnp.dot(act.astype(BF16), w_ref[...].astype(BF16),
                         preferred_element_type=F32) + b_ref[...]


def _ada_all(cond, ada_w, ada_b):
    tn = 1024
    n6 = 6 * D_MODEL
    return pl.pallas_call(
        _ada_kernel,
        out_shape=jax.ShapeDtypeStruct((DEPTH, 16, n6), F32),
        grid=(DEPTH, n6 // tn),
        in_specs=[
            pl.BlockSpec((16, D_MODEL), lambda l, j: (0, 0)),
            pl.BlockSpec((None, D_MODEL, tn), lambda l, j: (l, 0, j)),
            pl.BlockSpec((None, 1, tn), lambda l, j: (l, 0, j)),
        ],
        out_specs=pl.BlockSpec((None, 16, tn), lambda l, j: (l, 0, j)),
        compiler_params=_cparams(("arbitrary", "arbitrary")),
        name="ada_mod",
    )(cond, ada_w, ada_b.reshape(DEPTH, 1, n6))


def _inproj_kernel(x_ref, mod_ref, nw_ref, w_ref, wh_ref, o_ref, u_ref, zh_ref):
    x = x_ref[...]
    ms = jnp.mean(x * x, axis=-1, keepdims=True)
    y = x * lax.rsqrt(ms + EPS) * nw_ref[...]
    y = (y * mod_ref[0:1, :] + mod_ref[1:2, :]).astype(BF16)
    p = jnp.dot(y, w_ref[...], preferred_element_type=F32)
    o_ref[...] = p
    for c in range(2):
        u_ref.at[c][pl.ds(pl.program_id(1), x.shape[0], stride=8), :] = (
            p[:, P_COLS + 128 * c:P_COLS + 128 * (c + 1)])
    zt = lax.dot_general(wh_ref[...], y, _NT, preferred_element_type=F32)
    for s in range(zh_ref.shape[0]):
        zh_ref[s] = zt[:, 128 * s:128 * (s + 1)]


def _inproj(x, mods, nw, w_bf16, wh_bf16, tm):
    b_, L, _ = x.shape
    return pl.pallas_call(
        _inproj_kernel,
        out_shape=(jax.ShapeDtypeStruct((b_, L, P_COLS + GROUP_W), F32),
                   jax.ShapeDtypeStruct((2, L * b_, 128), F32),
                   jax.ShapeDtypeStruct((b_, L // 128, 3 * HY_CH, 128), F32)),
        grid=(L // tm, b_),
        in_specs=[
            pl.BlockSpec((None, tm, D_MODEL), lambda i, b: (b, i, 0)),
            pl.BlockSpec((None, 8, D_MODEL), lambda i, b: (b, 0, 0)),
            pl.BlockSpec((1, D_MODEL), lambda i, b: (0, 0)),
            pl.BlockSpec((D_MODEL, P_COLS + GROUP_W), lambda i, b: (0, 0)),
            pl.BlockSpec((3 * HY_CH, D_MODEL), lambda i, b: (0, 0)),
        ],
        out_specs=(pl.BlockSpec((None, tm, P_COLS + GROUP_W), lambda i, b: (b, i, 0)),
                   pl.BlockSpec((2, tm * b_, 128), lambda i, b: (0, i, 0)),
                   pl.BlockSpec((None, tm // 128, 3 * HY_CH, 128), lambda i, b: (b, i, 0, 0))),
        compiler_params=_cparams(("arbitrary", "arbitrary")),
        name="in_proj",
    )(x, mods, nw.reshape(1, D_MODEL), w_bf16, wh_bf16)


def _half_swap(t):
    w = t.shape[-1]
    lane = lax.broadcasted_iota(jnp.int32, t.shape, t.ndim - 1)
    return jnp.where((lane % ATT_HD) < ATT_HD // 2,
                     pltpu.roll(t, w - ATT_HD // 2, t.ndim - 1),
                     pltpu.roll(t, ATT_HD // 2, t.ndim - 1))


def _rope(t, cos, sin):
    return t * cos + _half_swap(t) * sin


def _attend(q, keys, vals, valid, sink_ref):
    lane = lax.broadcasted_iota(jnp.int32, (1, 128), 1)
    lo = lane < ATT_HD
    kb = keys.astype(BF16)
    v_lo = jnp.where(lo, vals, 0.0).astype(BF16)
    v_hi = jnp.where(lo, 0.0, vals).astype(BF16)
    v_cat = jnp.concatenate([v_lo, v_hi], axis=0)
    outs = []
    for m in range(2):
        qm = q[:, 128 * m:128 * (m + 1)]
        probs, inv = [], []
        for half in range(2):
            head = (0, 2, 1, 3)[2 * m + half]
            qh = jnp.where(lo if half == 0 else jnp.logical_not(lo), qm, 0.0).astype(BF16)
            s = lax.dot_general(qh, kb, (((1,), (1,)), ((), ())), preferred_element_type=F32)
            if valid is not None:
                s = jnp.where(valid, s, -jnp.inf)
            sink = sink_ref[head]
            mx = jnp.maximum(jnp.max(s, axis=-1, keepdims=True), sink)
            e = jnp.exp(s - mx)
            den = jnp.sum(e, axis=-1, keepdims=True) + jnp.exp(sink - mx)
            probs.append(e.astype(BF16))
            inv.append(1.0 / den)
        p_cat = jnp.concatenate(probs, axis=1)
        outs.append(jnp.dot(p_cat, v_cat, preferred_element_type=F32) * jnp.where(lo, inv[0], inv[1]))
    return jnp.concatenate(outs, axis=1)


def _att_lat_kernel(sink_ref, q_ref, kp_ref, kc_ref, kn_ref, vp_ref, vc_ref, vn_ref,
                    kx_ref, vx_ref, cp_ref, cc_ref, cn_ref, sp_ref, sc_ref, sn_ref, o_ref,
                    *, seq_len):
    n = pl.program_id(1) * ATT_QB
    cos_c, sin_c = cc_ref[...], sc_ref[...]
    q = _rope(q_ref[...], jnp.concatenate([cos_c, cos_c], axis=1),
              jnp.concatenate([sin_c, sin_c], axis=1)) * ATT_SCALE
    keys = jnp.concatenate([
        kx_ref[...],
        _rope(kp_ref[...], cp_ref[...], sp_ref[...]),
        _rope(kc_ref[...], cos_c, sin_c),
        _rope(kn_ref[...], cn_ref[...], sn_ref[...])], axis=0)
    vals = jnp.concatenate([vx_ref[...], vp_ref[...], vc_ref[...], vn_ref[...]], axis=0)
    lc = kx_ref.shape[0]
    s_tot = lc + (ATT_QB + 2) * ATT_BLOCK
    col = lax.broadcasted_iota(jnp.int32, (ATT_QB * ATT_BLOCK, s_tot), 1)
    row = lax.broadcasted_iota(jnp.int32, (ATT_QB * ATT_BLOCK, s_tot), 0)
    qpos = n * ATT_BLOCK + row
    kpos = (n - 1) * ATT_BLOCK + (col - lc)
    valid = (col < lc) | ((jnp.abs(kpos - qpos) <= WINDOW) & (kpos >= 0) & (kpos < seq_len))
    o_ref[...] = _attend(q, keys, vals, valid, sink_ref)


def _att_lat(p_lat, p_ctx, cos_t, sin_t, sink):
    b_, L, _ = p_lat.shape
    lc = p_ctx.shape[1]
    nb = L // ATT_BLOCK
    kcol, vcol = COL_TK // 128, COL_TV // 128
    qrows = ATT_QB * ATT_BLOCK
    prev = lambda b, n: jnp.maximum(n * ATT_QB - 1, 0)
    nxt = lambda b, n: jnp.minimum((n + 1) * ATT_QB, nb - 1)
    edge = lambda c, f: pl.BlockSpec((None, ATT_BLOCK, 128), lambda b, n: (b, f(b, n), c))
    cur = lambda c: pl.BlockSpec((None, qrows, 128), lambda b, n: (b, n, c))
    tab_edge = lambda f: pl.BlockSpec((ATT_BLOCK, 128), lambda b, n: (f(b, n), 0))
    tab_cur = lambda: pl.BlockSpec((qrows, 128), lambda b, n: (n, 0))
    return pl.pallas_call(
        functools.partial(_att_lat_kernel, seq_len=L),
        out_shape=jax.ShapeDtypeStruct((b_, L, GROUP_W), F32),
        grid=(b_, nb // ATT_QB),
        in_specs=[
            pl.BlockSpec(memory_space=pltpu.SMEM),
            pl.BlockSpec((None, qrows, 256), lambda b, n: (b, n, COL_TQ // 256)),
            edge(kcol, prev), cur(kcol), edge(kcol, nxt),
            edge(vcol, prev), cur(vcol), edge(vcol, nxt),
            pl.BlockSpec((None, lc, 128), lambda b, n: (b, 0, kcol)),
            pl.BlockSpec((None, lc, 128), lambda b, n: (b, 0, vcol)),
            tab_edge(prev), tab_cur(), tab_edge(nxt), tab_edge(prev), tab_cur(), tab_edge(nxt),
        ],
        out_specs=pl.BlockSpec((None, qrows, GROUP_W), lambda b, n: (b, n, 0)),
        compiler_params=_cparams(("arbitrary", "arbitrary")),
        name="att_latent",
    )(sink, p_lat, p_lat, p_lat, p_lat, p_lat, p_lat, p_lat, p_ctx, p_ctx,
      cos_t, cos_t, cos_t, sin_t, sin_t, sin_t)


def _att_ctx_kernel(sink_ref, q_ref, k_ref, v_ref, o_ref):
    o_ref[...] = _attend(q_ref[...] * ATT_SCALE, k_ref[...], v_ref[...], None, sink_ref)


def _att_ctx(p_ctx, sink):
    b_, lc, _ = p_ctx.shape
    return pl.pallas_call(
        _att_ctx_kernel,
        out_shape=jax.ShapeDtypeStruct((b_, lc, GROUP_W), F32),
        grid=(b_,),
        in_specs=[
            pl.BlockSpec(memory_space=pltpu.SMEM),
            pl.BlockSpec((None, lc, 256), lambda b: (b, 0, COL_TQ // 256)),
            pl.BlockSpec((None, lc, 128), lambda b: (b, 0, COL_TK // 128)),
            pl.BlockSpec((None, lc, 128), lambda b: (b, 0, COL_TV // 128)),
        ],
        out_specs=pl.BlockSpec((None, lc, GROUP_W), lambda b: (b, 0, 0)),
        compiler_params=_cparams(("arbitrary",)),
        name="att_context",
    )(sink, p_ctx, p_ctx, p_ctx)


HG_T = 128
HG_SUB = 4
HG_LEVELS = (64, 32, 16)
_NT = (((1,), (1,)), ((), ()))
_TN = (((0,), (0,)), ((), ()))


def _row_fill(ref, rows, blk):
    w = ref.shape[-1]
    return jnp.concatenate([jnp.broadcast_to(ref[r:r + 1, :], (blk, w)) for r in rows], axis=0)


def _gla_block(q_raw, z, v, lb, st_ref, cum_scr, rev):
    t = HG_T
    e = jnp.exp(-jnp.abs(z))
    big = 1.0 / (1.0 + e)
    small = e * big
    sg = jnp.where(z >= 0, big, small)
    g = jnp.log(lb + (1.0 - lb) * sg)
    k = (1.0 - lb) * jnp.where(z >= 0, small, big)
    q = q_raw * jax.nn.sigmoid(q_raw)

    g_hi = g.astype(BF16)
    r1 = g - g_hi.astype(F32)
    g_mid = r1.astype(BF16)
    g_lo = (r1 - g_mid.astype(F32)).astype(BF16)
    ti = lax.broadcasted_iota(jnp.int32, (t, t), 0)
    si = lax.broadcasted_iota(jnp.int32, (t, t), 1)
    tri = jnp.where((si >= ti) if rev else (si <= ti), 1.0, 0.0).astype(BF16)
    c3 = jnp.dot(tri, jnp.concatenate([g_hi, g_mid, g_lo], axis=1), preferred_element_type=F32)
    cum = c3[:, 0:256] + c3[:, 256:512] + c3[:, 512:768]
    cum_scr[...] = cum
    edge = 0 if rev else t - 1
    g_tot = cum_scr[edge:edge + 1, :]

    lane = lax.broadcasted_iota(jnp.int32, (1, 128), 1)
    lo = lane < HG_DK
    rowi = lax.broadcasted_iota(jnp.int32, (t, 1), 0)
    vi = lax.broadcasted_iota(jnp.int32, (128, 128), 0)
    ki = lax.broadcasted_iota(jnp.int32, (128, 128), 1)
    same_head = (vi < HG_DV) == (ki < HG_DK)

    lo_b = jnp.where(lo, 1.0, 0.0).astype(BF16)
    hi_b = jnp.where(lo, 0.0, 1.0).astype(BF16)

    def scores(qt, kt):
        q2 = jnp.concatenate([qt * lo_b, qt * hi_b], axis=0)
        return lax.dot_general(q2, kt, _NT, preferred_element_type=F32)

    outs = []
    for pr in range(2):
        sl = slice(128 * pr, 128 * (pr + 1))
        qp, kp, vp, cp, gp = q[:, sl], k[:, sl], v[:, sl], cum[:, sl], g_tot[:, sl]
        qb, kb = qp.astype(BF16), kp.astype(BF16)
        a0 = jnp.zeros((t, t), F32)
        a1 = jnp.zeros((t, t), F32)
        for sh, h in zip((6, 5, 4), HG_LEVELS):
            rows = [b * 2 * h + (h if rev else h - 1) for b in range(t // (2 * h))]
            e = jnp.exp(-jnp.abs(cp - _row_fill(cum_scr.at[:, sl], rows, 2 * h)))
            odd = (jnp.right_shift(rowi, sh) & 1) == 1
            q_on = jnp.logical_not(odd) if rev else odd
            sc = scores(qb * jnp.where(q_on, e, 0.0).astype(BF16),
                        kb * jnp.where(q_on, 0.0, e).astype(BF16))
            if 2 * h < t:
                keep = jnp.right_shift(ti, sh + 1) == jnp.right_shift(si, sh + 1)
                a0 += jnp.where(keep, sc[:t], 0.0)
                a1 += jnp.where(keep, sc[t:], 0.0)
            else:
                a0 += sc[:t]
                a1 += sc[t:]
        rows = [b * HG_CHUNK + (HG_CHUNK // 2 if rev else HG_CHUNK // 2 - 1) for b in range(t // HG_CHUNK)]
        dd = cp - _row_fill(cum_scr.at[:, sl], rows, HG_CHUNK)
        sc = scores((qp * jnp.exp(dd)).astype(BF16), (kp * jnp.exp(-dd)).astype(BF16))
        keep = (jnp.right_shift(ti, 4) == jnp.right_shift(si, 4)) & ((si >= ti) if rev else (si <= ti))
        a0 += jnp.where(keep, sc[:t], 0.0)
        a1 += jnp.where(keep, sc[t:], 0.0)

        v2 = jnp.concatenate([jnp.where(lo, vp, 0.0), jnp.where(lo, 0.0, vp)], axis=0).astype(BF16)
        o = jnp.dot(jnp.concatenate([a0, a1], axis=1).astype(BF16), v2, preferred_element_type=F32)
        st = st_ref[pr]
        o += lax.dot_general((qp * jnp.exp(cp)).astype(BF16), st.astype(BF16), _NT,
                             preferred_element_type=F32)
        upd = lax.dot_general(vp.astype(BF16), (kp * jnp.exp(gp - cp)).astype(BF16), _TN,
                              preferred_element_type=F32)
        st_ref[pr] = st * jnp.exp(gp) + jnp.where(same_head, upd, 0.0)
        outs.append(o)
    return jnp.concatenate(outs, axis=1)


def _hgrn2_kernel(qf_ref, zf_ref, vf_ref, qb_ref, zb_ref, vb_ref, lb_ref, s0_ref,
                  of_ref, ob_ref, sl_ref, st_f, st_b, cum_scr):
    j = pl.program_id(1)

    @pl.when(j == 0)
    def _():
        st_f[...] = s0_ref[0]
        st_b[...] = s0_ref[1]

    sub = qf_ref.shape[0] // HG_T
    for i in range(sub):
        rf = slice(HG_T * i, HG_T * (i + 1))
        of_ref[rf, :] = _gla_block(qf_ref[rf, :], zf_ref[rf, :], vf_ref[rf, :], lb_ref[0:1, :],
                                   st_f, cum_scr.at[0, i], False)
        rb = slice(HG_T * (sub - 1 - i), HG_T * (sub - i))
        ob_ref[rb, :] = _gla_block(qb_ref[rb, :], zb_ref[rb, :], vb_ref[rb, :], lb_ref[1:2, :],
                                   st_b, cum_scr.at[1, i], True)
    sl_ref[0] = st_f[...]
    sl_ref[1] = st_b[...]


def _hgrn2(p, lb, s0):
    b_, L, _ = p.shape
    sub = min(HG_SUB, L // HG_T)
    rows = HG_T * sub
    n = L // rows
    fwd = lambda c: pl.BlockSpec((None, rows, GROUP_W), lambda b, j: (b, j, c))
    bwd = lambda c: pl.BlockSpec((None, rows, GROUP_W), lambda b, j: (b, n - 1 - j, c))
    st_spec = pl.BlockSpec((None, 2, 2, 128, 128), lambda b, j: (b, 0, 0, 0, 0))
    o_shape = jax.ShapeDtypeStruct((b_, L, GROUP_W), F32)
    return pl.pallas_call(
        _hgrn2_kernel,
        out_shape=(o_shape, o_shape, jax.ShapeDtypeStruct((b_, 2, 2, 128, 128), F32)),
        grid=(b_, n),
        in_specs=[fwd(0), fwd(1), fwd(3), bwd(0), bwd(2), bwd(3),
                  pl.BlockSpec((2, GROUP_W), lambda b, j: (0, 0)), st_spec],
        out_specs=(fwd(0), bwd(0), st_spec),
        scratch_shapes=[pltpu.VMEM((2, 128, 128), F32), pltpu.VMEM((2, 128, 128), F32),
                        pltpu.VMEM((2, sub, HG_T, GROUP_W), F32)],
        compiler_params=_cparams(("arbitrary", "arbitrary")),
        name="hgrn2",
    )(p, p, p, p, p, p, lb, s0)


S5_TT = 128
S5_NS = S5_GROUPS * S5_STATE
S5_LANES = 256


def _s5_kernel(uf_ref, ub_ref, wb_ref, wc_ref, a_ref, x0_ref, yf_ref, yb_ref, xl_ref,
               buf_f, buf_b, st, y_scr):
    j = pl.program_id(0)

    @pl.when(j == 0)
    def _():
        st[...] = x0_ref[...]

    halves = lambda r: jnp.concatenate([r[0], r[1]], axis=1).astype(BF16)
    buf_f[...] = jnp.dot(halves(uf_ref), wb_ref[0], preferred_element_type=F32)
    buf_b[...] = jnp.dot(halves(ub_ref), wb_ref[1], preferred_element_type=F32)

    for cc in range(S5_NS // S5_LANES):
        re = slice(cc * S5_LANES, (cc + 1) * S5_LANES)
        im = slice(S5_NS + cc * S5_LANES, S5_NS + (cc + 1) * S5_LANES)
        arf, aif, arb, aib = a_ref[0, :, re], a_ref[1, :, re], a_ref[2, :, re], a_ref[3, :, re]

        def step(t, carry, re=re, im=im, arf=arf, aif=aif, arb=arb, aib=aib):
            xrf, xif, xrb, xib = carry
            rf = pl.multiple_of(t * 8, 8)
            rb = pl.multiple_of((S5_TT - 1 - t) * 8, 8)
            nrf = arf * xrf - aif * xif + buf_f[pl.ds(rf, 8), re]
            nif = arf * xif + aif * xrf + buf_f[pl.ds(rf, 8), im]
            nrb = arb * xrb - aib * xib + buf_b[pl.ds(rb, 8), re]
            nib = arb * xib + aib * xrb + buf_b[pl.ds(rb, 8), im]
            buf_f[pl.ds(rf, 8), re] = nrf
            buf_f[pl.ds(rf, 8), im] = nif
            buf_b[pl.ds(rb, 8), re] = nrb
            buf_b[pl.ds(rb, 8), im] = nib
            return nrf, nif, nrb, nib

        fin = lax.fori_loop(0, S5_TT, step,
                            (st[0, :, re], st[1, :, re], st[2, :, re], st[3, :, re]), unroll=4)
        for k in range(4):
            st[k, :, re] = fin[k]

    for y_ref, buf, d in ((yf_ref, buf_f, 0), (yb_ref, buf_b, 1)):
        y = jnp.dot(buf[...].astype(BF16), wc_ref[d], preferred_element_type=F32)
        for c in range(2):
            y_scr[c] = y[:, 128 * c:128 * (c + 1)]
        for b in range(8):
            for c in range(2):
                y_ref[b, :, 128 * c:128 * (c + 1)] = y_scr.at[c][pl.ds(b, S5_TT, stride=8), :]
    xl_ref[...] = st[...]


def _s5_scan(u2, wb, wc, a_bc, x0):
    L = u2.shape[1] // 8
    rows = S5_TT * 8
    n = L // S5_TT
    y_shape = jax.ShapeDtypeStruct((8, L, GROUP_W), F32)
    return pl.pallas_call(
        _s5_kernel,
        out_shape=(y_shape, y_shape, jax.ShapeDtypeStruct((4, 8, S5_NS), F32)),
        grid=(n,),
        in_specs=[
            pl.BlockSpec((2, rows, 128), lambda j: (0, j, 0)),
            pl.BlockSpec((2, rows, 128), lambda j: (0, n - 1 - j, 0)),
            pl.BlockSpec((2, GROUP_W, 2 * S5_NS), lambda j: (0, 0, 0)),
            pl.BlockSpec((2, 2 * S5_NS, GROUP_W), lambda j: (0, 0, 0)),
            pl.BlockSpec((4, 8, S5_NS), lambda j: (0, 0, 0)),
            pl.BlockSpec((4, 8, S5_NS), lambda j: (0, 0, 0)),
        ],
        out_specs=(pl.BlockSpec((8, S5_TT, GROUP_W), lambda j: (0, j, 0)),
                   pl.BlockSpec((8, S5_TT, GROUP_W), lambda j: (0, n - 1 - j, 0)),
                   pl.BlockSpec((4, 8, S5_NS), lambda j: (0, 0, 0))),
        scratch_shapes=[pltpu.VMEM((rows, 2 * S5_NS), F32),
                        pltpu.VMEM((rows, 2 * S5_NS), F32),
                        pltpu.VMEM((4, 8, S5_NS), F32),
                        pltpu.VMEM((2, rows, 128), F32)],
        compiler_params=_cparams(("arbitrary",)),
        name="s5_scan",
    )(u2, u2, wb, wc, a_bc, x0)


def _s5_weights(a_re, a_im, log_step, b_re, b_im, c_re, c_im):
    eye = jnp.eye(S5_GROUPS, dtype=F32)
    wbs, wcs, abc = [], [], []
    for d in range(2):
        ab_re, ab_im, bb_re, bb_im = _s5_discretize(a_re[d], a_im[d], log_step[d], b_re, b_im)
        wb_re = jnp.einsum('gph,gk->ghkp', bb_re, eye).reshape(GROUP_W, S5_NS)
        wb_im = jnp.einsum('gph,gk->ghkp', bb_im, eye).reshape(GROUP_W, S5_NS)
        wbs.append(jnp.concatenate([wb_re, wb_im], axis=1))
        wc_re = jnp.einsum('ghp,gk->gpkh', c_re[d], eye).reshape(S5_NS, GROUP_W)
        wc_im = jnp.einsum('ghp,gk->gpkh', c_im[d], eye).reshape(S5_NS, GROUP_W)
        wcs.append(jnp.concatenate([wc_re, -wc_im], axis=0))
        abc += [jnp.broadcast_to(ab_re.reshape(1, S5_NS), (8, S5_NS)),
                jnp.broadcast_to(ab_im.reshape(1, S5_NS), (8, S5_NS))]
    return jnp.stack(wbs).astype(BF16), jnp.stack(wcs).astype(BF16), jnp.stack(abc)


def _group_norm(m, w):
    return m * lax.rsqrt(jnp.mean(m * m, axis=-1, keepdims=True) + EPS) * w


def _merge_kernel(hgf_ref, hgb_ref, yf_ref, yb_ref, u_ref, sd_ref, glu_ref, hy_ref, at_ref, gate_ref,
                  x_ref, mod_ref, mw_ref, fw_ref, wo_ref, xo_ref, ho_ref):
    gate = gate_ref[...]
    z = jax.nn.gelu(yf_ref[...] + yb_ref[...] + sd_ref[...] * u_ref[...])
    s5 = z * jax.nn.sigmoid(jnp.dot(z.astype(BF16), glu_ref[...], preferred_element_type=F32))
    parts = [
        _group_norm(hgf_ref[...] + hgb_ref[...], mw_ref[:, 0:256]) * (gate * jax.nn.sigmoid(gate)),
        _group_norm(s5, mw_ref[:, 256:512]),
        _group_norm(jnp.concatenate([hy_ref[s].T for s in range(hy_ref.shape[0])], axis=0),
                    mw_ref[:, 512:768]),
        _group_norm(at_ref[...], mw_ref[:, 768:1024]),
    ]
    mix = jnp.concatenate(parts, axis=1).astype(BF16)
    xn = x_ref[...] + mod_ref[2:3, :] * jnp.dot(mix, wo_ref[...], preferred_element_type=F32)
    xo_ref[...] = xn
    h = xn * lax.rsqrt(jnp.mean(xn * xn, axis=-1, keepdims=True) + EPS) * fw_ref[...]
    ho_ref[...] = (h * mod_ref[3:4, :] + mod_ref[4:5, :]).astype(BF16)


def _merge(hg_f, hg_b, s5_yf, s5_yb, s5_d, glu_bf16, hy, at, p, x, mods, mw, fw, wo_bf16, tm):
    b_, L, _ = x.shape
    grp = lambda: pl.BlockSpec((None, tm, GROUP_W), lambda i, b: (b, i, 0))
    pcol = lambda c: pl.BlockSpec((None, tm, GROUP_W), lambda i, b: (b, i, c))
    return pl.pallas_call(
        _merge_kernel,
        out_shape=(jax.ShapeDtypeStruct((b_, L, D_MODEL), F32),
                   jax.ShapeDtypeStruct((b_, L, D_MODEL), BF16)),
        grid=(L // tm, b_),
        in_specs=[
            grp(), grp(), grp(), grp(), pcol(P_COLS // GROUP_W),
            pl.BlockSpec((1, GROUP_W), lambda i, b: (0, 0)),
            pl.BlockSpec((GROUP_W, GROUP_W), lambda i, b: (0, 0)),
            pl.BlockSpec((None, tm // 128, HY_CH, 128), lambda i, b: (b, i, 0, 0)),
            grp(),
            pl.BlockSpec((None, tm, GROUP_W), lambda i, b: (b, i, 4)),
            pl.BlockSpec((None, tm, D_MODEL), lambda i, b: (b, i, 0)),
            pl.BlockSpec((None, 8, D_MODEL), lambda i, b: (b, 0, 0)),
            pl.BlockSpec((1, D_MODEL), lambda i, b: (0, 0)),
            pl.BlockSpec((1, D_MODEL), lambda i, b: (0, 0)),
            pl.BlockSpec((D_MODEL, D_MODEL), lambda i, b: (0, 0)),
        ],
        out_specs=(pl.BlockSpec((None, tm, D_MODEL), lambda i, b: (b, i, 0)),
                   pl.BlockSpec((None, tm, D_MODEL), lambda i, b: (b, i, 0))),
        compiler_params=_cparams(("arbitrary", "arbitrary")),
        name="merge_out_proj",
    )(hg_f, hg_b, s5_yf, s5_yb, p, s5_d.reshape(1, GROUP_W), glu_bf16, hy, at, p, x, mods,
      mw.reshape(1, D_MODEL), fw.reshape(1, D_MODEL), wo_bf16)


def _ffn_kernel(hp_ref, h_ref, hn_ref, x_ref, mod_ref, wu_ref, cw_ref, wd_ref, fw_ref, o_ref,
                acc_scr, *, tm, final_norm):
    i = pl.program_id(1)
    last = pl.num_programs(1) - 1
    h = h_ref[...]
    hp = jnp.where(i > 0, hp_ref[...], jnp.zeros_like(hp_ref))
    hn = jnp.where(i < last, hn_ref[...], jnp.zeros_like(hn_ref))
    h_ext = jnp.concatenate([hp, h, hn], axis=0)
    rows = tm + 2 * HALO

    def gated(j):
        cols = slice(FF_CHUNK * j, FF_CHUNK * (j + 1))
        a = jnp.dot(h_ext, wu_ref[:, cols], preferred_element_type=F32)
        v = jnp.dot(h, wu_ref[:, D_FF + FF_CHUNK * j:D_FF + FF_CHUNK * (j + 1)],
                    preferred_element_type=F32)
        cw = cw_ref[:, cols]
        conv = (pltpu.roll(a, 1, 0)[HALO:HALO + tm] * cw[0:1, :]
                + a[HALO:HALO + tm] * cw[1:2, :]
                + pltpu.roll(a, rows - 1, 0)[HALO:HALO + tm] * cw[2:3, :] + cw[3:4, :])
        return (conv * jax.nn.sigmoid(conv) * v).astype(BF16)

    for j in range(0, N_FF_CHUNKS, 2):
        n = min(2, N_FF_CHUNKS - j)
        g = jnp.concatenate([gated(j + d) for d in range(n)], axis=1)
        part = jnp.dot(g, wd_ref[FF_CHUNK * j:FF_CHUNK * (j + n), :], preferred_element_type=F32)
        if j == 0:
            acc_scr[...] = part
        elif j + n < N_FF_CHUNKS:
            acc_scr[...] += part
        else:
            xo = x_ref[...] + mod_ref[5:6, :] * (acc_scr[...] + part)
            if final_norm:
                xo = xo * lax.rsqrt(jnp.mean(xo * xo, axis=-1, keepdims=True) + EPS) * fw_ref[...]
            o_ref[...] = xo


def _ffn(h, x, mods, w_up, cw, w_down, layer, tm, final_w=None):
    b_, L, _ = x.shape
    nh = L // HALO
    r = tm // HALO
    fw = jnp.ones((1, D_MODEL), F32) if final_w is None else final_w.reshape(1, D_MODEL)
    return pl.pallas_call(
        functools.partial(_ffn_kernel, tm=tm, final_norm=final_w is not None),
        out_shape=jax.ShapeDtypeStruct((b_, L, D_MODEL), F32),
        grid=(b_, L // tm),
        in_specs=[
            pl.BlockSpec((None, HALO, D_MODEL), lambda b, i: (b, jnp.maximum(i * r - 1, 0), 0)),
            pl.BlockSpec((None, tm, D_MODEL), lambda b, i: (b, i, 0)),
            pl.BlockSpec((None, HALO, D_MODEL), lambda b, i: (b, jnp.minimum((i + 1) * r, nh - 1), 0)),
            pl.BlockSpec((None, tm, D_MODEL), lambda b, i: (b, i, 0)),
            pl.BlockSpec((None, 8, D_MODEL), lambda b, i: (b, 0, 0)),
            pl.BlockSpec((None, D_MODEL, 2 * D_FF), lambda b, i: (layer, 0, 0),
                         pipeline_mode=pl.Buffered(1)),
            pl.BlockSpec((8, D_FF), lambda b, i: (0, 0)),
            pl.BlockSpec((None, D_FF, D_MODEL), lambda b, i: (layer, 0, 0),
                         pipeline_mode=pl.Buffered(1)),
            pl.BlockSpec((1, D_MODEL), lambda b, i: (0, 0)),
        ],
        out_specs=pl.BlockSpec((None, tm, D_MODEL), lambda b, i: (b, i, 0)),
        scratch_shapes=[pltpu.VMEM((tm, D_MODEL), F32)],
        compiler_params=_cparams(("arbitrary", "arbitrary")),
        name="conv_ffn",
    )(h, h, h, x, mods, w_up, cw, w_down, fw)


def _s5_discretize(a_re, a_im, log_step, b_re, b_im):
    dt = jnp.exp(log_step)[:, None]
    mag = jnp.exp(a_re * dt)
    ang = a_im * dt
    ab_re, ab_im = mag * jnp.cos(ang), mag * jnp.sin(ang)
    den = a_re * a_re + a_im * a_im
    nr, ni = ab_re - 1.0, ab_im
    fr = (nr * a_re + ni * a_im) / den
    fi = (ni * a_re - nr * a_im) / den
    bb_re = fr[..., None] * b_re - fi[..., None] * b_im
    bb_im = fr[..., None] * b_im + fi[..., None] * b_re
    return ab_re, ab_im, bb_re, bb_im


HY_LANES = 128
HY_CB_LATENT = 16
HY_CB_CONTEXT = 64
HY_EMB_PAD = 128


def _vadd(a, b):
    return b if a is None else a if b is None else a + b


def _vsub(a, b):
    return (None if b is None else -b) if a is None else a if b is None else a - b


def _vscale(a, c):
    return None if (a is None or c == 0.0) else a if c == 1.0 else -a if c == -1.0 else a * c


def _cmul_const(x, wr, wi):
    re, im = x
    return (_vsub(_vscale(re, wr), _vscale(im, wi)), _vadd(_vscale(im, wr), _vscale(re, wi)))


def _unit_root(k, n, sign):
    k %= n
    if (4 * k) % n == 0:
        return ((1.0, 0.0), (0.0, float(sign)), (-1.0, 0.0), (0.0, float(-sign)))[4 * k // n]
    ang = sign * 2.0 * math.pi * k / n
    return math.cos(ang), math.sin(ang)


def _fft_slabs(x, sign, n_out=None):
    n = len(x)
    if n == 1:
        return list(x)
    n_out = n if n_out is None else n_out
    ev = _fft_slabs(x[0::2], sign)
    od = _fft_slabs(x[1::2], sign)
    out = [None] * n_out
    for k in range(n // 2):
        t = _cmul_const(od[k], *_unit_root(k, n, sign))
        if k < n_out:
            out[k] = (_vadd(ev[k][0], t[0]), _vadd(ev[k][1], t[1]))
        if k + n // 2 < n_out:
            out[k + n // 2] = (_vsub(ev[k][0], t[0]), _vsub(ev[k][1], t[1]))
    return out


def _twiddle(slabs, tw_ref, conj):
    out = [slabs[0]]
    for k in range(1, len(slabs)):
        re, im = slabs[k]
        tr = tw_ref[0, k]
        ti = -tw_ref[1, k] if conj else tw_ref[1, k]
        if im is None:
            out.append((re * tr, re * ti))
        else:
            out.append((re * tr - im * ti, re * ti + im * tr))
    return out


def _dot_split(x, w_ref):
    hi = x.astype(BF16)
    lo = (x - hi.astype(F32)).astype(BF16)
    return (jnp.dot(hi, w_ref[0], preferred_element_type=F32)
            + jnp.dot(lo, w_ref[0], preferred_element_type=F32)
            + jnp.dot(hi, w_ref[1], preferred_element_type=F32))


def _lane_dft(slabs, w_ref, cb):
    zero = jnp.zeros((cb, HY_LANES), F32)
    rows = jnp.concatenate(
        [jnp.concatenate([zero if re is None else re, zero if im is None else im], axis=1)
         for re, im in slabs], axis=0)
    out = _dot_split(rows, w_ref)
    return [(out[cb * i:cb * (i + 1), :HY_LANES], out[cb * i:cb * (i + 1), HY_LANES:])
            for i in range(len(slabs))]


def _short_conv(ref, part, cw_ref, ch, lane):
    n = ref.shape[1]
    w0, w1, w2, bb = cw_ref[ch, 0], cw_ref[ch, 1], cw_ref[ch, 2], cw_ref[ch, 3]
    right = [pltpu.roll(ref[part, s], 1, 1) for s in range(n)]
    left = [pltpu.roll(ref[part, s], HY_LANES - 1, 1) for s in range(n)]
    zero = jnp.zeros(lane.shape, F32)
    out = []
    for s in range(n):
        xm = jnp.where(lane == 0, right[s - 1] if s > 0 else zero, right[s])
        xp = jnp.where(lane == HY_LANES - 1, left[s + 1] if s < n - 1 else zero, left[s])
        out.append(xm * w0 + ref[part, s] * w1 + xp * w2 + bb)
    return out


def _hy_conv_kernel(x1_ref, x2_ref, z_ref, cw_ref, bias_ref, kf_ref, tw_ref, wf_ref, wi_ref, o_ref,
                    *, n1):
    nh = n1 // 2
    cb = z_ref.shape[2]
    lane = lax.broadcasted_iota(jnp.int32, (cb, HY_LANES), 1)
    gates = [[_short_conv(r, part, cw_ref, ch, lane) for part in range(2)]
             for ch, r in ((0, x1_ref), (1, x2_ref))]
    z = [_short_conv(z_ref, part, cw_ref, 2, lane) for part in range(2)]
    for o in range(HY_ORDER):
        spec = _fft_slabs([(z[0][s], z[1][s]) for s in range(nh)] + [(None, None)] * nh, -1)
        spec = _lane_dft(_twiddle(spec, tw_ref, False), wf_ref, cb)
        spec = [(re * kf_ref[o, 0, k] - im * kf_ref[o, 1, k], re * kf_ref[o, 1, k] + im * kf_ref[o, 0, k])
                for k, (re, im) in enumerate(spec)]
        spec = _twiddle(_lane_dft(spec, wi_ref, cb), tw_ref, True)
        y = _fft_slabs(spec, 1, n_out=nh)
        bo = bias_ref[o]
        z = [[gates[o][part][s] * (y[s][part] + bo * z[part][s]) for s in range(nh)]
             for part in range(2)]
    for part in range(2):
        for s in range(nh):
            o_ref[part, s] = z[part][s]


def _hy_conv(zh, cw, bias_b, kf, tw, wf, wi):
    b_, nh = zh.shape[0], zh.shape[1]
    cb = tw.shape[2]
    half, ncb = b_ // 2, HY_CH // cb
    n1 = 2 * nh
    zspec = lambda ch: pl.BlockSpec((2, None, nh, cb, HY_LANES),
                                    lambda c, b: (0, b, 0, ch * ncb + c, 0))
    const3 = lambda c, b: (0, 0, 0)
    z5 = zh.reshape(2, half, nh, 3 * HY_CH, HY_LANES)
    out = pl.pallas_call(
        functools.partial(_hy_conv_kernel, n1=n1),
        out_shape=jax.ShapeDtypeStruct((2, half, nh, HY_CH, HY_LANES), F32),
        grid=(ncb, half),
        in_specs=[
            zspec(0), zspec(1), zspec(2),
            pl.BlockSpec((3, 4, cb, HY_LANES), lambda c, b: (0, 0, c, 0)),
            pl.BlockSpec((HY_ORDER, cb, HY_LANES), lambda c, b: (0, c, 0)),
            pl.BlockSpec((HY_ORDER, 2, None, n1, cb, HY_LANES), lambda c, b: (0, 0, c, 0, 0, 0)),
            pl.BlockSpec((2, n1, cb, HY_LANES), lambda c, b: (0, 0, 0, 0)),
            pl.BlockSpec((2, 256, 256), const3),
            pl.BlockSpec((2, 256, 256), const3),
        ],
        out_specs=pl.BlockSpec((2, None, nh, cb, HY_LANES), lambda c, b: (0, b, 0, c, 0)),
        compiler_params=_cparams(("arbitrary", "arbitrary")),
        name="hyena_conv",
    )(z5, z5, z5, cw, bias_b, kf, tw, wf, wi)
    return out.reshape(b_, nh, HY_CH, HY_LANES)


def _hy_spectrum_kernel(k_ref, tw_ref, wf_ref, o_ref, *, n1):
    spec = _fft_slabs([(k_ref[s], None) for s in range(n1)], -1)
    spec = _lane_dft(_twiddle(spec, tw_ref, False), wf_ref, k_ref.shape[1])
    scale = 1.0 / (n1 * HY_LANES)
    for k in range(n1):
        o_ref[0, k] = spec[k][0] * scale
        o_ref[1, k] = spec[k][1] * scale


def _hy_spectrum(taps, tw, wf):
    n1, cb = taps.shape[1], tw.shape[2]
    return pl.pallas_call(
        functools.partial(_hy_spectrum_kernel, n1=n1),
        out_shape=jax.ShapeDtypeStruct((HY_ORDER, 2, HY_CH // cb, n1, cb, HY_LANES), F32),
        grid=(HY_ORDER, HY_CH // cb),
        in_specs=[
            pl.BlockSpec((None, n1, cb, HY_LANES), lambda o, c: (o, 0, c, 0)),
            pl.BlockSpec((2, n1, cb, HY_LANES), lambda o, c: (0, 0, 0, 0)),
            pl.BlockSpec((2, 256, 256), lambda o, c: (0, 0, 0)),
        ],
        out_specs=pl.BlockSpec((None, 2, None, n1, cb, HY_LANES), lambda o, c: (o, 0, c, 0, 0, 0)),
        compiler_params=_cparams(("arbitrary", "arbitrary")),
        name="hyena_spectrum",
    )(taps, tw, wf)


def _dot_f32(a, b, dims=(((1,), (0,)), ((), ()))):
    a_hi = a.astype(BF16)
    a_lo = (a - a_hi.astype(F32)).astype(BF16)
    b_hi = b.astype(BF16)
    b_lo = (b - b_hi.astype(F32)).astype(BF16)
    dot = lambda p, q: lax.dot_general(p, q, dims, preferred_element_type=F32)
    return dot(a_hi, b_hi) + dot(a_lo, b_hi) + dot(a_hi, b_lo)


def _hy_mlp_kernel(emb_ref, w1_ref, w2_ref, w3t_ref, vec_ref, pos_ref, dl_ref, o_ref):
    z = emb_ref[...]
    h = jnp.sin(vec_ref[2:3, :] * (_dot_f32(z, w1_ref[...]) + vec_ref[0:1, :]))
    h = jnp.sin(vec_ref[3:4, :] * (_dot_f32(h, w2_ref[...]) + vec_ref[1:2, :]))
    ht = _dot_f32(w3t_ref[...], h, _NT)
    dl = dl_ref[...]
    for s in range(o_ref.shape[1]):
        lanes = slice(HY_LANES * s, HY_LANES * (s + 1))
        win = jnp.exp(-pos_ref[0:1, lanes] * dl)
        wf, wb = pos_ref[1:2, lanes] * win, pos_ref[2:3, lanes] * win
        for o in range(HY_ORDER):
            base = 2 * HY_CH * o
            o_ref[o, s] = (ht[base:base + HY_CH, lanes] * wf
                           + ht[base + HY_CH:base + 2 * HY_CH, lanes] * wb)


def _hy_filter_taps(L, w1, b1, freq, w2, b2, w3):
    pad = HY_EMB_PAD
    n = 2 * L
    t01 = np.linspace(0.0, 1.0, L, dtype=np.float32)[:, None]
    w = (np.float32(2.0 * math.pi) * np.arange(L, dtype=np.float32)[:, None]) / np.float32(L)
    bands = (HY_EMB - 1) // 2
    fr = np.linspace(1e-4, bands - 1, bands, dtype=np.float32)[None, :]
    arg = (fr * w).astype(np.float64)
    emb = np.zeros((L, pad), np.float32)
    emb[:, 0:1] = t01
    emb[:, 1:1 + bands] = np.cos(arg)
    emb[:, 1 + bands:HY_EMB] = -np.sin(arg)
    pos_idx = np.arange(n)
    lag = np.where(pos_idx < L, pos_idx, (n - pos_idx) % L)
    pos = np.zeros((8, n), np.float32)
    pos[0] = t01[lag, 0]
    pos[1] = pos_idx < L
    pos[2] = (pos_idx > L) | (pos_idx == 0)
    hid = w1.shape[1]
    n_out = w3.shape[1]
    w1p = jnp.zeros((pad, pad), F32).at[:HY_EMB, :hid].set(w1)
    w2p = jnp.zeros((pad, pad), F32).at[:hid, :hid].set(w2)
    w3t = jnp.zeros((n_out, pad), F32).at[:, :hid].set(w3.T)
    vec = jnp.zeros((8, pad), F32).at[0, :hid].set(b1).at[1, :hid].set(b2)
    vec = vec.at[2, :hid].set(freq[0]).at[3, :hid].set(freq[1])
    deltas = np.abs(np.linspace(HY_MIN_DECAY, HY_MAX_DECAY, HY_CH, dtype=np.float32))
    dl = np.broadcast_to(deltas[:, None], (HY_CH, HY_LANES))
    tl = min(n, 512)
    return pl.pallas_call(
        _hy_mlp_kernel,
        out_shape=jax.ShapeDtypeStruct((HY_ORDER, n // HY_LANES, HY_CH, HY_LANES), F32),
        grid=(n // tl,),
        in_specs=[
            pl.BlockSpec((tl, pad), lambda i: (i, 0)),
            pl.BlockSpec((pad, pad), lambda i: (0, 0)),
            pl.BlockSpec((pad, pad), lambda i: (0, 0)),
            pl.BlockSpec((n_out, pad), lambda i: (0, 0)),
            pl.BlockSpec((8, pad), lambda i: (0, 0)),
            pl.BlockSpec((8, tl), lambda i: (0, i)),
            pl.BlockSpec((HY_CH, HY_LANES), lambda i: (0, 0)),
        ],
        out_specs=pl.BlockSpec((HY_ORDER, tl // HY_LANES, HY_CH, HY_LANES), lambda i: (0, i, 0, 0)),
        compiler_params=_cparams(("arbitrary",)),
        name="hyena_filter_mlp",
    )(jnp.asarray(emb[lag]), w1p, w2p, w3t, vec, jnp.asarray(pos), jnp.asarray(dl))


def _hy_constants(n1, cb):
    n = n1 * HY_LANES
    ang = -2.0 * np.pi * np.outer(np.arange(n1), np.arange(HY_LANES)) / n
    tw = np.stack([np.cos(ang), np.sin(ang)])[:, :, None, :].repeat(cb, axis=2).astype(np.float32)
    a2 = -2.0 * np.pi * np.outer(np.arange(HY_LANES), np.arange(HY_LANES)) / HY_LANES
    fr, fi = np.cos(a2), np.sin(a2)
    fwd = np.block([[fr, fi], [-fi, fr]])
    inv = np.block([[fr, -fi], [fi, fr]])

    def split(m):
        hi = m.astype(BF16)
        lo = (m - hi.astype(np.float64)).astype(BF16)
        return jnp.asarray(np.stack([hi, lo]))

    return jnp.asarray(tw), split(fwd), split(inv)


def _regroup_in_proj(w):
    d = w.shape[0]
    half = ATT_HD // 2
    q = w[:, W_COL_TQ:W_COL_TK].reshape(d, ATT_KV, 2, half, 2)
    q = q.transpose(0, 2, 1, 4, 3).reshape(d, ATT_HEADS * ATT_HD)
    k = w[:, W_COL_TK:W_COL_TV].reshape(d, ATT_KV, half, 2).transpose(0, 1, 3, 2).reshape(d, ATT_KV * ATT_HD)
    return jnp.concatenate([w[:, :W_COL_S5], q, k, w[:, W_COL_TV:], w[:, W_COL_S5:W_COL_HY]], axis=1)


def _regroup_att_rows(m):
    at = m[3 * GROUP_W:].reshape((2, 2, ATT_HD) + m.shape[1:])
    at = jnp.swapaxes(at, 0, 1).reshape((GROUP_W,) + m.shape[1:])
    return jnp.concatenate([m[:3 * GROUP_W], at], axis=0)


def _rope_tables(L):
    rows = L // GRID_W
    row = jnp.repeat(jnp.arange(rows), GRID_W).astype(F32)
    col = jnp.tile(jnp.arange(GRID_W), rows).astype(F32)
    axis_dim = ATT_HD // 2
    inv = 1.0 / (ROPE_BASE ** (jnp.arange(0, axis_dim, 2, dtype=F32) / axis_dim))
    ang = jnp.concatenate([row[:, None] * inv, col[:, None] * inv], axis=-1)
    c, s = jnp.cos(ang), jnp.sin(ang)
    return jnp.tile(jnp.concatenate([c, c], axis=1), (1, 2)), jnp.tile(jnp.concatenate([-s, s], axis=1), (1, 2))


def kernel(x, c, ctx, c_ctx, ada_w, ada_b, norm_mix_w, norm_ffn_w, w_in, hg_lb_logits, s5_a_re, s5_a_im, s5_log_step, s5_b_re, s5_b_im, s5_c_re, s5_c_im, s5_d, s5_glu_w, hy_conv_w, hy_conv_b, hy_w1, hy_b1, hy_freq, hy_w2, hy_b2, hy_w3, hy_bias, att_sink, merge_norm_w, w_out, ffn_w_up, ffn_conv_w, ffn_conv_b, ffn_w_down, final_norm_w):
    bsz, seq_len, _ = x.shape
    ctx_len = ctx.shape[1]

    cond = jnp.zeros((16, D_MODEL), F32).at[:bsz].set(c).at[bsz].set(c_ctx)
    ada = _ada_all(cond, ada_w, ada_b)

    def mods_of(m):
        sh1, sc1, g1, sh2, sc2, g2 = jnp.split(m, 6, axis=-1)
        z = jnp.zeros_like(sh1)
        return jnp.stack([1.0 + sc1, sh1, g1, 1.0 + sc2, sh2, g2, z, z], axis=1)

    sm = jax.nn.softmax(hg_lb_logits, axis=0)
    lower_bounds = jnp.cumsum(sm, axis=0) - sm[0:1]
    cos_t, sin_t = _rope_tables(seq_len)
    hg_zero = jnp.zeros((bsz, 2, 2, 128, 128), F32)
    hy_tw_lat, hy_wf, hy_wi = _hy_constants(2 * seq_len // HY_LANES, HY_CB_LATENT)
    hy_tw_ctx, _, _ = _hy_constants(2 * ctx_len // HY_LANES, HY_CB_CONTEXT)
    s5_zero = jnp.zeros((4, bsz, S5_NS), F32)
    w_up_bf = ffn_w_up.astype(BF16)
    w_down_bf = ffn_w_down.astype(BF16)

    xc = ctx
    for l in range(DEPTH):
        last = l == DEPTH - 1
        mods_lat = mods_of(ada[l, :bsz])
        mods_ctx = jnp.broadcast_to(mods_of(ada[l, bsz:bsz + 1]), (bsz, 8, D_MODEL))
        w_in_l = _regroup_in_proj(w_in[l]).astype(BF16)
        mw = _regroup_att_rows(merge_norm_w[l])
        wo = _regroup_att_rows(w_out[l]).astype(BF16)
        cw = jnp.concatenate([ffn_conv_w[l], ffn_conv_b[l][None], jnp.zeros((4, D_FF), F32)], axis=0)

        wh = w_in[l][:, W_COL_HY:W_COL_HY + 3 * HY_CH].T.astype(BF16)
        p_lat, u_lat, zh_lat = _inproj(x, mods_lat, norm_mix_w[l], w_in_l, wh, 512)
        p_ctx, u_ctx, zh_ctx = _inproj(xc, mods_ctx, norm_mix_w[l], w_in_l, wh, ctx_len)

        hgf_ctx, hgb_ctx, hg_state = _hgrn2(p_ctx, lower_bounds[l], hg_zero)
        hgf_lat, hgb_lat, _ = _hgrn2(p_lat, lower_bounds[l], hg_state)

        s5_wb, s5_wc, s5_a = _s5_weights(s5_a_re[l], s5_a_im[l], s5_log_step[l], s5_b_re[l],
                                         s5_b_im[l], s5_c_re[l], s5_c_im[l])
        glu = s5_glu_w[l].astype(BF16)
        yf_ctx, yb_ctx, s5_state = _s5_scan(u_ctx, s5_wb, s5_wc, s5_a, s5_zero)
        yf_lat, yb_lat, _ = _s5_scan(u_lat, s5_wb, s5_wc, s5_a, s5_state)

        hy_params = (hy_w1[l], hy_b1[l], hy_freq[l], hy_w2[l], hy_b2[l], hy_w3[l])
        hy_cw = jnp.concatenate([hy_conv_w[l], hy_conv_b[l][None]], axis=0)
        hy_cw = jnp.broadcast_to(hy_cw.reshape(4, 3, HY_CH).transpose(1, 0, 2)[..., None],
                                 (3, 4, HY_CH, HY_LANES))
        hy_bb = jnp.broadcast_to(hy_bias[l][..., None], (HY_ORDER, HY_CH, HY_LANES))
        kf_lat = _hy_spectrum(_hy_filter_taps(seq_len, *hy_params), hy_tw_lat, hy_wf)
        hy_lat = _hy_conv(zh_lat, hy_cw, hy_bb, kf_lat, hy_tw_lat, hy_wf, hy_wi)

        at_lat = _att_lat(p_lat, p_ctx, cos_t, sin_t, att_sink[l])

        x_mid, h_lat = _merge(hgf_lat, hgb_lat, yf_lat, yb_lat, s5_d[l], glu, hy_lat, at_lat, p_lat, x,
                              mods_lat, mw, norm_ffn_w[l], wo, 512)
        x_new = _ffn(h_lat, x_mid, mods_lat, w_up_bf, cw, w_down_bf, l, 512,
                     final_w=final_norm_w if last else None)
        if not last:
            kf_ctx = _hy_spectrum(_hy_filter_taps(ctx_len, *hy_params), hy_tw_ctx, hy_wf)
            hy_ctx = _hy_conv(zh_ctx, hy_cw, hy_bb, kf_ctx, hy_tw_ctx, hy_wf, hy_wi)
            at_ctx = _att_ctx(p_ctx, att_sink[l])
            xc_mid, h_ctx = _merge(hgf_ctx, hgb_ctx, yf_ctx, yb_ctx, s5_d[l], glu, hy_ctx, at_ctx, p_ctx,
                                   xc, mods_ctx, mw, norm_ffn_w[l], wo, ctx_len)
            xc = _ffn(h_ctx, xc_mid, mods_ctx, w_up_bf, cw, w_down_bf, l, ctx_len)
        x = x_new
    return x
```

```python
import functools
import math

import jax
import jax.numpy as jnp
import numpy as np
from jax import lax
from jax.experimental import pallas as pl
from jax.experimental.pallas import tpu as pltpu

F32 = jnp.float32
BF16 = jnp.bfloat16

D_MODEL = 1024
DEPTH = 4
GRID_W = 64
GROUP_W = 256
HG_DK = 64
HG_DV = 64
HG_HEADS = 4
HG_CHUNK = 16
S5_GROUP_CH = 16
S5_GROUPS = 16
S5_STATE = 64
HY_CH = 256
HY_ORDER = 2
HY_EMB = 33
HY_MAX_DECAY = math.log(1e-2) / 0.3
HY_MIN_DECAY = math.log(1e-2) / 1.5
ATT_HD = 64
ATT_HEADS = 4
ATT_KV = 2
WINDOW = 128
ATT_BLOCK = 128
ATT_QB = 2
ATT_SCALE = 1.0 / math.sqrt(ATT_HD)
ROPE_BASE = 10000.0
D_FF = 2816
EPS = 1e-6
IN_COLS = 2816
W_COL_S5 = 1280
W_COL_HY = 1536
W_COL_TQ = 2304
W_COL_TK = 2560
W_COL_TV = 2688
P_COLS = 1792
COL_TQ = 1280
COL_TK = 1536
COL_TV = 1664

FF_CHUNK = 256
N_FF_CHUNKS = D_FF // FF_CHUNK
HALO = 16
VMEM_LIMIT = 56 * 1024 * 1024


def _cparams(sem):
    return pltpu.CompilerParams(dimension_semantics=sem, vmem_limit_bytes=VMEM_LIMIT)


def _ada_kernel(c_ref, w_ref, b_ref, o_ref):
    cond = c_ref[...]
    act = cond * jax.nn.sigmoid(cond)
    o_ref[...] = jnp.dot(act.astype(BF16), w_ref[...].astype(BF16),
                         preferred_element_type=F32) + b_ref[...]


def _ada_all(cond, ada_w, ada_b):
    tn = 1024
    n6 = 6 * D_MODEL
    return pl.pallas_call(
        _ada_kernel,
        out_shape=jax.ShapeDtypeStruct((DEPTH, 16, n6), F32),
        grid=(DEPTH, n6 // tn),
        in_specs=[
            pl.BlockSpec((16, D_MODEL), lambda l, j: (0, 0)),
            pl.BlockSpec((None, D_MODEL, tn), lambda l, j: (l, 0, j)),
            pl.BlockSpec((None, 1, tn), lambda l, j: (l, 0, j)),
        ],
        out_specs=pl.BlockSpec((None, 16, tn), lambda l, j: (l, 0, j)),
        compiler_params=_cparams(("arbitrary", "arbitrary")),
        name="ada_mod",
    )(cond, ada_w, ada_b.reshape(DEPTH, 1, n6))


def _inproj_kernel(x_ref, mod_ref, nw_ref, w_ref, wh_ref, o_ref, u_ref, zh_ref):
    x = x_ref[...]
    ms = jnp.mean(x * x, axis=-1, keepdims=True)
    y = x * lax.rsqrt(ms + EPS) * nw_ref[...]
    y = (y * mod_ref[0:1, :] + mod_ref[1:2, :]).astype(BF16)
    p = jnp.dot(y, w_ref[...], preferred_element_type=F32)
    o_ref[...] = p
    for c in range(2):
        u_ref.at[c][pl.ds(pl.program_id(1), x.shape[0], stride=8), :] = (
            p[:, P_COLS + 128 * c:P_COLS + 128 * (c + 1)])
    zt = lax.dot_general(wh_ref[...], y, _NT, preferred_element_type=F32)
    for s in range(zh_ref.shape[0]):
        zh_ref[s] = zt[:, 128 * s:128 * (s + 1)]


def _inproj(x, mods, nw, w_bf16, wh_bf16, tm):
    b_, L, _ = x.shape
    return pl.pallas_call(
        _inproj_kernel,
        out_shape=(jax.ShapeDtypeStruct((b_, L, P_COLS + GROUP_W), F32),
                   jax.ShapeDtypeStruct((2, L * b_, 128), F32),
                   jax.ShapeDtypeStruct((b_, L // 128, 3 * HY_CH, 128), F32)),
        grid=(L // tm, b_),
        in_specs=[
            pl.BlockSpec((None, tm, D_MODEL), lambda i, b: (b, i, 0)),
            pl.BlockSpec((None, 8, D_MODEL), lambda i, b: (b, 0, 0)),
            pl.BlockSpec((1, D_MODEL), lambda i, b: (0, 0)),
            pl.BlockSpec((D_MODEL, P_COLS + GROUP_W), lambda i, b: (0, 0)),
            pl.BlockSpec((3 * HY_CH, D_MODEL), lambda i, b: (0, 0)),
        ],
        out_specs=(pl.BlockSpec((None, tm, P_COLS + GROUP_W), lambda i, b: (b, i, 0)),
                   pl.BlockSpec((2, tm * b_, 128), lambda i, b: (0, i, 0)),
                   pl.BlockSpec((None, tm // 128, 3 * HY_CH, 128), lambda i, b: (b, i, 0, 0))),
        compiler_params=_cparams(("arbitrary", "arbitrary")),
        name="in_proj",
    )(x, mods, nw.reshape(1, D_MODEL), w_bf16, wh_bf16)


def _half_swap(t):
    w = t.shape[-1]
    lane = lax.broadcasted_iota(jnp.int32, t.shape, t.ndim - 1)
    return jnp.where((lane % ATT_HD) < ATT_HD // 2,
                     pltpu.roll(t, w - ATT_HD // 2, t.ndim - 1),
                     pltpu.roll(t, ATT_HD // 2, t.ndim - 1))


def _rope(t, cos, sin):
    return t * cos + _half_swap(t) * sin


def _attend(q, keys, vals, valid, sink_ref):
    lane = lax.broadcasted_iota(jnp.int32, (1, 128), 1)
    lo = lane < ATT_HD
    kb = keys.astype(BF16)
    v_lo = jnp.where(lo, vals, 0.0).astype(BF16)
    v_hi = jnp.where(lo, 0.0, vals).astype(BF16)
    v_cat = jnp.concatenate([v_lo, v_hi], axis=0)
    outs = []
    for m in range(2):
        qm = q[:, 128 * m:128 * (m + 1)]
        probs, inv = [], []
        for half in range(2):
            head = (0, 2, 1, 3)[2 * m + half]
            qh = jnp.where(lo if half == 0 else jnp.logical_not(lo), qm, 0.0).astype(BF16)
            s = lax.dot_general(qh, kb, (((1,), (1,)), ((), ())), preferred_element_type=F32)
            if valid is not None:
                s = jnp.where(valid, s, -jnp.inf)
            sink = sink_ref[head]
            mx = jnp.maximum(jnp.max(s, axis=-1, keepdims=True), sink)
            e = jnp.exp(s - mx)
            den = jnp.sum(e, axis=-1, keepdims=True) + jnp.exp(sink - mx)
            probs.append(e.astype(BF16))
            inv.append(1.0 / den)
        p_cat = jnp.concatenate(probs, axis=1)
        outs.append(jnp.dot(p_cat, v_cat, preferred_element_type=F32) * jnp.where(lo, inv[0], inv[1]))
    return jnp.concatenate(outs, axis=1)


def _att_lat_kernel(sink_ref, q_ref, kp_ref, kc_ref, kn_ref, vp_ref, vc_ref, vn_ref,
                    kx_ref, vx_ref, cp_ref, cc_ref, cn_ref, sp_ref, sc_ref, sn_ref, o_ref,
                    *, seq_len):
    n = pl.program_id(1) * ATT_QB
    cos_c, sin_c = cc_ref[...], sc_ref[...]
    q = _rope(q_ref[...], jnp.concatenate([cos_c, cos_c], axis=1),
              jnp.concatenate([sin_c, sin_c], axis=1)) * ATT_SCALE
    keys = jnp.concatenate([
        kx_ref[...],
        _rope(kp_ref[...], cp_ref[...], sp_ref[...]),
        _rope(kc_ref[...], cos_c, sin_c),
        _rope(kn_ref[...], cn_ref[...], sn_ref[...])], axis=0)
    vals = jnp.concatenate([vx_ref[...], vp_ref[...], vc_ref[...], vn_ref[...]], axis=0)
    lc = kx_ref.shape[0]
    s_tot = lc + (ATT_QB + 2) * ATT_BLOCK
    col = lax.broadcasted_iota(jnp.int32, (ATT_QB * ATT_BLOCK, s_tot), 1)
    row = lax.broadcasted_iota(jnp.int32, (ATT_QB * ATT_BLOCK, s_tot), 0)
    qpos = n * ATT_BLOCK + row
    kpos = (n - 1) * ATT_BLOCK + (col - lc)
    valid = (col < lc) | ((jnp.abs(kpos - qpos) <= WINDOW) & (kpos >= 0) & (kpos < seq_len))
    o_ref[...] = _attend(q, keys, vals, valid, sink_ref)


def _att_lat(p_lat, p_ctx, cos_t, sin_t, sink):
    b_, L, _ = p_lat.shape
    lc = p_ctx.shape[1]
    nb = L // ATT_BLOCK
    kcol, vcol = COL_TK // 128, COL_TV // 128
    qrows = ATT_QB * ATT_BLOCK
    prev = lambda b, n: jnp.maximum(n * ATT_QB - 1, 0)
    nxt = lambda b, n: jnp.minimum((n + 1) * ATT_QB, nb - 1)
    edge = lambda c, f: pl.BlockSpec((None, ATT_BLOCK, 128), lambda b, n: (b, f(b, n), c))
    cur = lambda c: pl.BlockSpec((None, qrows, 128), lambda b, n: (b, n, c))
    tab_edge = lambda f: pl.BlockSpec((ATT_BLOCK, 128), lambda b, n: (f(b, n), 0))
    tab_cur = lambda: pl.BlockSpec((qrows, 128), lambda b, n: (n, 0))
    return pl.pallas_call(
        functools.partial(_att_lat_kernel, seq_len=L),
        out_shape=jax.ShapeDtypeStruct((b_, L, GROUP_W), F32),
        grid=(b_, nb // ATT_QB),
        in_specs=[
            pl.BlockSpec(memory_space=pltpu.SMEM),
            pl.BlockSpec((None, qrows, 256), lambda b, n: (b, n, COL_TQ // 256)),
            edge(kcol, prev), cur(kcol), edge(kcol, nxt),
            edge(vcol, prev), cur(vcol), edge(vcol, nxt),
            pl.BlockSpec((None, lc, 128), lambda b, n: (b, 0, kcol)),
            pl.BlockSpec((None, lc, 128), lambda b, n: (b, 0, vcol)),
            tab_edge(prev), tab_cur(), tab_edge(nxt), tab_edge(prev), tab_cur(), tab_edge(nxt),
        ],
        out_specs=pl.BlockSpec((None, qrows, GROUP_W), lambda b, n: (b, n, 0)),
        compiler_params=_cparams(("arbitrary", "arbitrary")),
        name="att_latent",
    )(sink, p_lat, p_lat, p_lat, p_lat, p_lat, p_lat, p_lat, p_ctx, p_ctx,
      cos_t, cos_t, cos_t, sin_t, sin_t, sin_t)


def _att_ctx_kernel(sink_ref, q_ref, k_ref, v_ref, o_ref):
    o_ref[...] = _attend(q_ref[...] * ATT_SCALE, k_ref[...], v_ref[...], None, sink_ref)


def _att_ctx(p_ctx, sink):
    b_, lc, _ = p_ctx.shape
    return pl.pallas_call(
        _att_ctx_kernel,
        out_shape=jax.ShapeDtypeStruct((b_, lc, GROUP_W), F32),
        grid=(b_,),
        in_specs=[
            pl.BlockSpec(memory_space=pltpu.SMEM),
            pl.BlockSpec((None, lc, 256), lambda b: (b, 0, COL_TQ // 256)),
            pl.BlockSpec((None, lc, 128), lambda b: (b, 0, COL_TK // 128)),
            pl.BlockSpec((None, lc, 128), lambda b: (b, 0, COL_TV // 128)),
        ],
        out_specs=pl.BlockSpec((None, lc, GROUP_W), lambda b: (b, 0, 0)),
        compiler_params=_cparams(("arbitrary",)),
        name="att_context",
    )(sink, p_ctx, p_ctx, p_ctx)


HG_T = 128
HG_SUB = 4
HG_LEVELS = (64, 32, 16)
_NT = (((1,), (1,)), ((), ()))
_TN = (((0,), (0,)), ((), ()))


def _row_fill(ref, rows, blk):
    w = ref.shape[-1]
    return jnp.concatenate([jnp.broadcast_to(ref[r:r + 1, :], (blk, w)) for r in rows], axis=0)


def _gla_block(q_raw, z, v, lb, st_ref, cum_scr, rev):
    t = HG_T
    e = jnp.exp(-jnp.abs(z))
    big = 1.0 / (1.0 + e)
    small = e * big
    sg = jnp.where(z >= 0, big, small)
    g = jnp.log(lb + (1.0 - lb) * sg)
    k = (1.0 - lb) * jnp.where(z >= 0, small, big)
    q = q_raw * jax.nn.sigmoid(q_raw)

    g_hi = g.astype(BF16)
    r1 = g - g_hi.astype(F32)
    g_mid = r1.astype(BF16)
    g_lo = (r1 - g_mid.astype(F32)).astype(BF16)
    ti = lax.broadcasted_iota(jnp.int32, (t, t), 0)
    si = lax.broadcasted_iota(jnp.int32, (t, t), 1)
    tri = jnp.where((si >= ti) if rev else (si <= ti), 1.0, 0.0).astype(BF16)
    c3 = jnp.dot(tri, jnp.concatenate([g_hi, g_mid, g_lo], axis=1), preferred_element_type=F32)
    cum = c3[:, 0:256] + c3[:, 256:512] + c3[:, 512:768]
    cum_scr[...] = cum
    edge = 0 if rev else t - 1
    g_tot = cum_scr[edge:edge + 1, :]

    lane = lax.broadcasted_iota(jnp.int32, (1, 128), 1)
    lo = lane < HG_DK
    rowi = lax.broadcasted_iota(jnp.int32, (t, 1), 0)
    vi = lax.broadcasted_iota(jnp.int32, (128, 128), 0)
    ki = lax.broadcasted_iota(jnp.int32, (128, 128), 1)
    same_head = (vi < HG_DV) == (ki < HG_DK)

    lo_b = jnp.where(lo, 1.0, 0.0).astype(BF16)
    hi_b = jnp.where(lo, 0.0, 1.0).astype(BF16)

    def scores(qt, kt):
        q2 = jnp.concatenate([qt * lo_b, qt * hi_b], axis=0)
        return lax.dot_general(q2, kt, _NT, preferred_element_type=F32)

    outs = []
    for pr in range(2):
        sl = slice(128 * pr, 128 * (pr + 1))
        qp, kp, vp, cp, gp = q[:, sl], k[:, sl], v[:, sl], cum[:, sl], g_tot[:, sl]
        qb, kb = qp.astype(BF16), kp.astype(BF16)
        a0 = jnp.zeros((t, t), F32)
        a1 = jnp.zeros((t, t), F32)
        for sh, h in zip((6, 5, 4), HG_LEVELS):
            rows = [b * 2 * h + (h if rev else h - 1) for b in range(t // (2 * h))]
            e = jnp.exp(-jnp.abs(cp - _row_fill(cum_scr.at[:, sl], rows, 2 * h)))
            odd = (jnp.right_shift(rowi, sh) & 1) == 1
            q_on = jnp.logical_not(odd) if rev else odd
            sc = scores(qb * jnp.where(q_on, e, 0.0).astype(BF16),
                        kb * jnp.where(q_on, 0.0, e).astype(BF16))
            if 2 * h < t:
                keep = jnp.right_shift(ti, sh + 1) == jnp.right_shift(si, sh + 1)
                a0 += jnp.where(keep, sc[:t], 0.0)
                a1 += jnp.where(keep, sc[t:], 0.0)
            else:
                a0 += sc[:t]
                a1 += sc[t:]
        rows = [b * HG_CHUNK + (HG_CHUNK // 2 if rev else HG_CHUNK // 2 - 1) for b in range(t // HG_CHUNK)]
        dd = cp - _row_fill(cum_scr.at[:, sl], rows, HG_CHUNK)
        sc = scores((qp * jnp.exp(dd)).astype(BF16), (kp * jnp.exp(-dd)).astype(BF16))
        keep = (jnp.right_shift(ti, 4) == jnp.right_shift(si, 4)) & ((si >= ti) if rev else (si <= ti))
        a0 += jnp.where(keep, sc[:t], 0.0)
        a1 += jnp.where(keep, sc[t:], 0.0)

        v2 = jnp.concatenate([jnp.where(lo, vp, 0.0), jnp.where(lo, 0.0, vp)], axis=0).astype(BF16)
        o = jnp.dot(jnp.concatenate([a0, a1], axis=1).astype(BF16), v2, preferred_element_type=F32)
        st = st_ref[pr]
        o += lax.dot_general((qp * jnp.exp(cp)).astype(BF16), st.astype(BF16), _NT,
                             preferred_element_type=F32)
        upd = lax.dot_general(vp.astype(BF16), (kp * jnp.exp(gp - cp)).astype(BF16), _TN,
                              preferred_element_type=F32)
        st_ref[pr] = st * jnp.exp(gp) + jnp.where(same_head, upd, 0.0)
        outs.append(o)
    return jnp.concatenate(outs, axis=1)


def _hgrn2_kernel(qf_ref, zf_ref, vf_ref, qb_ref, zb_ref, vb_ref, lb_ref, s0_ref,
                  of_ref, ob_ref, sl_ref, st_f, st_b, cum_scr):
    j = pl.program_id(1)

    @pl.when(j == 0)
    def _():
        st_f[...] = s0_ref[0]
        st_b[...] = s0_ref[1]

    sub = qf_ref.shape[0] // HG_T
    for i in range(sub):
        rf = slice(HG_T * i, HG_T * (i + 1))
        of_ref[rf, :] = _gla_block(qf_ref[rf, :], zf_ref[rf, :], vf_ref[rf, :], lb_ref[0:1, :],
                                   st_f, cum_scr.at[0, i], False)
        rb = slice(HG_T * (sub - 1 - i), HG_T * (sub - i))
        ob_ref[rb, :] = _gla_block(qb_ref[rb, :], zb_ref[rb, :], vb_ref[rb, :], lb_ref[1:2, :],
                                   st_b, cum_scr.at[1, i], True)
    sl_ref[0] = st_f[...]
    sl_ref[1] = st_b[...]


def _hgrn2(p, lb, s0):
    b_, L, _ = p.shape
    sub = min(HG_SUB, L // HG_T)
    rows = HG_T * sub
    n = L // rows
    fwd = lambda c: pl.BlockSpec((None, rows, GROUP_W), lambda b, j: (b, j, c))
    bwd = lambda c: pl.BlockSpec((None, rows, GROUP_W), lambda b, j: (b, n - 1 - j, c))
    st_spec = pl.BlockSpec((None, 2, 2, 128, 128), lambda b, j: (b, 0, 0, 0, 0))
    o_shape = jax.ShapeDtypeStruct((b_, L, GROUP_W), F32)
    return pl.pallas_call(
        _hgrn2_kernel,
        out_shape=(o_shape, o_shape, jax.ShapeDtypeStruct((b_, 2, 2, 128, 128), F32)),
        grid=(b_, n),
        in_specs=[fwd(0), fwd(1), fwd(3), bwd(0), bwd(2), bwd(3),
                  pl.BlockSpec((2, GROUP_W), lambda b, j: (0, 0)), st_spec],
        out_specs=(fwd(0), bwd(0), st_spec),
        scratch_shapes=[pltpu.VMEM((2, 128, 128), F32), pltpu.VMEM((2, 128, 128), F32),
                        pltpu.VMEM((2, sub, HG_T, GROUP_W), F32)],
        compiler_params=_cparams(("arbitrary", "arbitrary")),
        name="hgrn2",
    )(p, p, p, p, p, p, lb, s0)


S5_TT = 128
S5_NS = S5_GROUPS * S5_STATE
S5_LANES = 256


def _s5_kernel(uf_ref, ub_ref, wb_ref, wc_ref, a_ref, x0_ref, yf_ref, yb_ref, xl_ref,
               buf_f, buf_b, st, y_scr):
    j = pl.program_id(0)

    @pl.when(j == 0)
    def _():
        st[...] = x0_ref[...]

    halves = lambda r: jnp.concatenate([r[0], r[1]], axis=1).astype(BF16)
    buf_f[...] = jnp.dot(halves(uf_ref), wb_ref[0], preferred_element_type=F32)
    buf_b[...] = jnp.dot(halves(ub_ref), wb_ref[1], preferred_element_type=F32)

    for cc in range(S5_NS // S5_LANES):
        re = slice(cc * S5_LANES, (cc + 1) * S5_LANES)
        im = slice(S5_NS + cc * S5_LANES, S5_NS + (cc + 1) * S5_LANES)
        arf, aif, arb, aib = a_ref[0, :, re], a_ref[1, :, re], a_ref[2, :, re], a_ref[3, :, re]

        def step(t, carry, re=re, im=im, arf=arf, aif=aif, arb=arb, aib=aib):
            xrf, xif, xrb, xib = carry
            rf = pl.multiple_of(t * 8, 8)
            rb = pl.multiple_of((S5_TT - 1 - t) * 8, 8)
            nrf = arf * xrf - aif * xif + buf_f[pl.ds(rf, 8), re]
            nif = arf * xif + aif * xrf + buf_f[pl.ds(rf, 8), im]
            nrb = arb * xrb - aib * xib + buf_b[pl.ds(rb, 8), re]
            nib = arb * xib + aib * xrb + buf_b[pl.ds(rb, 8), im]
            buf_f[pl.ds(rf, 8), re] = nrf
            buf_f[pl.ds(rf, 8), im] = nif
            buf_b[pl.ds(rb, 8), re] = nrb
            buf_b[pl.ds(rb, 8), im] = nib
            return nrf, nif, nrb, nib

        fin = lax.fori_loop(0, S5_TT, step,
                            (st[0, :, re], st[1, :, re], st[2, :, re], st[3, :, re]), unroll=4)
        for k in range(4):
            st[k, :, re] = fin[k]

    for y_ref, buf, d in ((yf_ref, buf_f, 0), (yb_ref, buf_b, 1)):
        y = jnp.dot(buf[...].astype(BF16), wc_ref[d], preferred_element_type=F32)
        for c in range(2):
            y_scr[c] = y[:, 128 * c:128 * (c + 1)]
        for b in range(8):
            for c in range(2):
                y_ref[b, :, 128 * c:128 * (c + 1)] = y_scr.at[c][pl.ds(b, S5_TT, stride=8), :]
    xl_ref[...] = st[...]


def _s5_scan(u2, wb, wc, a_bc, x0):
    L = u2.shape[1] // 8
    rows = S5_TT * 8
    n = L // S5_TT
    y_shape = jax.ShapeDtypeStruct((8, L, GROUP_W), F32)
    return pl.pallas_call(
        _s5_kernel,
        out_shape=(y_shape, y_shape, jax.ShapeDtypeStruct((4, 8, S5_NS), F32)),
        grid=(n,),
        in_specs=[
            pl.BlockSpec((2, rows, 128), lambda j: (0, j, 0)),
            pl.BlockSpec((2, rows, 128), lambda j: (0, n - 1 - j, 0)),
            pl.BlockSpec((2, GROUP_W, 2 * S5_NS), lambda j: (0, 0, 0)),
            pl.BlockSpec((2, 2 * S5_NS, GROUP_W), lambda j: (0, 0, 0)),
            pl.BlockSpec((4, 8, S5_NS), lambda j: (0, 0, 0)),
            pl.BlockSpec((4, 8, S5_NS), lambda j: (0, 0, 0)),
        ],
        out_specs=(pl.BlockSpec((8, S5_TT, GROUP_W), lambda j: (0, j, 0)),
                   pl.BlockSpec((8, S5_TT, GROUP_W), lambda j: (0, n - 1 - j, 0)),
                   pl.BlockSpec((4, 8, S5_NS), lambda j: (0, 0, 0))),
        scratch_shapes=[pltpu.VMEM((rows, 2 * S5_NS), F32),
                        pltpu.VMEM((rows, 2 * S5_NS), F32),
                        pltpu.VMEM((4, 8, S5_NS), F32),
                        pltpu.VMEM((2, rows, 128), F32)],
        compiler_params=_cparams(("arbitrary",)),
        name="s5_scan",
    )(u2, u2, wb, wc, a_bc, x0)


def _s5_weights(a_re, a_im, log_step, b_re, b_im, c_re, c_im):
    eye = jnp.eye(S5_GROUPS, dtype=F32)
    wbs, wcs, abc = [], [], []
    for d in range(2):
        ab_re, ab_im, bb_re, bb_im = _s5_discretize(a_re[d], a_im[d], log_step[d], b_re, b_im)
        wb_re = jnp.einsum('gph,gk->ghkp', bb_re, eye).reshape(GROUP_W, S5_NS)
        wb_im = jnp.einsum('gph,gk->ghkp', bb_im, eye).reshape(GROUP_W, S5_NS)
        wbs.append(jnp.concatenate([wb_re, wb_im], axis=1))
        wc_re = jnp.einsum('ghp,gk->gpkh', c_re[d], eye).reshape(S5_NS, GROUP_W)
        wc_im = jnp.einsum('ghp,gk->gpkh', c_im[d], eye).reshape(S5_NS, GROUP_W)
        wcs.append(jnp.concatenate([wc_re, -wc_im], axis=0))
        abc += [jnp.broadcast_to(ab_re.reshape(1, S5_NS), (8, S5_NS)),
                jnp.broadcast_to(ab_im.reshape(1, S5_NS), (8, S5_NS))]
    return jnp.stack(wbs).astype(BF16), jnp.stack(wcs).astype(BF16), jnp.stack(abc)


def _group_norm(m, w):
    return m * lax.rsqrt(jnp.mean(m * m, axis=-1, keepdims=True) + EPS) * w


def _merge_kernel(hgf_ref, hgb_ref, yf_ref, yb_ref, u_ref, sd_ref, glu_ref, hy_ref, at_ref, gate_ref,
                  x_ref, mod_ref, mw_ref, fw_ref, wo_ref, xo_ref, ho_ref):
    gate = gate_ref[...]
    z = jax.nn.gelu(yf_ref[...] + yb_ref[...] + sd_ref[...] * u_ref[...])
    s5 = z * jax.nn.sigmoid(jnp.dot(z.astype(BF16), glu_ref[...], preferred_element_type=F32))
    parts = [
        _group_norm(hgf_ref[...] + hgb_ref[...], mw_ref[:, 0:256]) * (gate * jax.nn.sigmoid(gate)),
        _group_norm(s5, mw_ref[:, 256:512]),
        _group_norm(jnp.concatenate([hy_ref[s].T for s in range(hy_ref.shape[0])], axis=0),
                    mw_ref[:, 512:768]),
        _group_norm(at_ref[...], mw_ref[:, 768:1024]),
    ]
    mix = jnp.concatenate(parts, axis=1).astype(BF16)
    xn = x_ref[...] + mod_ref[2:3, :] * jnp.dot(mix, wo_ref[...], preferred_element_type=F32)
    xo_ref[...] = xn
    h = xn * lax.rsqrt(jnp.mean(xn * xn, axis=-1, keepdims=True) + EPS) * fw_ref[...]
    ho_ref[...] = (h * mod_ref[3:4, :] + mod_ref[4:5, :]).astype(BF16)


def _merge(hg_f, hg_b, s5_yf, s5_yb, s5_d, glu_bf16, hy, at, p, x, mods, mw, fw, wo_bf16, tm):
    b_, L, _ = x.shape
    grp = lambda: pl.BlockSpec((None, tm, GROUP_W), lambda i, b: (b, i, 0))
    pcol = lambda c: pl.BlockSpec((None, tm, GROUP_W), lambda i, b: (b, i, c))
    return pl.pallas_call(
        _merge_kernel,
        out_shape=(jax.ShapeDtypeStruct((b_, L, D_MODEL), F32),
                   jax.ShapeDtypeStruct((b_, L, D_MODEL), BF16)),
        grid=(L // tm, b_),
        in_specs=[
            grp(), grp(), grp(), grp(), pcol(P_COLS // GROUP_W),
            pl.BlockSpec((1, GROUP_W), lambda i, b: (0, 0)),
            pl.BlockSpec((GROUP_W, GROUP_W), lambda i, b: (0, 0)),
            pl.BlockSpec((None, tm // 128, HY_CH, 128), lambda i, b: (b, i, 0, 0)),
            grp(),
            pl.BlockSpec((None, tm, GROUP_W), lambda i, b: (b, i, 4)),
            pl.BlockSpec((None, tm, D_MODEL), lambda i, b: (b, i, 0)),
            pl.BlockSpec((None, 8, D_MODEL), lambda i, b: (b, 0, 0)),
            pl.BlockSpec((1, D_MODEL), lambda i, b: (0, 0)),
            pl.BlockSpec((1, D_MODEL), lambda i, b: (0, 0)),
            pl.BlockSpec((D_MODEL, D_MODEL), lambda i, b: (0, 0)),
        ],
        out_specs=(pl.BlockSpec((None, tm, D_MODEL), lambda i, b: (b, i, 0)),
                   pl.BlockSpec((None, tm, D_MODEL), lambda i, b: (b, i, 0))),
        compiler_params=_cparams(("arbitrary", "arbitrary")),
        name="merge_out_proj",
    )(hg_f, hg_b, s5_yf, s5_yb, p, s5_d.reshape(1, GROUP_W), glu_bf16, hy, at, p, x, mods,
      mw.reshape(1, D_MODEL), fw.reshape(1, D_MODEL), wo_bf16)


def _ffn_kernel(hp_ref, h_ref, hn_ref, x_ref, mod_ref, wu_ref, cw_ref, wd_ref, fw_ref, o_ref,
                acc_scr, *, tm, final_norm):
    i = pl.program_id(1)
    last = pl.num_programs(1) - 1
    h = h_ref[...]
    hp = jnp.where(i > 0, hp_ref[...], jnp.zeros_like(hp_ref))
    hn = jnp.where(i < last, hn_ref[...], jnp.zeros_like(hn_ref))
    h_ext = jnp.concatenate([hp, h, hn], axis=0)
    rows = tm + 2 * HALO

    def gated(j):
        cols = slice(FF_CHUNK * j, FF_CHUNK * (j + 1))
        a = jnp.dot(h_ext, wu_ref[:, cols], preferred_element_type=F32)
        v = jnp.dot(h, wu_ref[:, D_FF + FF_CHUNK * j:D_FF + FF_CHUNK * (j + 1)],
                    preferred_element_type=F32)
        cw = cw_ref[:, cols]
        conv = (pltpu.roll(a, 1, 0)[HALO:HALO + tm] * cw[0:1, :]
                + a[HALO:HALO + tm] * cw[1:2, :]
                + pltpu.roll(a, rows - 1, 0)[HALO:HALO + tm] * cw[2:3, :] + cw[3:4, :])
        return (conv * jax.nn.sigmoid(conv) * v).astype(BF16)

    for j in range(0, N_FF_CHUNKS, 2):
        n = min(2, N_FF_CHUNKS - j)
        g = jnp.concatenate([gated(j + d) for d in range(n)], axis=1)
        part = jnp.dot(g, wd_ref[FF_CHUNK * j:FF_CHUNK * (j + n), :], preferred_element_type=F32)
        if j == 0:
            acc_scr[...] = part
        elif j + n < N_FF_CHUNKS:
            acc_scr[...] += part
        else:
            xo = x_ref[...] + mod_ref[5:6, :] * (acc_scr[...] + part)
            if final_norm:
                xo = xo * lax.rsqrt(jnp.mean(xo * xo, axis=-1, keepdims=True) + EPS) * fw_ref[...]
            o_ref[...] = xo


def _ffn(h, x, mods, w_up, cw, w_down, layer, tm, final_w=None):
    b_, L, _ = x.shape
    nh = L // HALO
    r = tm // HALO
    fw = jnp.ones((1, D_MODEL), F32) if final_w is None else final_w.reshape(1, D_MODEL)
    return pl.pallas_call(
        functools.partial(_ffn_kernel, tm=tm, final_norm=final_w is not None),
        out_shape=jax.ShapeDtypeStruct((b_, L, D_MODEL), F32),
        grid=(b_, L // tm),
        in_specs=[
            pl.BlockSpec((None, HALO, D_MODEL), lambda b, i: (b, jnp.maximum(i * r - 1, 0), 0)),
            pl.BlockSpec((None, tm, D_MODEL), lambda b, i: (b, i, 0)),
            pl.BlockSpec((None, HALO, D_MODEL), lambda b, i: (b, jnp.minimum((i + 1) * r, nh - 1), 0)),
            pl.BlockSpec((None, tm, D_MODEL), lambda b, i: (b, i, 0)),
            pl.BlockSpec((None, 8, D_MODEL), lambda b, i: (b, 0, 0)),
            pl.BlockSpec((None, D_MODEL, 2 * D_FF), lambda b, i: (layer, 0, 0),
                         pipeline_mode=pl.Buffered(1)),
            pl.BlockSpec((8, D_FF), lambda b, i: (0, 0)),
            pl.BlockSpec((None, D_FF, D_MODEL), lambda b, i: (layer, 0, 0),
                         pipeline_mode=pl.Buffered(1)),
            pl.BlockSpec((1, D_MODEL), lambda b, i: (0, 0)),
        ],
        out_specs=pl.BlockSpec((None, tm, D_MODEL), lambda b, i: (b, i, 0)),
        scratch_shapes=[pltpu.VMEM((tm, D_MODEL), F32)],
        compiler_params=_cparams(("arbitrary", "arbitrary")),
        name="conv_ffn",
    )(h, h, h, x, mods, w_up, cw, w_down, fw)


def _s5_discretize(a_re, a_im, log_step, b_re, b_im):
    dt = jnp.exp(log_step)[:, None]
    mag = jnp.exp(a_re * dt)
    ang = a_im * dt
    ab_re, ab_im = mag * jnp.cos(ang), mag * jnp.sin(ang)
    den = a_re * a_re + a_im * a_im
    nr, ni = ab_re - 1.0, ab_im
    fr = (nr * a_re + ni * a_im) / den
    fi = (ni * a_re - nr * a_im) / den
    bb_re = fr[..., None] * b_re - fi[..., None] * b_im
    bb_im = fr[..., None] * b_im + fi[..., None] * b_re
    return ab_re, ab_im, bb_re, bb_im


HY_LANES = 128
HY_CB_LATENT = 16
HY_CB_CONTEXT = 64
HY_EMB_PAD = 128


def _vadd(a, b):
    return b if a is None else a if b is None else a + b


def _vsub(a, b):
    return (None if b is None else -b) if a is None else a if b is None else a - b


def _vscale(a, c):
    return None if (a is None or c == 0.0) else a if c == 1.0 else -a if c == -1.0 else a * c


def _cmul_const(x, wr, wi):
    re, im = x
    return (_vsub(_vscale(re, wr), _vscale(im, wi)), _vadd(_vscale(im, wr), _vscale(re, wi)))


def _unit_root(k, n, sign):
    k %= n
    if (4 * k) % n == 0:
        return ((1.0, 0.0), (0.0, float(sign)), (-1.0, 0.0), (0.0, float(-sign)))[4 * k // n]
    ang = sign * 2.0 * math.pi * k / n
    return math.cos(ang), math.sin(ang)


def _fft_slabs(x, sign, n_out=None):
    n = len(x)
    if n == 1:
        return list(x)
    n_out = n if n_out is None else n_out
    ev = _fft_slabs(x[0::2], sign)
    od = _fft_slabs(x[1::2], sign)
    out = [None] * n_out
    for k in range(n // 2):
        t = _cmul_const(od[k], *_unit_root(k, n, sign))
        if k < n_out:
            out[k] = (_vadd(ev[k][0], t[0]), _vadd(ev[k][1], t[1]))
        if k + n // 2 < n_out:
            out[k + n // 2] = (_vsub(ev[k][0], t[0]), _vsub(ev[k][1], t[1]))
    return out


def _twiddle(slabs, tw_ref, conj):
    out = [slabs[0]]
    for k in range(1, len(slabs)):
        re, im = slabs[k]
        tr = tw_ref[0, k]
        ti = -tw_ref[1, k] if conj else tw_ref[1, k]
        if im is None:
            out.append((re * tr, re * ti))
        else:
            out.append((re * tr - im * ti, re * ti + im * tr))
    return out


def _dot_split(x, w_ref):
    hi = x.astype(BF16)
    return jnp.dot(hi, w_ref[0], preferred_element_type=F32)


def _lane_dft(slabs, w_ref, cb):
    zero = jnp.zeros((cb, HY_LANES), F32)
    rows = jnp.concatenate(
        [jnp.concatenate([zero if re is None else re, zero if im is None else im], axis=1)
         for re, im in slabs], axis=0)
    out = _dot_split(rows, w_ref)
    return [(out[cb * i:cb * (i + 1), :HY_LANES], out[cb * i:cb * (i + 1), HY_LANES:])
            for i in range(len(slabs))]


def _short_conv(ref, part, cw_ref, ch, lane):
    n = ref.shape[1]
    w0, w1, w2, bb = cw_ref[ch, 0], cw_ref[ch, 1], cw_ref[ch, 2], cw_ref[ch, 3]
    right = [pltpu.roll(ref[part, s], 1, 1) for s in range(n)]
    left = [pltpu.roll(ref[part, s], HY_LANES - 1, 1) for s in range(n)]
    zero = jnp.zeros(lane.shape, F32)
    out = []
    for s in range(n):
        xm = jnp.where(lane == 0, right[s - 1] if s > 0 else zero, right[s])
        xp = jnp.where(lane == HY_LANES - 1, left[s + 1] if s < n - 1 else zero, left[s])
        out.append(xm * w0 + ref[part, s] * w1 + xp * w2 + bb)
    return out


def _hy_conv_kernel(x1_ref, x2_ref, z_ref, cw_ref, bias_ref, kf_ref, tw_ref, wf_ref, wi_ref, o_ref,
                    *, n1):
    nh = n1 // 2
    cb = z_ref.shape[2]
    lane = lax.broadcasted_iota(jnp.int32, (cb, HY_LANES), 1)
    gates = [[_short_conv(r, part, cw_ref, ch, lane) for part in range(2)]
             for ch, r in ((0, x1_ref), (1, x2_ref))]
    z = [_short_conv(z_ref, part, cw_ref, 2, lane) for part in range(2)]
    for o in range(HY_ORDER):
        spec = _fft_slabs([(z[0][s], z[1][s]) for s in range(nh)] + [(None, None)] * nh, -1)
        spec = _lane_dft(_twiddle(spec, tw_ref, False), wf_ref, cb)
        spec = [(re * kf_ref[o, 0, k] - im * kf_ref[o, 1, k], re * kf_ref[o, 1, k] + im * kf_ref[o, 0, k])
                for k, (re, im) in enumerate(spec)]
        spec = _twiddle(_lane_dft(spec, wi_ref, cb), tw_ref, True)
        y = _fft_slabs(spec, 1, n_out=nh)
        bo = bias_ref[o]
        z = [[gates[o][part][s] * (y[s][part] + bo * z[part][s]) for s in range(nh)]
             for part in range(2)]
    for part in range(2):
        for s in range(nh):
            o_ref[part, s] = z[part][s]


def _hy_conv(zh, cw, bias_b, kf, tw, wf, wi):
    b_, nh = zh.shape[0], zh.shape[1]
    cb = tw.shape[2]
    half, ncb = b_ // 2, HY_CH // cb
    n1 = 2 * nh
    zspec = lambda ch: pl.BlockSpec((2, None, nh, cb, HY_LANES),
                                    lambda c, b: (0, b, 0, ch * ncb + c, 0))
    const3 = lambda c, b: (0, 0, 0)
    z5 = zh.reshape(2, half, nh, 3 * HY_CH, HY_LANES)
    out = pl.pallas_call(
        functools.partial(_hy_conv_kernel, n1=n1),
        out_shape=jax.ShapeDtypeStruct((2, half, nh, HY_CH, HY_LANES), F32),
        grid=(ncb, half),
        in_specs=[
            zspec(0), zspec(1), zspec(2),
            pl.BlockSpec((3, 4, cb, HY_LANES), lambda c, b: (0, 0, c, 0)),
            pl.BlockSpec((HY_ORDER, cb, HY_LANES), lambda c, b: (0, c, 0)),
            pl.BlockSpec((HY_ORDER, 2, None, n1, cb, HY_LANES), lambda c, b: (0, 0, c, 0, 0, 0)),
            pl.BlockSpec((2, n1, cb, HY_LANES), lambda c, b: (0, 0, 0, 0)),
            pl.BlockSpec((2, 256, 256), const3),
            pl.BlockSpec((2, 256, 256), const3),
        ],
        out_specs=pl.BlockSpec((2, None, nh, cb, HY_LANES), lambda c, b: (0, b, 0, c, 0)),
        compiler_params=_cparams(("arbitrary", "arbitrary")),
        name="hyena_conv",
    )(z5, z5, z5, cw, bias_b, kf, tw, wf, wi)
    return out.reshape(b_, nh, HY_CH, HY_LANES)


def _hy_spectrum_kernel(k_ref, tw_ref, wf_ref, o_ref, *, n1):
    spec = _fft_slabs([(k_ref[s], None) for s in range(n1)], -1)
    spec = _lane_dft(_twiddle(spec, tw_ref, False), wf_ref, k_ref.shape[1])
    scale = 1.0 / (n1 * HY_LANES)
    for k in range(n1):
        o_ref[0, k] = spec[k][0] * scale
        o_ref[1, k] = spec[k][1] * scale


def _hy_spectrum(taps, tw, wf):
    n1, cb = taps.shape[1], tw.shape[2]
    return pl.pallas_call(
        functools.partial(_hy_spectrum_kernel, n1=n1),
        out_shape=jax.ShapeDtypeStruct((HY_ORDER, 2, HY_CH // cb, n1, cb, HY_LANES), F32),
        grid=(HY_ORDER, HY_CH // cb),
        in_specs=[
            pl.BlockSpec((None, n1, cb, HY_LANES), lambda o, c: (o, 0, c, 0)),
            pl.BlockSpec((2, n1, cb, HY_LANES), lambda o, c: (0, 0, 0, 0)),
            pl.BlockSpec((2, 256, 256), lambda o, c: (0, 0, 0)),
        ],
        out_specs=pl.BlockSpec((None, 2, None, n1, cb, HY_LANES), lambda o, c: (o, 0, c, 0, 0, 0)),
        compiler_params=_cparams(("arbitrary", "arbitrary")),
        name="hyena_spectrum",
    )(taps, tw, wf)


def _dot_f32(a, b, dims=(((1,), (0,)), ((), ()))):
    a_hi = a.astype(BF16)
    a_lo = (a - a_hi.astype(F32)).astype(BF16)
    b_hi = b.astype(BF16)
    b_lo = (b - b_hi.astype(F32)).astype(BF16)
    dot = lambda p, q: lax.dot_general(p, q, dims, preferred_element_type=F32)
    return dot(a_hi, b_hi) + dot(a_lo, b_hi) + dot(a_hi, b_lo)


def _hy_mlp_kernel(emb_ref, w1_ref, w2_ref, w3t_ref, vec_ref, pos_ref, dl_ref, o_ref):
    z = emb_ref[...]
    h = jnp.sin(vec_ref[2:3, :] * (_dot_f32(z, w1_ref[...]) + vec_ref[0:1, :]))
    h = jnp.sin(vec_ref[3:4, :] * (_dot_f32(h, w2_ref[...]) + vec_ref[1:2, :]))
    ht = _dot_f32(w3t_ref[...], h, _NT)
    dl = dl_ref[...]
    for s in range(o_ref.shape[1]):
        lanes = slice(HY_LANES * s, HY_LANES * (s + 1))
        win = jnp.exp(-pos_ref[0:1, lanes] * dl)
        wf, wb = pos_ref[1:2, lanes] * win, pos_ref[2:3, lanes] * win
        for o in range(HY_ORDER):
            base = 2 * HY_CH * o
            o_ref[o, s] = (ht[base:base + HY_CH, lanes] * wf
                           + ht[base + HY_CH:base + 2 * HY_CH, lanes] * wb)


def _hy_filter_taps(L, w1, b1, freq, w2, b2, w3):
    pad = HY_EMB_PAD
    n = 2 * L
    t01 = np.linspace(0.0, 1.0, L, dtype=np.float32)[:, None]
    w = (np.float32(2.0 * math.pi) * np.arange(L, dtype=np.float32)[:, None]) / np.float32(L)
    bands = (HY_EMB - 1) // 2
    fr = np.linspace(1e-4, bands - 1, bands, dtype=np.float32)[None, :]
    arg = (fr * w).astype(np.float64)
    emb = np.zeros((L, pad), np.float32)
    emb[:, 0:1] = t01
    emb[:, 1:1 + bands] = np.cos(arg)
    emb[:, 1 + bands:HY_EMB] = -np.sin(arg)
    pos_idx = np.arange(n)
    lag = np.where(pos_idx < L, pos_idx, (n - pos_idx) % L)
    pos = np.zeros((8, n), np.float32)
    pos[0] = t01[lag, 0]
    pos[1] = pos_idx < L
    pos[2] = (pos_idx > L) | (pos_idx == 0)
    hid = w1.shape[1]
    n_out = w3.shape[1]
    w1p = jnp.zeros((pad, pad), F32).at[:HY_EMB, :hid].set(w1)
    w2p = jnp.zeros((pad, pad), F32).at[:hid, :hid].set(w2)
    w3t = jnp.zeros((n_out, pad), F32).at[:, :hid].set(w3.T)
    vec = jnp.zeros((8, pad), F32).at[0, :hid].set(b1).at[1, :hid].set(b2)
    vec = vec.at[2, :hid].set(freq[0]).at[3, :hid].set(freq[1])
    deltas = np.abs(np.linspace(HY_MIN_DECAY, HY_MAX_DECAY, HY_CH, dtype=np.float32))
    dl = np.broadcast_to(deltas[:, None], (HY_CH, HY_LANES))
    tl = min(n, 512)
    return pl.pallas_call(
        _hy_mlp_kernel,
        out_shape=jax.ShapeDtypeStruct((HY_ORDER, n // HY_LANES, HY_CH, HY_LANES), F32),
        grid=(n // tl,),
        in_specs=[
            pl.BlockSpec((tl, pad), lambda i: (i, 0)),
            pl.BlockSpec((pad, pad), lambda i: (0, 0)),
            pl.BlockSpec((pad, pad), lambda i: (0, 0)),
            pl.BlockSpec((n_out, pad), lambda i: (0, 0)),
            pl.BlockSpec((8, pad), lambda i: (0, 0)),
            pl.BlockSpec((8, tl), lambda i: (0, i)),
            pl.BlockSpec((HY_CH, HY_LANES), lambda i: (0, 0)),
        ],
        out_specs=pl.BlockSpec((HY_ORDER, tl // HY_LANES, HY_CH, HY_LANES), lambda i: (0, i, 0, 0)),
        compiler_params=_cparams(("arbitrary",)),
        name="hyena_filter_mlp",
    )(jnp.asarray(emb[lag]), w1p, w2p, w3t, vec, jnp.asarray(pos), jnp.asarray(dl))


def _hy_constants(n1, cb):
    n = n1 * HY_LANES
    ang = -2.0 * np.pi * np.outer(np.arange(n1), np.arange(HY_LANES)) / n
    tw = np.stack([np.cos(ang), np.sin(ang)])[:, :, None, :].repeat(cb, axis=2).astype(np.float32)
    a2 = -2.0 * np.pi * np.outer(np.arange(HY_LANES), np.arange(HY_LANES)) / HY_LANES
    fr, fi = np.cos(a2), np.sin(a2)
    fwd = np.block([[fr, fi], [-fi, fr]])
    inv = np.block([[fr, -fi], [fi, fr]])

    def split(m):
        hi = m.astype(BF16)
        lo = (m - hi.astype(np.float64)).astype(BF16)
        return jnp.asarray(np.stack([hi, lo]))

    return jnp.asarray(tw), split(fwd), split(inv)


def _regroup_in_proj(w):
    d = w.shape[0]
    half = ATT_HD // 2
    q = w[:, W_COL_TQ:W_COL_TK].reshape(d, ATT_KV, 2, half, 2)
    q = q.transpose(0, 2, 1, 4, 3).reshape(d, ATT_HEADS * ATT_HD)
    k = w[:, W_COL_TK:W_COL_TV].reshape(d, ATT_KV, half, 2).transpose(0, 1, 3, 2).reshape(d, ATT_KV * ATT_HD)
    return jnp.concatenate([w[:, :W_COL_S5], q, k, w[:, W_COL_TV:], w[:, W_COL_S5:W_COL_HY]], axis=1)


def _regroup_att_rows(m):
    at = m[3 * GROUP_W:].reshape((2, 2, ATT_HD) + m.shape[1:])
    at = jnp.swapaxes(at, 0, 1).reshape((GROUP_W,) + m.shape[1:])
    return jnp.concatenate([m[:3 * GROUP_W], at], axis=0)


def _rope_tables(L):
    rows = L // GRID_W
    row = jnp.repeat(jnp.arange(rows), GRID_W).astype(F32)
    col = jnp.tile(jnp.arange(GRID_W), rows).astype(F32)
    axis_dim = ATT_HD // 2
    inv = 1.0 / (ROPE_BASE ** (jnp.arange(0, axis_dim, 2, dtype=F32) / axis_dim))
    ang = jnp.concatenate([row[:, None] * inv, col[:, None] * inv], axis=-1)
    c, s = jnp.cos(ang), jnp.sin(ang)
    return jnp.tile(jnp.concatenate([c, c], axis=1), (1, 2)), jnp.tile(jnp.concatenate([-s, s], axis=1), (1, 2))


def kernel(x, c, ctx, c_ctx, ada_w, ada_b, norm_mix_w, norm_ffn_w, w_in, hg_lb_logits, s5_a_re, s5_a_im, s5_log_step, s5_b_re, s5_b_im, s5_c_re, s5_c_im, s5_d, s5_glu_w, hy_conv_w, hy_conv_b, hy_w1, hy_b1, hy_freq, hy_w2, hy_b2, hy_w3, hy_bias, att_sink, merge_norm_w, w_out, ffn_w_up, ffn_conv_w, ffn_conv_b, ffn_w_down, final_norm_w):
    bsz, seq_len, _ = x.shape
    ctx_len = ctx.shape[1]

    cond = jnp.zeros((16, D_MODEL), F32).at[:bsz].set(c).at[bsz].set(c_ctx)
    ada = _ada_all(cond, ada_w, ada_b)

    def mods_of(m):
        sh1, sc1, g1, sh2, sc2, g2 = jnp.split(m, 6, axis=-1)
        z = jnp.zeros_like(sh1)
        return jnp.stack([1.0 + sc1, sh1, g1, 1.0 + sc2, sh2, g2, z, z], axis=1)

    sm = jax.nn.softmax(hg_lb_logits, axis=0)
    lower_bounds = jnp.cumsum(sm, axis=0) - sm[0:1]
    cos_t, sin_t = _rope_tables(seq_len)
    hg_zero = jnp.zeros((bsz, 2, 2, 128, 128), F32)
    hy_tw_lat, hy_wf, hy_wi = _hy_constants(2 * seq_len // HY_LANES, HY_CB_LATENT)
    hy_tw_ctx, _, _ = _hy_constants(2 * ctx_len // HY_LANES, HY_CB_CONTEXT)
    s5_zero = jnp.zeros((4, bsz, S5_NS), F32)
    w_up_bf = ffn_w_up.astype(BF16)
    w_down_bf = ffn_w_down.astype(BF16)

    xc = ctx
    for l in range(DEPTH):
        last = l == DEPTH - 1
        mods_lat = mods_of(ada[l, :bsz])
        mods_ctx = jnp.broadcast_to(mods_of(ada[l, bsz:bsz + 1]), (bsz, 8, D_MODEL))
        w_in_l = _regroup_in_proj(w_in[l]).astype(BF16)
        mw = _regroup_att_rows(merge_norm_w[l])
        wo = _regroup_att_rows(w_out[l]).astype(BF16)
        cw = jnp.concatenate([ffn_conv_w[l], ffn_conv_b[l][None], jnp.zeros((4, D_FF), F32)], axis=0)

        wh = w_in[l][:, W_COL_HY:W_COL_HY + 3 * HY_CH].T.astype(BF16)
        p_lat, u_lat, zh_lat = _inproj(x, mods_lat, norm_mix_w[l], w_in_l, wh, 512)
        p_ctx, u_ctx, zh_ctx = _inproj(xc, mods_ctx, norm_mix_w[l], w_in_l, wh, ctx_len)

        hgf_ctx, hgb_ctx, hg_state = _hgrn2(p_ctx, lower_bounds[l], hg_zero)
        hgf_lat, hgb_lat, _ = _hgrn2(p_lat, lower_bounds[l], hg_state)

        s5_wb, s5_wc, s5_a = _s5_weights(s5_a_re[l], s5_a_im[l], s5_log_step[l], s5_b_re[l],
                                         s5_b_im[l], s5_c_re[l], s5_c_im[l])
        glu = s5_glu_w[l].astype(BF16)
        yf_ctx, yb_ctx, s5_state = _s5_scan(u_ctx, s5_wb, s5_wc, s5_a, s5_zero)
        yf_lat, yb_lat, _ = _s5_scan(u_lat, s5_wb, s5_wc, s5_a, s5_state)

        hy_params = (hy_w1[l], hy_b1[l], hy_freq[l], hy_w2[l], hy_b2[l], hy_w3[l])
        hy_cw = jnp.concatenate([hy_conv_w[l], hy_conv_b[l][None]], axis=0)
        hy_cw = jnp.broadcast_to(hy_cw.reshape(4, 3, HY_CH).transpose(1, 0, 2)[..., None],
                                 (3, 4, HY_CH, HY_LANES))
        hy_bb = jnp.broadcast_to(hy_bias[l][..., None], (HY_ORDER, HY_CH, HY_LANES))
        kf_lat = _hy_spectrum(_hy_filter_taps(seq_len, *hy_params), hy_tw_lat, hy_wf)
        hy_lat = _hy_conv(zh_lat, hy_cw, hy_bb, kf_lat, hy_tw_lat, hy_wf, hy_wi)

        at_lat = _att_lat(p_lat, p_ctx, cos_t, sin_t, att_sink[l])

        x_mid, h_lat = _merge(hgf_lat, hgb_lat, yf_lat, yb_lat, s5_d[l], glu, hy_lat, at_lat, p_lat, x,
                              mods_lat, mw, norm_ffn_w[l], wo, 512)
        x_new = _ffn(h_lat, x_mid, mods_lat, w_up_bf, cw, w_down_bf, l, 512,
                     final_w=final_norm_w if last else None)
        if not last:
            kf_ctx = _hy_spectrum(_hy_filter_taps(ctx_len, *hy_params), hy_tw_ctx, hy_wf)
            hy_ctx = _hy_conv(zh_ctx, hy_cw, hy_bb, kf_ctx, hy_tw_ctx, hy_wf, hy_wi)
            at_ctx = _att_ctx(p_ctx, att_sink[l])
            xc_mid, h_ctx = _merge(hgf_ctx, hgb_ctx, yf_ctx, yb_ctx, s5_d[l], glu, hy_ctx, at_ctx, p_ctx,
                                   xc, mods_ctx, mw, norm_ffn_w[l], wo, ctx_len)
            xc = _ffn(h_ctx, xc_mid, mods_ctx, w_up_bf, cw, w_down_bf, l, ctx_len)
        x = x_new
    return x
```
